```python
import math
import jax
import jax.numpy as jnp
from jax import lax
import numpy as np

D_MODEL = 1024
BATCH = 1
SEQ = 16384
DEPTH = 2
DEC_BATCH = 2
DEC_SEQ = 8192
PAST_LEN = 128

N_EVEN = (DEPTH + 1) // 2
N_ODD = DEPTH // 2
RMS_EPS = 1e-6

HY_ORDER = 2
HY_SHORT = 3
HY_EMB = 33
HY_BANDS = (HY_EMB - 1) // 2
HY_FHID = 64
HY_FAST_PCT = 0.3
HY_SLOW_PCT = 1.5
HY_TARGET = 1e-2
HY_FILTER_SCALE = 0.02

GDN_NK = 8
GDN_NV = 16
GDN_DK = 128
GDN_DV = 128
GDN_KEY = GDN_NK * GDN_DK
GDN_VAL = GDN_NV * GDN_DV
GDN_CONV_DIM = 2 * GDN_KEY + GDN_VAL
GDN_IN = GDN_CONV_DIM + GDN_VAL + 4 * GDN_NV
GDN_SHORT = 3
GDN_CHUNK = 64

D_FF = 3584
N_EXPERTS = 8
TOP_K = 2

kernel_name = "hyena_gdn_hybrid_encoder"


def rms_norm(x, w):
    xf = x.astype(jnp.float32)
    y = xf * lax.rsqrt(jnp.mean(xf * xf, axis=-1, keepdims=True) + RMS_EPS)
    return (y * w.astype(jnp.float32)).astype(x.dtype)


def l2_normalize(x):
    return x * lax.rsqrt(jnp.sum(x * x, axis=-1, keepdims=True) + RMS_EPS)


def centred_dwconv(x, w):
    k = w.shape[0]
    return lax.conv_general_dilated(
        x, w.astype(x.dtype)[:, None, :], window_strides=(1,),
        padding=[(k // 2, k // 2)], dimension_numbers=("NWC", "WIO", "NWC"),
        feature_group_count=x.shape[-1])


def hyena_filter_spectra(seq_len, f_w1, f_b1, f_w2, f_b2, f_w3, f_b3, f_w4, f_freq):
    f32 = jnp.float32
    pos = jnp.arange(seq_len, dtype=f32)
    t = jnp.linspace(0.0, 1.0, seq_len, dtype=f32)
    omega = (2.0 * math.pi / seq_len) * pos
    bands = jnp.linspace(1e-4, HY_BANDS - 1, HY_BANDS, dtype=f32)
    ang = omega[:, None] * bands[None, :]
    feats = jnp.concatenate([t[:, None], jnp.cos(ang), -jnp.sin(ang)], axis=-1)
    freq = f_freq.astype(f32)
    h = jnp.sin(freq * (feats @ f_w1.astype(f32) + f_b1.astype(f32)))
    h = jnp.sin(freq * (h @ f_w2.astype(f32) + f_b2.astype(f32)))
    h = jnp.sin(freq * (h @ f_w3.astype(f32) + f_b3.astype(f32)))
    h = (h @ f_w4.astype(f32)).reshape(seq_len, HY_ORDER, 2, D_MODEL)
    max_decay = math.log(HY_TARGET) / HY_FAST_PCT
    min_decay = math.log(HY_TARGET) / HY_SLOW_PCT
    deltas = jnp.abs(jnp.linspace(min_decay, max_decay, D_MODEL, dtype=f32))
    window = jnp.exp(-t[:, None] * deltas[None, :])
    h = h * window[:, None, None, :]
    fwd, bwd = h[:, :, 0], h[:, :, 1]
    taps = jnp.concatenate([fwd, jnp.zeros_like(fwd[:1]), bwd[:0:-1]], axis=0)
    return jnp.fft.rfft(taps, axis=0)


def long_conv(u, k_spec, skip):
    L = u.shape[1]
    u_spec = jnp.fft.rfft(u, n=2 * L, axis=1)
    y = jnp.fft.irfft(u_spec * k_spec[None], n=2 * L, axis=1)[:, :L]
    return y + u * skip


def hyena_mixer(x, w_in, conv_w, f_w1, f_b1, f_w2, f_b2, f_w3, f_b3, f_w4, f_freq, skip, w_out):
    L = x.shape[1]
    u = centred_dwconv(x @ w_in, conv_w).astype(jnp.float32)
    v, gate1, gate2 = jnp.split(u, 3, axis=-1)
    k_spec = hyena_filter_spectra(L, f_w1, f_b1, f_w2, f_b2, f_w3, f_b3, f_w4, f_freq)
    skip = skip.astype(jnp.float32)
    z = v
    for o, gate in enumerate((gate1, gate2)):
        z = gate * long_conv(z, k_spec[:, o], skip[o])
    return z.astype(x.dtype) @ w_out


def chunk_gated_delta_rule(q, k, v, g, beta):
    B, L, H, DK = q.shape
    DV = v.shape[-1]
    C = GDN_CHUNK
    N = L // C
    def to_chunks(t):
        return jnp.swapaxes(t, 1, 2).reshape(B, H, N, C, t.shape[-1])
    q, k, v = to_chunks(q), to_chunks(k), to_chunks(v)
    g = jnp.swapaxes(g, 1, 2).reshape(B, H, N, C)
    beta = jnp.swapaxes(beta, 1, 2).reshape(B, H, N, C)
    g = jnp.cumsum(g, axis=-1)
    tril = jnp.tril(jnp.ones((C, C), dtype=bool))
    strict = jnp.tril(jnp.ones((C, C), dtype=bool), k=-1)
    diff = g[..., :, None] - g[..., None, :]
    decay = jnp.where(tril, jnp.exp(jnp.where(tril, diff, 0.0)), 0.0)
    kb = k * beta[..., None]
    a_mat = jnp.where(strict, jnp.einsum('bhncd,bhnsd->bhncs', kb, k) * decay, 0.0)
    eye = jnp.eye(C, dtype=q.dtype)
    t_mat = lax.linalg.triangular_solve(eye + a_mat, jnp.broadcast_to(eye, a_mat.shape),
                                        left_side=True, lower=True, unit_diagonal=True)
    u = jnp.einsum('bhncs,bhnsv->bhncv', t_mat, v * beta[..., None])
    w = jnp.einsum('bhncs,bhnsd->bhncd', t_mat, kb * jnp.exp(g)[..., None])
    qk = jnp.where(tril, jnp.einsum('bhncd,bhnsd->bhncs', q, k) * decay, 0.0)
    q_g = q * jnp.exp(g)[..., None]
    k_g = k * jnp.exp(g[..., -1:] - g)[..., None]
    g_last = jnp.exp(g[..., -1])

    def step(state, inp):
        u_c, w_c, qk_c, qg_c, kg_c, gl_c = inp
        v_new = u_c - jnp.einsum('bhck,bhkv->bhcv', w_c, state)
        o_c = jnp.einsum('bhck,bhkv->bhcv', qg_c, state) + jnp.einsum('bhcs,bhsv->bhcv', qk_c, v_new)
        state = state * gl_c[..., None, None] + jnp.einsum('bhck,bhcv->bhkv', kg_c, v_new)
        return state, o_c

    xs = tuple(jnp.moveaxis(t, 2, 0) for t in (u, w, qk, q_g, k_g, g_last))
    state0 = jnp.zeros((B, H, DK, DV), q.dtype)
    _, o = lax.scan(step, state0, xs)
    return jnp.transpose(o, (1, 0, 3, 2, 4)).reshape(B, L, H, DV)


def gated_deltanet_mixer(x, w_in, conv_w, a_log, dt_bias, norm_w, w_out):
    B, L, _ = x.shape
    f32 = jnp.float32
    p = x @ w_in
    qkv = jax.nn.silu(centred_dwconv(p[..., :GDN_CONV_DIM], conv_w)).astype(f32)
    z = p[..., GDN_CONV_DIM:GDN_CONV_DIM + GDN_VAL].astype(f32).reshape(B, L, GDN_NV, GDN_DV)
    ba = p[..., GDN_CONV_DIM + GDN_VAL:].astype(f32).reshape(B, L, 2, 2, GDN_NV)
    rep = GDN_NV // GDN_NK
    q = jnp.repeat(l2_normalize(qkv[..., :GDN_KEY].reshape(B, L, GDN_NK, GDN_DK)), rep, axis=2) * (GDN_DK ** -0.5)
    k = jnp.repeat(l2_normalize(qkv[..., GDN_KEY:2 * GDN_KEY].reshape(B, L, GDN_NK, GDN_DK)), rep, axis=2)
    v = qkv[..., 2 * GDN_KEY:].reshape(B, L, GDN_NV, GDN_DV)
    beta = jax.nn.sigmoid(ba[:, :, 0])
    g = -jnp.exp(a_log.astype(f32)) * jax.nn.softplus(ba[:, :, 1] + dt_bias.astype(f32))
    o_fwd = chunk_gated_delta_rule(q, k, v, g[:, :, 0], beta[:, :, 0])
    flip = lambda t: jnp.flip(t, axis=1)
    o_bwd = flip(chunk_gated_delta_rule(flip(q), flip(k), flip(v), flip(g[:, :, 1]), flip(beta[:, :, 1])))
    o = o_fwd + o_bwd
    o = o * lax.rsqrt(jnp.mean(o * o, axis=-1, keepdims=True) + RMS_EPS) * norm_w.astype(f32)
    o = (o * jax.nn.silu(z)).reshape(B, L, GDN_VAL).astype(x.dtype)
    return o @ w_out


def swiglu(x, w_gate, w_up, w_down):
    return (jax.nn.silu(x @ w_gate) * (x @ w_up)) @ w_down


def moe_swiglu(x, router, router_bias, w_gate, w_up, w_down):
    logits = (x @ router).astype(jnp.float32) + router_bias.astype(jnp.float32)
    top_val, top_idx = lax.top_k(logits, TOP_K)
    top_w = jax.nn.softmax(top_val, axis=-1)
    gates = jnp.sum(jax.nn.one_hot(top_idx, N_EXPERTS, dtype=jnp.float32) * top_w[..., None], axis=-2)
    y = jnp.zeros(x.shape, jnp.float32)
    for e in range(N_EXPERTS):
        y = y + gates[..., e:e + 1] * swiglu(x, w_gate[e], w_up[e], w_down[e]).astype(jnp.float32)
    return y.astype(x.dtype)


def setup_inputs(seed: int = 0) -> dict:
    key = jax.random.key(seed)
    ks = iter(jax.random.split(key, 40))
    f32 = jnp.float32

    def nrm(shape, scale):
        return jax.random.normal(next(ks), shape, f32) * scale

    def gain(shape):
        return 1.0 + 0.02 * jax.random.normal(next(ks), shape, f32)

    dt = jnp.exp(jax.random.uniform(next(ks), (N_ODD, 2, GDN_NV), f32, math.log(1e-3), math.log(1e-1)))
    a_val = jax.random.uniform(next(ks), (N_ODD, 2, GDN_NV), f32, 1.0, 16.0)
    return {
        "x_prompt": nrm((BATCH, SEQ, D_MODEL), 1.0),
        "x_sample": nrm((DEC_BATCH, DEC_SEQ, D_MODEL), 1.0),
        "norm_mix": gain((DEPTH, D_MODEL)),
        "norm_ffn": gain((DEPTH, D_MODEL)),
        "norm_final": gain((D_MODEL,)),
        "hy_w_in": nrm((N_EVEN, D_MODEL, 3 * D_MODEL), D_MODEL ** -0.5),
        "hy_conv": nrm((N_EVEN, HY_SHORT, 3 * D_MODEL), HY_SHORT ** -0.5),
        "hy_f_w1": nrm((N_EVEN, HY_EMB, HY_FHID), HY_EMB ** -0.5),
        "hy_f_b1": nrm((N_EVEN, HY_FHID), 0.1),
        "hy_f_w2": nrm((N_EVEN, HY_FHID, HY_FHID), HY_FHID ** -0.5),
        "hy_f_b2": nrm((N_EVEN, HY_FHID), 0.1),
        "hy_f_w3": nrm((N_EVEN, HY_FHID, HY_FHID), HY_FHID ** -0.5),
        "hy_f_b3": nrm((N_EVEN, HY_FHID), 0.1),
        "hy_f_w4": nrm((N_EVEN, HY_FHID, HY_ORDER * 2 * D_MODEL), HY_FILTER_SCALE * HY_FHID ** -0.5),
        "hy_f_freq": gain((N_EVEN, HY_FHID)),
        "hy_skip": nrm((N_EVEN, HY_ORDER, D_MODEL), 1.0),
        "hy_w_out": nrm((N_EVEN, D_MODEL, D_MODEL), D_MODEL ** -0.5),
        "gdn_w_in": nrm((N_ODD, D_MODEL, GDN_IN), D_MODEL ** -0.5),
        "gdn_conv": nrm((N_ODD, GDN_SHORT, GDN_CONV_DIM), GDN_SHORT ** -0.5),
        "gdn_a_log": jnp.log(a_val),
        "gdn_dt_bias": dt + jnp.log(-jnp.expm1(-dt)),
        "gdn_norm": gain((N_ODD, GDN_DV)),
        "gdn_w_out": nrm((N_ODD, GDN_VAL, D_MODEL), GDN_VAL ** -0.5),
        "ffn_w_gate": nrm((N_EVEN, D_MODEL, D_FF), D_MODEL ** -0.5),
        "ffn_w_up": nrm((N_EVEN, D_MODEL, D_FF), D_MODEL ** -0.5),
        "ffn_w_down": nrm((N_EVEN, D_FF, D_MODEL), D_FF ** -0.5),
        "moe_router": nrm((N_ODD, D_MODEL, N_EXPERTS), D_MODEL ** -0.5),
        "moe_router_bias": nrm((N_ODD, N_EXPERTS), 0.01),
        "moe_w_gate": nrm((N_ODD, N_EXPERTS, D_MODEL, D_FF), D_MODEL ** -0.5),
        "moe_w_up": nrm((N_ODD, N_EXPERTS, D_MODEL, D_FF), D_MODEL ** -0.5),
        "moe_w_down": nrm((N_ODD, N_EXPERTS, D_FF, D_MODEL), D_FF ** -0.5),
    }


def reference(x_prompt, x_sample, norm_mix, norm_ffn, norm_final,
              hy_w_in, hy_conv, hy_f_w1, hy_f_b1, hy_f_w2, hy_f_b2, hy_f_w3, hy_f_b3,
              hy_f_w4, hy_f_freq, hy_skip, hy_w_out,
              gdn_w_in, gdn_conv, gdn_a_log, gdn_dt_bias, gdn_norm, gdn_w_out,
              ffn_w_gate, ffn_w_up, ffn_w_down,
              moe_router, moe_router_bias, moe_w_gate, moe_w_up, moe_w_down):
    hy = (hy_w_in, hy_conv, hy_f_w1, hy_f_b1, hy_f_w2, hy_f_b2, hy_f_w3, hy_f_b3,
          hy_f_w4, hy_f_freq, hy_skip, hy_w_out)
    gdn = (gdn_w_in, gdn_conv, gdn_a_log, gdn_dt_bias, gdn_norm, gdn_w_out)
    ffn = (ffn_w_gate, ffn_w_up, ffn_w_down)
    moe = (moe_router, moe_router_bias, moe_w_gate, moe_w_up, moe_w_down)

    def run(x):
        for i in range(DEPTH):
            j = i // 2
            h = rms_norm(x, norm_mix[i])
            if i % 2 == 0:
                x = x + hyena_mixer(h, *[p[j] for p in hy])
            else:
                x = x + gated_deltanet_mixer(h, *[p[j] for p in gdn])
            h = rms_norm(x, norm_ffn[i])
            if i % 2 == 0:
                x = x + swiglu(h, *[p[j] for p in ffn])
            else:
                x = x + moe_swiglu(h, *[p[j] for p in moe])
        return rms_norm(x, norm_final)

    y_prompt = run(x_prompt)
    y_sample = run(x_sample)
    return (y_prompt, y_sample)
```

```python
import functools
import math

import numpy as np
import jax
import jax.numpy as jnp
from jax import lax
from jax.experimental import pallas as pl
from jax.experimental.pallas import tpu as pltpu

F32 = jnp.float32
BF16 = jnp.bfloat16

RMS_EPS = 1e-6
HY_BANDS = 16
HY_FAST_PCT = 0.3
HY_SLOW_PCT = 1.5
HY_TARGET = 1e-2
GDN_NK = 8
GDN_NV = 16
GDN_DH = 128
TOP_K = 2

LANES = 128
HALO = 16
DFT_N1 = 256
GDN_CHUNK = 64
GDN_TILE = 256
VMEM_LIMIT = 56 * 1024 * 1024

HIGHEST = lax.Precision.HIGHEST


def _cparams(sem):
    return pltpu.CompilerParams(dimension_semantics=sem, vmem_limit_bytes=VMEM_LIMIT)


def _in_set(v, values):
    r = v == values[0]
    for b in values[1:]:
        r = jnp.logical_or(r, v == b)
    return r


def _norm_rows(x, g):
    ms = jnp.mean(x * x, axis=-1, keepdims=True)
    return x * lax.rsqrt(ms + RMS_EPS) * g


def _proj_conv_kernel(xp_ref, x_ref, xn_ref, g_ref, w_ref, cw_ref, o_ref, h_ref, *,
                      tm, starts, ends, act, transpose_out):
    i = pl.program_id(0)

    @pl.when(pl.program_id(1) == 0)
    def _():
        g = g_ref[...]
        row0 = i * tm
        hp = jnp.where(_in_set(row0, starts), 0.0, _norm_rows(xp_ref[...], g))
        hn = jnp.where(_in_set(row0 + tm, ends), 0.0, _norm_rows(xn_ref[...], g))
        h_ref[0:HALO, :] = hp.astype(BF16)
        h_ref[HALO:HALO + tm, :] = _norm_rows(x_ref[...], g).astype(BF16)
        h_ref[HALO + tm:HALO + tm + HALO, :] = hn.astype(BF16)

    p = jnp.dot(h_ref[...], w_ref[...], preferred_element_type=F32)
    n = tm + 2 * HALO
    up = pltpu.roll(p, 1, axis=0)
    dn = pltpu.roll(p, n - 1, axis=0)
    cw = cw_ref[...]
    y = (cw[0:1, :] * up[HALO:HALO + tm, :] + cw[1:2, :] * p[HALO:HALO + tm, :]
         + cw[2:3, :] * dn[HALO:HALO + tm, :])
    if act == "silu":
        y = y * jax.nn.sigmoid(y)
    o_ref[...] = y.T if transpose_out else y


def _proj_kernel(x_ref, g_ref, w_ref, o_ref, h_ref):
    @pl.when(pl.program_id(1) == 0)
    def _():
        h_ref[...] = _norm_rows(x_ref[...], g_ref[...]).astype(BF16)

    o_ref[...] = jnp.dot(h_ref[...], w_ref[...], preferred_element_type=F32)


def _proj(x, gain, w, conv_w=None, *, act=None, transpose_out=False, starts=(), ends=(),
          tm=1024, tn=512):
    m, d = x.shape
    n = w.shape[1]
    tm = min(tm, m)
    tn = min(tn, n)
    grid = (m // tm, n // tn)
    gain = gain.reshape(1, d).astype(F32)
    w = w.astype(BF16)
    if conv_w is None:
        return pl.pallas_call(
            _proj_kernel,
            grid=grid,
            in_specs=[pl.BlockSpec((tm, d), lambda i, j: (i, 0)),
                      pl.BlockSpec((1, d), lambda i, j: (0, 0)),
                      pl.BlockSpec((d, tn), lambda i, j: (0, j))],
            out_specs=pl.BlockSpec((tm, tn), lambda i, j: (i, j)),
            out_shape=jax.ShapeDtypeStruct((m, n), F32),
            scratch_shapes=[pltpu.VMEM((tm, d), BF16)],
            compiler_params=_cparams(("parallel", "arbitrary")),
            name="proj",
        )(x, gain, w)
    hb = tm // HALO
    last = m // HALO - 1
    kern = functools.partial(_proj_conv_kernel, tm=tm, starts=tuple(starts), ends=tuple(ends),
                             act=act, transpose_out=transpose_out)
    if transpose_out:
        out_spec = pl.BlockSpec((tn, tm), lambda i, j: (j, i))
        out_shape = jax.ShapeDtypeStruct((n, m), F32)
    else:
        out_spec = pl.BlockSpec((tm, tn), lambda i, j: (i, j))
        out_shape = jax.ShapeDtypeStruct((m, n), F32)
    return pl.pallas_call(
        kern,
        grid=grid,
        in_specs=[pl.BlockSpec((HALO, d), lambda i, j: (jnp.maximum(i * hb - 1, 0), 0)),
                  pl.BlockSpec((tm, d), lambda i, j: (i, 0)),
                  pl.BlockSpec((HALO, d), lambda i, j: (jnp.minimum((i + 1) * hb, last), 0)),
                  pl.BlockSpec((1, d), lambda i, j: (0, 0)),
                  pl.BlockSpec((d, tn), lambda i, j: (0, j)),
                  pl.BlockSpec((3, tn), lambda i, j: (0, j))],
        out_specs=out_spec,
        out_shape=out_shape,
        scratch_shapes=[pltpu.VMEM((tm + 2 * HALO, d), BF16)],
        compiler_params=_cparams(("parallel", "arbitrary")),
        name="proj_conv",
    )(x, x, x, gain, w, conv_w.astype(F32))


@functools.lru_cache(maxsize=None)
def _dft_consts(n2):
    n1 = DFT_N1
    n = n1 * n2
    h = n2 // 2
    k1 = np.arange(n1)
    f1 = np.exp(-2j * np.pi * np.outer(k1, k1) / n1)
    f1_fwd = np.block([[f1.real, f1.imag], [-f1.imag, f1.real]])
    f1_inv = np.block([[f1.real, -f1.imag], [f1.imag, f1.real]])
    k2 = np.arange(n2)
    f2 = np.exp(-2j * np.pi * np.outer(k2, k2) / n2)
    tw = np.exp(-2j * np.pi * np.outer(k2, k1) / n)
    lf_half = np.concatenate([f2.real[:, :h], f2.imag[:, :h]], axis=0)
    lf_full = np.concatenate([f2.real, f2.imag], axis=0)
    li_half = np.concatenate([f2.real[:h, :], f2.imag[:h, :]], axis=1) / n
    return dict(f1_fwd=f1_fwd, f1_inv=f1_inv, twre=tw.real, twim=tw.imag,
                lf_half=lf_half, lf_full=lf_full, li_half=li_half)


def _consts_dev(n2):
    c = _dft_consts(n2)
    return dict(
        f1_fwd=jnp.asarray(c["f1_fwd"], BF16), f1_inv=jnp.asarray(c["f1_inv"], BF16),
        twre=jnp.asarray(c["twre"], F32), twim=jnp.asarray(c["twim"], F32),
        lf_half=jnp.asarray(c["lf_half"], BF16), lf_full=jnp.asarray(c["lf_full"], BF16),
        li_half=jnp.asarray(c["li_half"], BF16))


def _dft_fwd(sig, lf, twre, twim, f1f, a_scr):
    ns, two_n2, _ = a_scr.shape
    n2 = two_n2 // 2
    for s in range(ns):
        a_scr[s] = jnp.dot(lf, sig(s).astype(BF16), preferred_element_type=F32)
    are = a_scr[:, 0:n2, :]
    aim = a_scr[:, n2:two_n2, :]
    bre = (are * twre - aim * twim).reshape(ns * n2, DFT_N1)
    bim = (are * twim + aim * twre).reshape(ns * n2, DFT_N1)
    bcat = jnp.concatenate([bre, bim], axis=1).astype(BF16)
    cc = jnp.dot(bcat, f1f, preferred_element_type=F32)
    return cc[:, 0:DFT_N1], cc[:, DFT_N1:2 * DFT_N1]


def _dft_inv_real(yre, yim, li, twre, twim, f1i, d_scr, emit):
    ns, two_n2, _ = d_scr.shape
    n2 = two_n2 // 2
    ycat = jnp.concatenate([yre, yim], axis=1).astype(BF16)
    bb = jnp.dot(ycat, f1i, preferred_element_type=F32)
    bre = bb[:, 0:DFT_N1].reshape(ns, n2, DFT_N1)
    bim = bb[:, DFT_N1:2 * DFT_N1].reshape(ns, n2, DFT_N1)
    d_scr[:, 0:n2, :] = (bre * twre + bim * twim).astype(BF16)
    d_scr[:, n2:two_n2, :] = (bim * twre - bre * twim).astype(BF16)
    for s in range(ns):
        emit(s, jnp.dot(li, d_scr[s], preferred_element_type=F32))


def _taps_kernel(bands_ref, w1t_ref, w1c_ref, w1s_ref, b1_ref, w2_ref, b2_ref, w3_ref, b3_ref,
                 fq_ref, w4_ref, dl_ref, o_ref, h_ref, tm_ref, *, seq_len, tl):
    nt = pl.program_id(0)

    @pl.when(pl.program_id(1) == 0)
    def _():
        col = nt * tl + lax.broadcasted_iota(jnp.int32, (1, tl), 1)
        pos = jnp.where(col < seq_len, col, 2 * seq_len - col)
        posf = pos.astype(F32)
        t = posf * np.float32(1.0 / (seq_len - 1))
        ang = bands_ref[...] * (posf * np.float32(2.0 * math.pi / seq_len))
        fq = fq_ref[...]
        pre = (w1t_ref[...] * t
               + jnp.dot(w1c_ref[...], jnp.cos(ang), preferred_element_type=F32, precision=HIGHEST)
               - jnp.dot(w1s_ref[...], jnp.sin(ang), preferred_element_type=F32, precision=HIGHEST))
        h = jnp.sin(fq * (pre + b1_ref[...]))
        h = jnp.sin(fq * (jnp.dot(w2_ref[...], h, preferred_element_type=F32, precision=HIGHEST) + b2_ref[...]))
        h = jnp.sin(fq * (jnp.dot(w3_ref[...], h, preferred_element_type=F32, precision=HIGHEST) + b3_ref[...]))
        h_ref[...] = h
        tm_ref[0:1, :] = t
        tm_ref[1:2, :] = jnp.where(col == seq_len, 0.0, 1.0)

    taps = jnp.dot(w4_ref[...], h_ref[...], preferred_element_type=F32, precision=HIGHEST)
    window = jnp.exp(-(dl_ref[...] * tm_ref[0:1, :]))
    o_ref[...] = taps * window * tm_ref[1:2, :]


def _hyena_taps(seq_len, f_w1, f_b1, f_w2, f_b2, f_w3, f_b3, f_w4, f_freq, d_model, *, tl=2048, td=256):
    fh = f_w1.shape[1]
    n_ord = f_w4.shape[1] // (2 * d_model)
    tl = min(tl, seq_len)
    ndt = d_model // td
    nlt = seq_len // tl
    col = lambda v: v.reshape(-1, 1).astype(F32)
    bands = jnp.linspace(1e-4, HY_BANDS - 1, HY_BANDS, dtype=F32).reshape(-1, 1)
    w1 = f_w1.astype(F32).T
    w4t = f_w4.astype(F32).reshape(fh, n_ord, 2, d_model).transpose(1, 2, 3, 0).reshape(n_ord * 2 * d_model, fh)
    max_decay = math.log(HY_TARGET) / HY_FAST_PCT
    min_decay = math.log(HY_TARGET) / HY_SLOW_PCT
    deltas = jnp.abs(jnp.linspace(min_decay, max_decay, d_model, dtype=F32))
    deltas = jnp.tile(deltas, n_ord).reshape(-1, 1)
    small = lambda a: pl.BlockSpec(a.shape, lambda n, j: (0, 0))
    args = [bands, w1[:, 0:1], w1[:, 1:1 + HY_BANDS], w1[:, 1 + HY_BANDS:1 + 2 * HY_BANDS], col(f_b1),
            f_w2.astype(F32).T, col(f_b2), f_w3.astype(F32).T, col(f_b3), col(f_freq)]
    return pl.pallas_call(
        functools.partial(_taps_kernel, seq_len=seq_len, tl=tl),
        grid=(2 * nlt, n_ord * ndt),
        in_specs=[small(a) for a in args] + [
            pl.BlockSpec((td, fh), lambda n, j: ((j // ndt * 2 + (n >= nlt).astype(jnp.int32)) * ndt + j % ndt, 0)),
            pl.BlockSpec((td, 1), lambda n, j: (j, 0))],
        out_specs=pl.BlockSpec((td, tl), lambda n, j: (j, n)),
        out_shape=jax.ShapeDtypeStruct((n_ord * d_model, 2 * seq_len), F32),
        scratch_shapes=[pltpu.VMEM((fh, tl), F32), pltpu.VMEM((8, tl), F32)],
        compiler_params=_cparams(("parallel", "arbitrary")),
        name="hyena_taps",
    )(*args, w4t, deltas)


def _spectrum_kernel(x_ref, lf_ref, twre_ref, twim_ref, f1f_ref, re_ref, im_ref, a_scr):
    ct, n2, _ = x_ref.shape
    cre, cim = _dft_fwd(lambda s: x_ref[s], lf_ref[...], twre_ref[...][None], twim_ref[...][None],
                        f1f_ref[...], a_scr)
    re_ref[...] = cre.reshape(ct, n2, DFT_N1)
    im_ref[...] = cim.reshape(ct, n2, DFT_N1)


def _spectrum(taps, *, ct=8):
    c, n = taps.shape
    n2 = n // DFT_N1
    cs = _consts_dev(n2)
    x = taps.reshape(c, n2, DFT_N1)
    blk = pl.BlockSpec((ct, n2, DFT_N1), lambda i: (i, 0, 0))
    full = lambda a: pl.BlockSpec(a.shape, lambda i: (0,) * a.ndim)
    consts = [cs["lf_full"], cs["twre"], cs["twim"], cs["f1_fwd"]]
    return pl.pallas_call(
        _spectrum_kernel,
        grid=(c // ct,),
        in_specs=[blk] + [full(a) for a in consts],
        out_specs=[blk, blk],
        out_shape=[jax.ShapeDtypeStruct((c, n2, DFT_N1), F32)] * 2,
        scratch_shapes=[pltpu.VMEM((ct, 2 * n2, DFT_N1), F32)],
        compiler_params=_cparams(("parallel",)),
        name="hyena_spectrum",
    )(x, *consts)


def _fftconv_kernel(x_ref, gate_ref, skip_ref,
                    kpre_ref, kpim_ref, lfp_ref, lip_ref, twpre_ref, twpim_ref,
                    ksre_ref, ksim_ref, lfs_ref, lis_ref, twsre_ref, twsim_ref,
                    f1f_ref, f1i_ref, o_ref, ap_scr, dp_scr, as_scr, ds_scr, *, hp, hs, nb):
    ct = x_ref.shape[0]
    f1f = f1f_ref[...]
    f1i = f1i_ref[...]

    def run(rows, n_sig, kre, kim, lf, li, twre, twim, a_scr, d_scr):
        def sig(s):
            c, r0, h = rows(s)
            return x_ref[c, r0:r0 + h, :]

        cre, cim = _dft_fwd(sig, lf, twre, twim, f1f, a_scr)
        yre = cre * kre - cim * kim
        yim = cre * kim + cim * kre

        def emit(s, y):
            c, r0, h = rows(s)
            xs = x_ref[c, r0:r0 + h, :]
            o_ref[c, r0:r0 + h, :] = gate_ref[c, r0:r0 + h, :] * (y + skip_ref[c] * xs)

        _dft_inv_real(yre, yim, li, twre, twim, f1i, d_scr, emit)

    n2p = 2 * hp
    run(lambda s: (s, 0, hp), ct,
        kpre_ref[...].reshape(ct * n2p, DFT_N1), kpim_ref[...].reshape(ct * n2p, DFT_N1),
        lfp_ref[...], lip_ref[...], twpre_ref[...][None], twpim_ref[...][None], ap_scr, dp_scr)
    n2s = 2 * hs
    rep = lambda k: jnp.broadcast_to(k[:, None], (ct, nb, n2s, DFT_N1)).reshape(ct * nb * n2s, DFT_N1)
    run(lambda s: (s // nb, hp + (s % nb) * hs, hs), ct * nb,
        rep(ksre_ref[...]), rep(ksim_ref[...]),
        lfs_ref[...], lis_ref[...], twsre_ref[...][None], twsim_ref[...][None], as_scr, ds_scr)


def _fftconv(x, x_off, gate, gate_off, skip, kp, kp_off, ks, ks_off, *, d_model, hp, hs, nb, ct=8):
    r = hp + nb * hs
    cp, cs_ = _consts_dev(2 * hp), _consts_dev(2 * hs)
    cb = lambda off: (lambda i: (off // ct + i, 0, 0))
    full = lambda a: pl.BlockSpec(a.shape, lambda i: (0,) * a.ndim)
    kspec = lambda n2, off: pl.BlockSpec((ct, n2, DFT_N1), cb(off))
    skip3 = jnp.broadcast_to(skip.astype(F32).reshape(d_model, 1, 1), (d_model, 1, DFT_N1))
    pc = [cp["lf_half"], cp["li_half"], cp["twre"], cp["twim"]]
    sc = [cs_["lf_half"], cs_["li_half"], cs_["twre"], cs_["twim"]]
    return pl.pallas_call(
        functools.partial(_fftconv_kernel, hp=hp, hs=hs, nb=nb),
        grid=(d_model // ct,),
        in_specs=[pl.BlockSpec((ct, r, DFT_N1), cb(x_off)), pl.BlockSpec((ct, r, DFT_N1), cb(gate_off)),
                  pl.BlockSpec((ct, 1, DFT_N1), cb(0)),
                  kspec(2 * hp, kp_off), kspec(2 * hp, kp_off)] + [full(a) for a in pc]
                 + [kspec(2 * hs, ks_off), kspec(2 * hs, ks_off)] + [full(a) for a in sc]
                 + [full(cp["f1_fwd"]), full(cp["f1_inv"])],
        out_specs=pl.BlockSpec((ct, r, DFT_N1), cb(0)),
        out_shape=jax.ShapeDtypeStruct((d_model, r, DFT_N1), F32),
        scratch_shapes=[pltpu.VMEM((ct, 4 * hp, DFT_N1), F32), pltpu.VMEM((ct, 4 * hp, DFT_N1), BF16),
                        pltpu.VMEM((ct * nb, 4 * hs, DFT_N1), F32), pltpu.VMEM((ct * nb, 4 * hs, DFT_N1), BF16)],
        compiler_params=_cparams(("parallel",)),
        name="hyena_fftconv",
    )(x, gate, skip3, kp[0], kp[1], *pc, ks[0], ks[1], *sc, cp["f1_fwd"], cp["f1_inv"])


def _outproj_t_kernel(zt_ref, w_ref, res_ref, o_ref):
    z = zt_ref[...].astype(BF16)
    y = lax.dot_general(z, w_ref[...], (((0,), (0,)), ((), ())), preferred_element_type=F32)
    o_ref[...] = res_ref[...] + y


def _outproj_t(zt, w, res, *, tm=512):
    k, m = zt.shape
    n = w.shape[1]
    tm = min(tm, m)
    return pl.pallas_call(
        _outproj_t_kernel,
        grid=(m // tm,),
        in_specs=[pl.BlockSpec((k, tm), lambda i: (0, i)),
                  pl.BlockSpec((k, n), lambda i: (0, 0)),
                  pl.BlockSpec((tm, n), lambda i: (i, 0))],
        out_specs=pl.BlockSpec((tm, n), lambda i: (i, 0)),
        out_shape=jax.ShapeDtypeStruct((m, n), F32),
        compiler_params=_cparams(("parallel",)),
        name="hyena_outproj",
    )(zt, w.astype(BF16), res)


def _softplus(x):
    return jnp.maximum(x, 0.0) + jnp.log1p(jnp.exp(-jnp.abs(x)))


def _gdn_kernel(qf_ref, kf_ref, vf_ref, baf_ref, batf_ref, qb_ref, kb_ref, vb_ref, bab_ref, batb_ref,
                alog_r_ref, dtb_r_ref, alog_c_ref, dtb_c_ref, of_ref, ob_ref, sf_ref, sb_ref, *,
                n_tiles, f_resets, b_resets):
    hp = pl.program_id(0)
    step = pl.program_id(1)
    t = GDN_TILE
    c = GDN_CHUNK
    nc = t // c
    dh = GDN_DH

    @pl.when(_in_set(step, f_resets))
    def _():
        sf_ref[...] = jnp.zeros_like(sf_ref)

    @pl.when(_in_set(n_tiles - 1 - step, b_resets))
    def _():
        sb_ref[...] = jnp.zeros_like(sb_ref)

    ri = lax.broadcasted_iota(jnp.int32, (t, t), 0)
    ci = lax.broadcasted_iota(jnp.int32, (t, t), 1)
    same = (ri // c) == (ci // c)
    rsub = lax.broadcasted_iota(jnp.int32, (t, LANES), 0) % c
    lane = lax.broadcasted_iota(jnp.int32, (t, LANES), 1)
    lrow = lax.broadcasted_iota(jnp.int32, (8, t), 1) % c

    def l2n(x):
        return x * lax.rsqrt(jnp.sum(x * x, axis=-1, keepdims=True) + RMS_EPS)

    def direction(back, q_ref, k_ref, v_ref, ba_ref, bat_ref, o_ref, s_ref):
        d = 1 if back else 0
        if back:
            incl = same & (ci >= ri)
            strict = same & (ci > ri)
        else:
            incl = same & (ci <= ri)
            strict = same & (ci < ri)
        q = l2n(q_ref[...]) * np.float32(GDN_DH ** -0.5)
        k = l2n(k_ref[...])
        q16 = q.astype(BF16)
        k16 = k.astype(BF16)
        qk_kk = lax.dot_general(jnp.concatenate([q16, k16], axis=0), k16, (((1,), (1,)), ((), ())),
                                preferred_element_type=F32)
        qk_raw = qk_kk[0:t]
        kk_raw = qk_kk[t:2 * t]

        ba = ba_ref[...]
        beta_all = jax.nn.sigmoid(ba)
        g_all = -jnp.exp(alog_r_ref[...]) * _softplus(ba + dtb_r_ref[...])
        sh = 1
        while sh < c:
            if back:
                g_all = g_all + jnp.where(rsub < c - sh, pltpu.roll(g_all, t - sh, axis=0), 0.0)
            else:
                g_all = g_all + jnp.where(rsub >= sh, pltpu.roll(g_all, sh, axis=0), 0.0)
            sh *= 2
        for hh in range(2):
            head = 2 * hp + hh
            jb = d * GDN_NV + head
            ja = 2 * GDN_NV + d * GDN_NV + head
            beta_c = jnp.sum(jnp.where(lane == jb, beta_all, 0.0), axis=1, keepdims=True)
            gc_c = jnp.sum(jnp.where(lane == ja, g_all, 0.0), axis=1, keepdims=True)
            a_row = bat_ref[pl.ds(ja, 1), :]
            g_row = -jnp.exp(alog_c_ref[pl.ds(ja, 1), :]) * _softplus(a_row + dtb_c_ref[pl.ds(ja, 1), :])
            g_row = jnp.broadcast_to(g_row, (8, t))
            sh = 1
            while sh < c:
                if back:
                    g_row = g_row + jnp.where(lrow < c - sh, pltpu.roll(g_row, t - sh, axis=1), 0.0)
                else:
                    g_row = g_row + jnp.where(lrow >= sh, pltpu.roll(g_row, sh, axis=1), 0.0)
                sh *= 2
            gc_r = g_row[0:1, :]
            decay = jnp.where(incl, jnp.exp(jnp.where(incl, gc_c - gc_r, 0.0)), 0.0)
            p = jnp.where(strict, -(beta_c * kk_raw * decay), 0.0)
            pm = p.astype(BF16)
            nn = p
            m = 1
            while 2 * m < c:
                both = jnp.dot(jnp.concatenate([pm, nn.astype(BF16)], axis=0), pm, preferred_element_type=F32) \
                    if m > 1 else None
                if m == 1:
                    p2 = jnp.dot(pm, pm, preferred_element_type=F32)
                    nn_p = None
                else:
                    p2 = both[0:t]
                    nn_p = both[t:2 * t]
                if m > 1:
                    nn = nn + pm_f32 + nn_p
                pm_f32 = p2
                pm = p2.astype(BF16)
                m *= 2
            nn = nn + pm_f32 + jnp.dot(nn.astype(BF16), pm, preferred_element_type=F32)

            v = v_ref[:, hh * dh:(hh + 1) * dh]
            rhs = jnp.concatenate([v * beta_c, k * (beta_c * jnp.exp(gc_c))], axis=1)
            uw = rhs + jnp.dot(nn.astype(BF16), rhs.astype(BF16), preferred_element_type=F32)
            uw16 = uw.astype(BF16)
            qkm = jnp.where(incl, qk_raw * decay, 0.0).astype(BF16)
            qq = jnp.dot(qkm, uw16, preferred_element_type=F32)
            gc3 = gc_c.reshape(nc, c, 1)
            gl3 = gc3[:, 0:1, :] if back else gc3[:, c - 1:c, :]
            kg16 = (k * jnp.exp(jnp.broadcast_to(gl3, (nc, c, 1)).reshape(t, 1) - gc_c)).astype(BF16)
            qg = q * jnp.exp(gc_c)
            qp16 = (qg - qq[:, dh:2 * dh]).astype(BF16)
            s = s_ref[hh]
            order = range(nc - 1, -1, -1) if back else range(nc)
            for n in order:
                sl = slice(n * c, (n + 1) * c)
                rp = lax.dot_general(kg16[sl], uw16[sl], (((0,), (0,)), ((), ())),
                                     preferred_element_type=F32)
                z = jnp.dot(jnp.concatenate([rp[:, dh:2 * dh].astype(BF16), qp16[sl]], axis=0),
                            s.astype(BF16), preferred_element_type=F32)
                o_ref[sl, hh * dh:(hh + 1) * dh] = z[dh:dh + c] + qq[sl, 0:dh]
                s = jnp.exp(gl3[n]) * s - z[0:dh] + rp[:, 0:dh]
            s_ref[hh] = s

    direction(False, qf_ref, kf_ref, vf_ref, baf_ref, batf_ref, of_ref, sf_ref)
    direction(True, qb_ref, kb_ref, vb_ref, bab_ref, batb_ref, ob_ref, sb_ref)


def _gdn_scan(qkv, ba, a_log, dt_bias, *, starts, ends):
    m = qkv.shape[0]
    t = GDN_TILE
    n_tiles = m // t
    key_blocks = GDN_NK
    bat = ba.T
    pad = lambda a: jnp.pad(a.astype(F32).reshape(-1), (2 * GDN_NV, LANES - 4 * GDN_NV))
    alog_r = pad(a_log).reshape(1, LANES)
    dtb_r = pad(dt_bias).reshape(1, LANES)
    alog_c = pad(a_log).reshape(LANES, 1)
    dtb_c = pad(dt_bias).reshape(LANES, 1)
    f_resets = tuple(s // t for s in starts)
    b_resets = tuple(e // t - 1 for e in ends)
    fwd = lambda h, s: s
    bwd = lambda h, s: n_tiles - 1 - s

    def specs(tile):
        return [pl.BlockSpec((t, GDN_DH), lambda h, s: (tile(h, s), h)),
                pl.BlockSpec((t, GDN_DH), lambda h, s: (tile(h, s), key_blocks + h)),
                pl.BlockSpec((t, 2 * GDN_DH), lambda h, s: (tile(h, s), key_blocks + h)),
                pl.BlockSpec((t, LANES), lambda h, s: (tile(h, s), 0)),
                pl.BlockSpec((LANES, t), lambda h, s: (0, tile(h, s)))]

    small = lambda a: pl.BlockSpec(a.shape, lambda h, s: (0, 0))
    out = jax.ShapeDtypeStruct((m, GDN_NV * GDN_DH), F32)
    return pl.pallas_call(
        functools.partial(_gdn_kernel, n_tiles=n_tiles, f_resets=f_resets, b_resets=b_resets),
        grid=(GDN_NK, n_tiles),
        in_specs=specs(fwd) + specs(bwd) + [small(alog_r), small(dtb_r), small(alog_c), small(dtb_c)],
        out_specs=[pl.BlockSpec((t, 2 * GDN_DH), lambda h, s: (s, h)),
                   pl.BlockSpec((t, 2 * GDN_DH), lambda h, s: (n_tiles - 1 - s, h))],
        out_shape=[out, out],
        scratch_shapes=[pltpu.VMEM((2, GDN_DH, GDN_DH), F32), pltpu.VMEM((2, GDN_DH, GDN_DH), F32)],
        compiler_params=_cparams(("parallel", "arbitrary")),
        name="gdn_scan",
    )(qkv, qkv, qkv, ba, bat, qkv, qkv, qkv, ba, bat, alog_r, dtb_r, alog_c, dtb_c)


def _gdn_out_kernel(of_ref, ob_ref, z_ref, nw_ref, w_ref, res_ref, o_ref, y_ref):
    nw = nw_ref[...]
    for h in range(GDN_NV):
        sl = slice(h * GDN_DH, (h + 1) * GDN_DH)
        o = of_ref[:, sl] + ob_ref[:, sl]
        o = o * lax.rsqrt(jnp.mean(o * o, axis=-1, keepdims=True) + RMS_EPS) * nw
        z = z_ref[:, sl]
        y_ref[:, sl] = (o * (z * jax.nn.sigmoid(z))).astype(BF16)
    o_ref[...] = res_ref[...] + jnp.dot(y_ref[...], w_ref[...], preferred_element_type=F32)


def _gdn_out(o_f, o_b, z, norm_w, w, res, *, tm=512):
    m, kv = o_f.shape
    n = w.shape[1]
    tm = min(tm, m)
    row = lambda width: pl.BlockSpec((tm, width), lambda i: (i, 0))
    return pl.pallas_call(
        _gdn_out_kernel,
        grid=(m // tm,),
        in_specs=[row(kv), row(kv), row(kv),
                  pl.BlockSpec((1, GDN_DH), lambda i: (0, 0)),
                  pl.BlockSpec((kv, n), lambda i: (0, 0)),
                  row(n)],
        out_specs=row(n),
        out_shape=jax.ShapeDtypeStruct((m, n), F32),
        scratch_shapes=[pltpu.VMEM((tm, kv), BF16)],
        compiler_params=_cparams(("parallel",)),
        name="gdn_out",
    )(o_f, o_b, z, norm_w.astype(F32).reshape(1, GDN_DH), w.astype(BF16), res)


def _mixer_kernel(x_ref, g_ref, r_ref, rb_ref, wg_ref, wu_ref, wd_ref, gf_ref, o_ref,
                  h_ref, acc_ref, gates_ref, *, n_exp, routed, final_norm):
    e = pl.program_id(1)
    f = pl.program_id(2)
    tm = x_ref.shape[0]

    @pl.when((e == 0) & (f == 0))
    def _():
        h = _norm_rows(x_ref[...], g_ref[...])
        h_ref[...] = h.astype(BF16)
        acc_ref[...] = jnp.zeros_like(acc_ref)
        if routed:
            lane = lax.broadcasted_iota(jnp.int32, (tm, LANES), 1)
            logits = jnp.dot(h, r_ref[...], preferred_element_type=F32, precision=HIGHEST) + rb_ref[...]
            logits = jnp.where(lane < n_exp, logits, -jnp.inf)
            m1 = jnp.max(logits, axis=1, keepdims=True)
            i1 = jnp.min(jnp.where(logits == m1, lane, LANES), axis=1, keepdims=True)
            rest = jnp.where(lane == i1, -jnp.inf, logits)
            m2 = jnp.max(rest, axis=1, keepdims=True)
            i2 = jnp.min(jnp.where(rest == m2, lane, LANES), axis=1, keepdims=True)
            e2 = jnp.exp(m2 - m1)
            w1 = 1.0 / (1.0 + e2)
            gates_ref[...] = jnp.where(lane == i1, w1, 0.0) + jnp.where(lane == i2, e2 * w1, 0.0)

    h = h_ref[...]
    a = jnp.dot(h, wg_ref[0], preferred_element_type=F32)
    u = jnp.dot(h, wu_ref[0], preferred_element_type=F32)
    act = (a * jax.nn.sigmoid(a) * u).astype(BF16)
    y = jnp.dot(act, wd_ref[0], preferred_element_type=F32)
    if routed:
        lane = lax.broadcasted_iota(jnp.int32, (tm, LANES), 1)
        gate = jnp.sum(jnp.where(lane == e, gates_ref[...], 0.0), axis=1, keepdims=True)
        y = gate * y
    acc_ref[...] += y

    @pl.when((e == n_exp - 1) & (f == pl.num_programs(2) - 1))
    def _():
        out = x_ref[...] + acc_ref[...]
        if final_norm:
            out = _norm_rows(out, gf_ref[...])
        o_ref[...] = out


def _mixer(x, gain, w_gate, w_up, w_down, *, router=None, router_bias=None, final_gain=None,
           tm=1024, tf=512):
    m, d = x.shape
    n_exp, _, ff = w_gate.shape
    tm = min(tm, m)
    tf = min(tf, ff)
    routed = router is not None
    if routed:
        r = jnp.pad(router.astype(F32), ((0, 0), (0, LANES - n_exp)))
        rb = jnp.pad(router_bias.astype(F32).reshape(1, n_exp), ((0, 0), (0, LANES - n_exp)))
    else:
        r = jnp.zeros((d, LANES), F32)
        rb = jnp.zeros((1, LANES), F32)
    final_norm = final_gain is not None
    gf = (final_gain if final_norm else gain).astype(F32).reshape(1, d)
    return pl.pallas_call(
        functools.partial(_mixer_kernel, n_exp=n_exp, routed=routed, final_norm=final_norm),
        grid=(m // tm, n_exp, ff // tf),
        in_specs=[pl.BlockSpec((tm, d), lambda i, e, f: (i, 0)),
                  pl.BlockSpec((1, d), lambda i, e, f: (0, 0)),
                  pl.BlockSpec((d, LANES), lambda i, e, f: (0, 0)),
                  pl.BlockSpec((1, LANES), lambda i, e, f: (0, 0)),
                  pl.BlockSpec((1, d, tf), lambda i, e, f: (e, 0, f)),
                  pl.BlockSpec((1, d, tf), lambda i, e, f: (e, 0, f)),
                  pl.BlockSpec((1, tf, d), lambda i, e, f: (e, f, 0)),
                  pl.BlockSpec((1, d), lambda i, e, f: (0, 0))],
        out_specs=pl.BlockSpec((tm, d), lambda i, e, f: (i, 0)),
        out_shape=jax.ShapeDtypeStruct((m, d), F32),
        scratch_shapes=[pltpu.VMEM((tm, d), BF16), pltpu.VMEM((tm, d), F32), pltpu.VMEM((tm, LANES), F32)],
        compiler_params=_cparams(("parallel", "arbitrary", "arbitrary")),
        name="moe_mixer" if routed else "ffn_mixer",
    )(x, gain.astype(F32).reshape(1, d), r, rb, w_gate.astype(BF16), w_up.astype(BF16),
      w_down.astype(BF16), gf)


def _hyena_layer(x, gain, lp, ls, nb_p, nb_s, w_in, conv_w, f_w1, f_b1, f_w2, f_b2, f_w3, f_b3, f_w4,
                 f_freq, skip, w_out, *, starts, ends):
    m, d = x.shape
    assert nb_p == 1 and lp % (2 * DFT_N1) == 0 and ls % (2 * DFT_N1) == 0
    hp, hs = lp // DFT_N1, ls // DFT_N1
    ut = _proj(x, gain, w_in, conv_w, transpose_out=True, starts=starts, ends=ends)
    u3 = ut.reshape(3 * d, m // DFT_N1, DFT_N1)
    filt = (f_w1, f_b1, f_w2, f_b2, f_w3, f_b3, f_w4, f_freq)
    kp = _spectrum(_hyena_taps(lp, *filt, d))
    ks = _spectrum(_hyena_taps(ls, *filt, d))
    conv = functools.partial(_fftconv, d_model=d, hp=hp, hs=hs, nb=nb_s)
    z1 = conv(u3, 0, u3, d, skip[0], kp, 0, ks, 0)
    z2 = conv(z1, 0, u3, 2 * d, skip[1], kp, d, ks, d)
    return _outproj_t(z2.reshape(d, m), w_out, x)


def _gdn_layer(x, gain, w_in, conv_w, a_log, dt_bias, norm_w, w_out, *, starts, ends):
    key, val = GDN_NK * GDN_DH, GDN_NV * GDN_DH
    cd = 2 * key + val
    qkv = _proj(x, gain, w_in[:, :cd], conv_w, act="silu", starts=starts, ends=ends)
    z = _proj(x, gain, w_in[:, cd:cd + val])
    w_ba = jnp.pad(w_in[:, cd + val:], ((0, 0), (0, LANES - 4 * GDN_NV)))
    ba = _proj(x, gain, w_ba)
    o_f, o_b = _gdn_scan(qkv, ba, a_log, dt_bias, starts=starts, ends=ends)
    return _gdn_out(o_f, o_b, z, norm_w, w_out, x)


def kernel(x_prompt, x_sample, norm_mix, norm_ffn, norm_final, hy_w_in, hy_conv, hy_f_w1, hy_f_b1, hy_f_w2, hy_f_b2, hy_f_w3, hy_f_b3, hy_f_w4, hy_f_freq, hy_skip, hy_w_out, gdn_w_in, gdn_conv, gdn_a_log, gdn_dt_bias, gdn_norm, gdn_w_out, ffn_w_gate, ffn_w_up, ffn_w_down, moe_router, moe_router_bias, moe_w_gate, moe_w_up, moe_w_down):
    bp, lp, d = x_prompt.shape
    bs, ls, _ = x_sample.shape
    x = jnp.concatenate([x_prompt.reshape(bp * lp, d), x_sample.reshape(bs * ls, d)], axis=0)
    starts = tuple(b * lp for b in range(bp)) + tuple(bp * lp + b * ls for b in range(bs))
    ends = tuple(s + lp for s in starts[:bp]) + tuple(s + ls for s in starts[bp:])
    depth = norm_mix.shape[0]
    for i in range(depth):
        j = i // 2
        last = i == depth - 1
        if i % 2 == 0:
            x = _hyena_layer(x, norm_mix[i], lp, ls, bp, bs, hy_w_in[j], hy_conv[j], hy_f_w1[j], hy_f_b1[j],
                             hy_f_w2[j], hy_f_b2[j], hy_f_w3[j], hy_f_b3[j], hy_f_w4[j], hy_f_freq[j],
                             hy_skip[j], hy_w_out[j], starts=starts, ends=ends)
            x = _mixer(x, norm_ffn[i], ffn_w_gate[j][None], ffn_w_up[j][None], ffn_w_down[j][None],
                       final_gain=norm_final if last else None)
        else:
            x = _gdn_layer(x, norm_mix[i], gdn_w_in[j], gdn_conv[j], gdn_a_log[j], gdn_dt_bias[j],
                           gdn_norm[j], gdn_w_out[j], starts=starts, ends=ends)
            x = _mixer(x, norm_ffn[i], moe_w_gate[j], moe_w_up[j], moe_w_down[j], router=moe_router[j],
                       router_bias=moe_router_bias[j], final_gain=norm_final if last else None)
    y_prompt = x[:bp * lp].reshape(bp, lp, d)
    y_sample = x[bp * lp:].reshape(bs, ls, d)
    return (y_prompt, y_sample)
```

```python
import functools
import math

import numpy as np
import jax
import jax.numpy as jnp
from jax import lax
from jax.experimental import pallas as pl
from jax.experimental.pallas import tpu as pltpu

F32 = jnp.float32
BF16 = jnp.bfloat16

RMS_EPS = 1e-6
HY_BANDS = 16
HY_FAST_PCT = 0.3
HY_SLOW_PCT = 1.5
HY_TARGET = 1e-2
GDN_NK = 8
GDN_NV = 16
GDN_DH = 128
TOP_K = 2

LANES = 128
HALO = 16
DFT_N1 = 256
GDN_CHUNK = 64
GDN_TILE = 256
VMEM_LIMIT = 56 * 1024 * 1024

HIGHEST = lax.Precision.HIGHEST


def _cparams(sem):
    return pltpu.CompilerParams(dimension_semantics=sem, vmem_limit_bytes=VMEM_LIMIT)


def _in_set(v, values):
    r = v == values[0]
    for b in values[1:]:
        r = jnp.logical_or(r, v == b)
    return r


def _norm_rows(x, g):
    ms = jnp.mean(x * x, axis=-1, keepdims=True)
    return x * lax.rsqrt(ms + RMS_EPS) * g


def _proj_conv_kernel(xp_ref, x_ref, xn_ref, g_ref, w_ref, cw_ref, o_ref, h_ref, *,
                      tm, starts, ends, act, transpose_out):
    i = pl.program_id(0)

    @pl.when(pl.program_id(1) == 0)
    def _():
        g = g_ref[...]
        row0 = i * tm
        hp = jnp.where(_in_set(row0, starts), 0.0, _norm_rows(xp_ref[...], g))
        hn = jnp.where(_in_set(row0 + tm, ends), 0.0, _norm_rows(xn_ref[...], g))
        h_ref[0:HALO, :] = hp.astype(BF16)
        h_ref[HALO:HALO + tm, :] = _norm_rows(x_ref[...], g).astype(BF16)
        h_ref[HALO + tm:HALO + tm + HALO, :] = hn.astype(BF16)

    p = jnp.dot(h_ref[...], w_ref[...], preferred_element_type=F32)
    n = tm + 2 * HALO
    up = pltpu.roll(p, 1, axis=0)
    dn = pltpu.roll(p, n - 1, axis=0)
    cw = cw_ref[...]
    y = (cw[0:1, :] * up[HALO:HALO + tm, :] + cw[1:2, :] * p[HALO:HALO + tm, :]
         + cw[2:3, :] * dn[HALO:HALO + tm, :])
    if act == "silu":
        y = y * jax.nn.sigmoid(y)
    o_ref[...] = y.T if transpose_out else y


def _proj_kernel(x_ref, g_ref, w_ref, o_ref, h_ref):
    @pl.when(pl.program_id(1) == 0)
    def _():
        h_ref[...] = _norm_rows(x_ref[...], g_ref[...]).astype(BF16)

    o_ref[...] = jnp.dot(h_ref[...], w_ref[...], preferred_element_type=F32)


def _proj(x, gain, w, conv_w=None, *, act=None, transpose_out=False, starts=(), ends=(),
          tm=1024, tn=512):
    m, d = x.shape
    n = w.shape[1]
    tm = min(tm, m)
    tn = min(tn, n)
    grid = (m // tm, n // tn)
    gain = gain.reshape(1, d).astype(F32)
    w = w.astype(BF16)
    if conv_w is None:
        return pl.pallas_call(
            _proj_kernel,
            grid=grid,
            in_specs=[pl.BlockSpec((tm, d), lambda i, j: (i, 0)),
                      pl.BlockSpec((1, d), lambda i, j: (0, 0)),
                      pl.BlockSpec((d, tn), lambda i, j: (0, j))],
            out_specs=pl.BlockSpec((tm, tn), lambda i, j: (i, j)),
            out_shape=jax.ShapeDtypeStruct((m, n), F32),
            scratch_shapes=[pltpu.VMEM((tm, d), BF16)],
            compiler_params=_cparams(("parallel", "arbitrary")),
            name="proj",
        )(x, gain, w)
    hb = tm // HALO
    last = m // HALO - 1
    kern = functools.partial(_proj_conv_kernel, tm=tm, starts=tuple(starts), ends=tuple(ends),
                             act=act, transpose_out=transpose_out)
    if transpose_out:
        out_spec = pl.BlockSpec((tn, tm), lambda i, j: (j, i))
        out_shape = jax.ShapeDtypeStruct((n, m), F32)
    else:
        out_spec = pl.BlockSpec((tm, tn), lambda i, j: (i, j))
        out_shape = jax.ShapeDtypeStruct((m, n), F32)
    return pl.pallas_call(
        kern,
        grid=grid,
        in_specs=[pl.BlockSpec((HALO, d), lambda i, j: (jnp.maximum(i * hb - 1, 0), 0)),
                  pl.BlockSpec((tm, d), lambda i, j: (i, 0)),
                  pl.BlockSpec((HALO, d), lambda i, j: (jnp.minimum((i + 1) * hb, last), 0)),
                  pl.BlockSpec((1, d), lambda i, j: (0, 0)),
                  pl.BlockSpec((d, tn), lambda i, j: (0, j)),
                  pl.BlockSpec((3, tn), lambda i, j: (0, j))],
        out_specs=out_spec,
        out_shape=out_shape,
        scratch_shapes=[pltpu.VMEM((tm + 2 * HALO, d), BF16)],
        compiler_params=_cparams(("parallel", "arbitrary")),
        name="proj_conv",
    )(x, x, x, gain, w, conv_w.astype(F32))


@functools.lru_cache(maxsize=None)
def _dft_consts(n2):
    n1 = DFT_N1
    n = n1 * n2
    h = n2 // 2
    k1 = np.arange(n1)
    f1 = np.exp(-2j * np.pi * np.outer(k1, k1) / n1)
    f1_fwd = np.block([[f1.real, f1.imag], [-f1.imag, f1.real]])
    f1_inv = np.block([[f1.real, -f1.imag], [f1.imag, f1.real]])
    k2 = np.arange(n2)
    f2 = np.exp(-2j * np.pi * np.outer(k2, k2) / n2)
    tw = np.exp(-2j * np.pi * np.outer(k2, k1) / n)
    lf_half = np.concatenate([f2.real[:, :h], f2.imag[:, :h]], axis=0)
    lf_full = np.concatenate([f2.real, f2.imag], axis=0)
    li_half = np.concatenate([f2.real[:h, :], f2.imag[:h, :]], axis=1) / n
    return dict(f1_fwd=f1_fwd, f1_inv=f1_inv, twre=tw.real, twim=tw.imag,
                lf_half=lf_half, lf_full=lf_full, li_half=li_half)


def _consts_dev(n2):
    c = _dft_consts(n2)
    return dict(
        f1_fwd=jnp.asarray(c["f1_fwd"], BF16), f1_inv=jnp.asarray(c["f1_inv"], BF16),
        twre=jnp.asarray(c["twre"], F32), twim=jnp.asarray(c["twim"], F32),
        lf_half=jnp.asarray(c["lf_half"], BF16), lf_full=jnp.asarray(c["lf_full"], BF16),
        li_half=jnp.asarray(c["li_half"], BF16))


def _dft_fwd(sig, lf, twre, twim, f1f, a_scr):
    ns, two_n2, _ = a_scr.shape
    n2 = two_n2 // 2
    for s in range(ns):
        a_scr[s] = jnp.dot(lf, sig(s).astype(BF16), preferred_element_type=F32)
    are = a_scr[:, 0:n2, :]
    aim = a_scr[:, n2:two_n2, :]
    bre = (are * twre - aim * twim).reshape(ns * n2, DFT_N1)
    bim = (are * twim + aim * twre).reshape(ns * n2, DFT_N1)
    bcat = jnp.concatenate([bre, bim], axis=1).astype(BF16)
    cc = jnp.dot(bcat, f1f, preferred_element_type=F32)
    return cc[:, 0:DFT_N1], cc[:, DFT_N1:2 * DFT_N1]


def _dft_inv_real(yre, yim, li, twre, twim, f1i, d_scr, emit):
    ns, two_n2, _ = d_scr.shape
    n2 = two_n2 // 2
    ycat = jnp.concatenate([yre, yim], axis=1).astype(BF16)
    bb = jnp.dot(ycat, f1i, preferred_element_type=F32)
    bre = bb[:, 0:DFT_N1].reshape(ns, n2, DFT_N1)
    bim = bb[:, DFT_N1:2 * DFT_N1].reshape(ns, n2, DFT_N1)
    d_scr[:, 0:n2, :] = (bre * twre + bim * twim).astype(BF16)
    d_scr[:, n2:two_n2, :] = (bim * twre - bre * twim).astype(BF16)
    for s in range(ns):
        emit(s, jnp.dot(li, d_scr[s], preferred_element_type=F32))


def _taps_kernel(bands_ref, w1t_ref, w1c_ref, w1s_ref, b1_ref, w2_ref, b2_ref, w3_ref, b3_ref,
                 fq_ref, w4_ref, dl_ref, o_ref, h_ref, tm_ref, *, seq_len, tl):
    nt = pl.program_id(0)

    @pl.when(pl.program_id(1) == 0)
    def _():
        col = nt * tl + lax.broadcasted_iota(jnp.int32, (1, tl), 1)
        pos = jnp.where(col < seq_len, col, 2 * seq_len - col)
        posf = pos.astype(F32)
        t = posf * np.float32(1.0 / (seq_len - 1))
        ang = bands_ref[...] * (posf * np.float32(2.0 * math.pi / seq_len))
        fq = fq_ref[...]
        pre = (w1t_ref[...] * t
               + jnp.dot(w1c_ref[...], jnp.cos(ang), preferred_element_type=F32, precision=HIGHEST)
               - jnp.dot(w1s_ref[...], jnp.sin(ang), preferred_element_type=F32, precision=HIGHEST))
        h = jnp.sin(fq * (pre + b1_ref[...]))
        h = jnp.sin(fq * (jnp.dot(w2_ref[...], h, preferred_element_type=F32, precision=HIGHEST) + b2_ref[...]))
        h = jnp.sin(fq * (jnp.dot(w3_ref[...], h, preferred_element_type=F32, precision=HIGHEST) + b3_ref[...]))
        h_ref[...] = h
        tm_ref[0:1, :] = t
        tm_ref[1:2, :] = jnp.where(col == seq_len, 0.0, 1.0)

    taps = jnp.dot(w4_ref[...], h_ref[...], preferred_element_type=F32, precision=HIGHEST)
    window = jnp.exp(-(dl_ref[...] * tm_ref[0:1, :]))
    o_ref[...] = taps * window * tm_ref[1:2, :]


def _hyena_taps(seq_len, f_w1, f_b1, f_w2, f_b2, f_w3, f_b3, f_w4, f_freq, d_model, *, tl=2048, td=256):
    fh = f_w1.shape[1]
    n_ord = f_w4.shape[1] // (2 * d_model)
    tl = min(tl, seq_len)
    ndt = d_model // td
    nlt = seq_len // tl
    col = lambda v: v.reshape(-1, 1).astype(F32)
    bands = jnp.linspace(1e-4, HY_BANDS - 1, HY_BANDS, dtype=F32).reshape(-1, 1)
    w1 = f_w1.astype(F32).T
    w4t = f_w4.astype(F32).reshape(fh, n_ord, 2, d_model).transpose(1, 2, 3, 0).reshape(n_ord * 2 * d_model, fh)
    max_decay = math.log(HY_TARGET) / HY_FAST_PCT
    min_decay = math.log(HY_TARGET) / HY_SLOW_PCT
    deltas = jnp.abs(jnp.linspace(min_decay, max_decay, d_model, dtype=F32))
    deltas = jnp.tile(deltas, n_ord).reshape(-1, 1)
    small = lambda a: pl.BlockSpec(a.shape, lambda n, j: (0, 0))
    args = [bands, w1[:, 0:1], w1[:, 1:1 + HY_BANDS], w1[:, 1 + HY_BANDS:1 + 2 * HY_BANDS], col(f_b1),
            f_w2.astype(F32).T, col(f_b2), f_w3.astype(F32).T, col(f_b3), col(f_freq)]
    return pl.pallas_call(
        functools.partial(_taps_kernel, seq_len=seq_len, tl=tl),
        grid=(2 * nlt, n_ord * ndt),
        in_specs=[small(a) for a in args] + [
            pl.BlockSpec((td, fh), lambda n, j: ((j // ndt * 2 + (n >= nlt).astype(jnp.int32)) * ndt + j % ndt, 0)),
            pl.BlockSpec((td, 1), lambda n, j: (j, 0))],
        out_specs=pl.BlockSpec((td, tl), lambda n, j: (j, n)),
        out_shape=jax.ShapeDtypeStruct((n_ord * d_model, 2 * seq_len), F32),
        scratch_shapes=[pltpu.VMEM((fh, tl), F32), pltpu.VMEM((8, tl), F32)],
        compiler_params=_cparams(("parallel", "arbitrary")),
        name="hyena_taps",
    )(*args, w4t, deltas)


def _spectrum_kernel(x_ref, lf_ref, twre_ref, twim_ref, f1f_ref, re_ref, im_ref, a_scr):
    ct, n2, _ = x_ref.shape
    cre, cim = _dft_fwd(lambda s: x_ref[s], lf_ref[...], twre_ref[...][None], twim_ref[...][None],
                        f1f_ref[...], a_scr)
    re_ref[...] = cre.reshape(ct, n2, DFT_N1)
    im_ref[...] = cim.reshape(ct, n2, DFT_N1)


def _spectrum(taps, *, ct=8):
    c, n = taps.shape
    n2 = n // DFT_N1
    cs = _consts_dev(n2)
    x = taps.reshape(c, n2, DFT_N1)
    blk = pl.BlockSpec((ct, n2, DFT_N1), lambda i: (i, 0, 0))
    full = lambda a: pl.BlockSpec(a.shape, lambda i: (0,) * a.ndim)
    consts = [cs["lf_full"], cs["twre"], cs["twim"], cs["f1_fwd"]]
    return pl.pallas_call(
        _spectrum_kernel,
        grid=(c // ct,),
        in_specs=[blk] + [full(a) for a in consts],
        out_specs=[blk, blk],
        out_shape=[jax.ShapeDtypeStruct((c, n2, DFT_N1), F32)] * 2,
        scratch_shapes=[pltpu.VMEM((ct, 2 * n2, DFT_N1), F32)],
        compiler_params=_cparams(("parallel",)),
        name="hyena_spectrum",
    )(x, *consts)


def _fftconv_kernel(x_ref, gate_ref, skip_ref,
                    kpre_ref, kpim_ref, lfp_ref, lip_ref, twpre_ref, twpim_ref,
                    ksre_ref, ksim_ref, lfs_ref, lis_ref, twsre_ref, twsim_ref,
                    f1f_ref, f1i_ref, o_ref, ap_scr, dp_scr, as_scr, ds_scr, *, hp, hs, nb):
    ct = x_ref.shape[0]
    f1f = f1f_ref[...]
    f1i = f1i_ref[...]

    def run(rows, n_sig, kre, kim, lf, li, twre, twim, a_scr, d_scr):
        def sig(s):
            c, r0, h = rows(s)
            return x_ref[c, r0:r0 + h, :]

        cre, cim = _dft_fwd(sig, lf, twre, twim, f1f, a_scr)
        yre = cre * kre - cim * kim
        yim = cre * kim + cim * kre

        def emit(s, y):
            c, r0, h = rows(s)
            xs = x_ref[c, r0:r0 + h, :]
            o_ref[c, r0:r0 + h, :] = gate_ref[c, r0:r0 + h, :] * (y + skip_ref[c] * xs)

        _dft_inv_real(yre, yim, li, twre, twim, f1i, d_scr, emit)

    n2p = 2 * hp
    run(lambda s: (s, 0, hp), ct,
        kpre_ref[...].reshape(ct * n2p, DFT_N1), kpim_ref[...].reshape(ct * n2p, DFT_N1),
        lfp_ref[...], lip_ref[...], twpre_ref[...][None], twpim_ref[...][None], ap_scr, dp_scr)
    n2s = 2 * hs
    rep = lambda k: jnp.broadcast_to(k[:, None], (ct, nb, n2s, DFT_N1)).reshape(ct * nb * n2s, DFT_N1)
    run(lambda s: (s // nb, hp + (s % nb) * hs, hs), ct * nb,
        rep(ksre_ref[...]), rep(ksim_ref[...]),
        lfs_ref[...], lis_ref[...], twsre_ref[...][None], twsim_ref[...][None], as_scr, ds_scr)


def _fftconv(x, x_off, gate, gate_off, skip, kp, kp_off, ks, ks_off, *, d_model, hp, hs, nb, ct=8):
    r = hp + nb * hs
    cp, cs_ = _consts_dev(2 * hp), _consts_dev(2 * hs)
    cb = lambda off: (lambda i: (off // ct + i, 0, 0))
    full = lambda a: pl.BlockSpec(a.shape, lambda i: (0,) * a.ndim)
    kspec = lambda n2, off: pl.BlockSpec((ct, n2, DFT_N1), cb(off))
    skip3 = jnp.broadcast_to(skip.astype(F32).reshape(d_model, 1, 1), (d_model, 1, DFT_N1))
    pc = [cp["lf_half"], cp["li_half"], cp["twre"], cp["twim"]]
    sc = [cs_["lf_half"], cs_["li_half"], cs_["twre"], cs_["twim"]]
    return pl.pallas_call(
        functools.partial(_fftconv_kernel, hp=hp, hs=hs, nb=nb),
        grid=(d_model // ct,),
        in_specs=[pl.BlockSpec((ct, r, DFT_N1), cb(x_off)), pl.BlockSpec((ct, r, DFT_N1), cb(gate_off)),
                  pl.BlockSpec((ct, 1, DFT_N1), cb(0)),
                  kspec(2 * hp, kp_off), kspec(2 * hp, kp_off)] + [full(a) for a in pc]
                 + [kspec(2 * hs, ks_off), kspec(2 * hs, ks_off)] + [full(a) for a in sc]
                 + [full(cp["f1_fwd"]), full(cp["f1_inv"])],
        out_specs=pl.BlockSpec((ct, r, DFT_N1), cb(0)),
        out_shape=jax.ShapeDtypeStruct((d_model, r, DFT_N1), F32),
        scratch_shapes=[pltpu.VMEM((ct, 4 * hp, DFT_N1), F32), pltpu.VMEM((ct, 4 * hp, DFT_N1), BF16),
                        pltpu.VMEM((ct * nb, 4 * hs, DFT_N1), F32), pltpu.VMEM((ct * nb, 4 * hs, DFT_N1), BF16)],
        compiler_params=_cparams(("parallel",)),
        name="hyena_fftconv",
    )(x, gate, skip3, kp[0], kp[1], *pc, ks[0], ks[1], *sc, cp["f1_fwd"], cp["f1_inv"])


def _outproj_t_kernel(zt_ref, w_ref, res_ref, o_ref):
    z = zt_ref[...].astype(BF16)
    y = lax.dot_general(z, w_ref[...], (((0,), (0,)), ((), ())), preferred_element_type=F32)
    o_ref[...] = res_ref[...] + y


def _outproj_t(zt, w, res, *, tm=512):
    k, m = zt.shape
    n = w.shape[1]
    tm = min(tm, m)
    return pl.pallas_call(
        _outproj_t_kernel,
        grid=(m // tm,),
        in_specs=[pl.BlockSpec((k, tm), lambda i: (0, i)),
                  pl.BlockSpec((k, n), lambda i: (0, 0)),
                  pl.BlockSpec((tm, n), lambda i: (i, 0))],
        out_specs=pl.BlockSpec((tm, n), lambda i: (i, 0)),
        out_shape=jax.ShapeDtypeStruct((m, n), F32),
        compiler_params=_cparams(("parallel",)),
        name="hyena_outproj",
    )(zt, w.astype(BF16), res)


def _softplus(x):
    return jnp.maximum(x, 0.0) + jnp.log1p(jnp.exp(-jnp.abs(x)))


def _gdn_kernel(qf_ref, kf_ref, vf_ref, baf_ref, batf_ref, qb_ref, kb_ref, vb_ref, bab_ref, batb_ref,
                alog_r_ref, dtb_r_ref, alog_c_ref, dtb_c_ref, of_ref, ob_ref, sf_ref, sb_ref, *,
                n_tiles, f_resets, b_resets, n_pairs):
    hp = pl.program_id(0)
    step = pl.program_id(1)
    t = GDN_TILE
    c = GDN_CHUNK
    nc = t // c
    dh = GDN_DH

    @pl.when(_in_set(step, f_resets))
    def _():
        sf_ref[...] = jnp.zeros_like(sf_ref)

    @pl.when(_in_set(n_tiles - 1 - step, b_resets))
    def _():
        sb_ref[...] = jnp.zeros_like(sb_ref)

    ri = lax.broadcasted_iota(jnp.int32, (t, t), 0)
    ci = lax.broadcasted_iota(jnp.int32, (t, t), 1)
    same = (ri // c) == (ci // c)
    rw = lax.broadcasted_iota(jnp.int32, (c, t), 0)
    lw = lax.broadcasted_iota(jnp.int32, (c, t), 1)
    lblk = lw // c
    lsub = lw % c
    rsub = lax.broadcasted_iota(jnp.int32, (t, LANES), 0) % c
    lane = lax.broadcasted_iota(jnp.int32, (t, LANES), 1)
    lrow = lax.broadcasted_iota(jnp.int32, (8, t), 1) % c

    def diag_to_wide(full):
        parts = [jnp.where(lblk == j, full[j * c:(j + 1) * c, :], 0.0) for j in range(nc)]
        return functools.reduce(lambda x, y: x + y, parts)

    def wide_to_diag16(wide):
        return jnp.where(same, jnp.concatenate([wide] * nc, axis=0), 0.0).astype(BF16)

    def l2n(x):
        return x * lax.rsqrt(jnp.sum(x * x, axis=-1, keepdims=True) + RMS_EPS)

    def mm(a, b):
        return jnp.dot(a, b, preferred_element_type=F32)

    dirs = [(0, qf_ref, kf_ref, vf_ref, baf_ref, batf_ref), (1, qb_ref, kb_ref, vb_ref, bab_ref, batb_ref)]

    per_dir = []
    for d, q_ref, k_ref, v_ref, ba_ref, bat_ref in dirs:
        back = d == 1
        incl = (lsub >= rw) if back else (lsub <= rw)
        strict = (lsub > rw) if back else (lsub < rw)
        pairs = []
        for pp in range(n_pairs):
            q = l2n(q_ref[:, pp * dh:(pp + 1) * dh]) * np.float32(GDN_DH ** -0.5)
            k = l2n(k_ref[:, pp * dh:(pp + 1) * dh])
            k16 = k.astype(BF16)
            qk_kk = lax.dot_general(jnp.concatenate([q.astype(BF16), k16], axis=0), k16, (((1,), (1,)), ((), ())),
                                    preferred_element_type=F32)
            pairs.append(dict(q=q, k=k, qk=diag_to_wide(qk_kk[0:t]), kk=diag_to_wide(qk_kk[t:2 * t])))
        ba = ba_ref[...]
        beta_all = jax.nn.sigmoid(ba)
        g_all = -jnp.exp(alog_r_ref[...]) * _softplus(ba + dtb_r_ref[...])
        sh = 1
        while sh < c:
            if back:
                g_all = g_all + jnp.where(rsub < c - sh, pltpu.roll(g_all, t - sh, axis=0), 0.0)
            else:
                g_all = g_all + jnp.where(rsub >= sh, pltpu.roll(g_all, sh, axis=0), 0.0)
            sh *= 2
        per_dir.append(dict(back=back, incl=incl, strict=strict, pairs=pairs,
                            beta_all=beta_all, g_all=g_all, bat_ref=bat_ref, v_ref=v_ref))

    var = []
    for d, pd in enumerate(per_dir):
        back = pd["back"]
        for ph in range(2 * n_pairs):
            pr = pd["pairs"][ph // 2]
            head = 2 * n_pairs * hp + ph
            jb = d * GDN_NV + head
            ja = 2 * GDN_NV + d * GDN_NV + head
            beta_c = jnp.sum(jnp.where(lane == jb, pd["beta_all"], 0.0), axis=1, keepdims=True)
            gc_c = jnp.sum(jnp.where(lane == ja, pd["g_all"], 0.0), axis=1, keepdims=True)
            a_row = pd["bat_ref"][pl.ds(ja, 1), :]
            g_row = -jnp.exp(alog_c_ref[pl.ds(ja, 1), :]) * _softplus(a_row + dtb_c_ref[pl.ds(ja, 1), :])
            g_row = jnp.broadcast_to(g_row, (8, t))
            sh = 1
            while sh < c:
                if back:
                    g_row = g_row + jnp.where(lrow < c - sh, pltpu.roll(g_row, t - sh, axis=1), 0.0)
                else:
                    g_row = g_row + jnp.where(lrow >= sh, pltpu.roll(g_row, sh, axis=1), 0.0)
                sh *= 2
            gc_r = g_row[0:1, :]
            incl = pd["incl"]
            decay = jnp.where(incl, jnp.exp(jnp.where(incl, diag_to_wide(gc_c) - gc_r, 0.0)), 0.0)
            p = jnp.where(pd["strict"], -(diag_to_wide(beta_c) * pr["kk"] * decay), 0.0)
            var.append(dict(pd=pd, pr=pr, ph=ph, beta_c=beta_c, gc_c=gc_c, decay=decay, p=p))

    for v in var:
        v["nn"] = v["p"]
        v["pm"] = mm(v["p"].astype(BF16), wide_to_diag16(v["p"]))
    m = 2
    while 2 * m < c:
        for v in var:
            both = mm(jnp.concatenate([v["pm"], v["nn"]], axis=0).astype(BF16), wide_to_diag16(v["pm"]))
            v["nn"] = v["nn"] + v["pm"] + both[c:2 * c]
            v["pm"] = both[0:c]
        m *= 2
    for v in var:
        v["nn"] = v["nn"] + v["pm"] + mm(v["nn"].astype(BF16), wide_to_diag16(v["pm"]))

    for v in var:
        pd, pr, ph, beta_c, gc_c = v["pd"], v["pr"], v["ph"], v["beta_c"], v["gc_c"]
        vv = pd["v_ref"][:, ph * dh:(ph + 1) * dh]
        rhs = jnp.concatenate([vv * beta_c, pr["k"] * (beta_c * jnp.exp(gc_c))], axis=1)
        v["uw16"] = (rhs + mm(wide_to_diag16(v["nn"]), rhs.astype(BF16))).astype(BF16)
    for v in var:
        pd, pr, gc_c = v["pd"], v["pr"], v["gc_c"]
        qq = mm(wide_to_diag16(jnp.where(pd["incl"], pr["qk"] * v["decay"], 0.0)), v["uw16"])
        gc3 = gc_c.reshape(nc, c, 1)
        gl3 = gc3[:, 0:1, :] if pd["back"] else gc3[:, c - 1:c, :]
        kg16 = (pr["k"] * jnp.exp(jnp.broadcast_to(gl3, (nc, c, 1)).reshape(t, 1) - gc_c)).astype(BF16)
        v["qp16"] = (pr["q"] * jnp.exp(gc_c) - qq[:, dh:2 * dh]).astype(BF16)
        v["qq"] = qq
        v["gl3"] = gl3
        v["rp"] = [lax.dot_general(kg16[n * c:(n + 1) * c], v["uw16"][n * c:(n + 1) * c], (((0,), (0,)), ((), ())),
                                   preferred_element_type=F32) for n in range(nc)]

    zero = jnp.zeros((dh, dh), F32)
    s_refs = [sf_ref, sb_ref]
    outs = [of_ref, ob_ref]
    state = [[s_refs[d][ph] for ph in range(2 * n_pairs)] for d in range(2)]
    for i in range(nc):
        for d in range(2):
            n = nc - 1 - i if d == 1 else i
            sl = slice(n * c, (n + 1) * c)
            for pp in range(n_pairs):
                v0, v1 = var[2 * n_pairs * d + 2 * pp], var[2 * n_pairs * d + 2 * pp + 1]
                s0, s1 = state[d][2 * pp], state[d][2 * pp + 1]
                s_d = jnp.concatenate([jnp.concatenate([s0, zero], axis=1),
                                       jnp.concatenate([zero, s1], axis=1)], axis=0).astype(BF16)
                lhs = jnp.concatenate([
                    jnp.concatenate([v0["rp"][n][:, dh:2 * dh], v1["rp"][n][:, dh:2 * dh]], axis=1).astype(BF16),
                    jnp.concatenate([v0["qp16"][sl], v1["qp16"][sl]], axis=1)], axis=0)
                z = mm(lhs, s_d)
                c0 = 2 * pp * dh
                outs[d][sl, c0:c0 + dh] = z[dh:dh + c, 0:dh] + v0["qq"][sl, 0:dh]
                outs[d][sl, c0 + dh:c0 + 2 * dh] = z[dh:dh + c, dh:2 * dh] + v1["qq"][sl, 0:dh]
                state[d][2 * pp] = jnp.exp(v0["gl3"][n]) * s0 - z[0:dh, 0:dh] + v0["rp"][n][:, 0:dh]
                state[d][2 * pp + 1] = jnp.exp(v1["gl3"][n]) * s1 - z[0:dh, dh:2 * dh] + v1["rp"][n][:, 0:dh]
    for d in range(2):
        for ph in range(2 * n_pairs):
            s_refs[d][ph] = state[d][ph]


def _gdn_scan(qkv, ba, a_log, dt_bias, *, starts, ends, n_pairs=4):
    m = qkv.shape[0]
    t = GDN_TILE
    n_tiles = m // t
    bat = ba.T
    pad = lambda a: jnp.pad(a.astype(F32).reshape(-1), (2 * GDN_NV, LANES - 4 * GDN_NV))
    alog_r = pad(a_log).reshape(1, LANES)
    dtb_r = pad(dt_bias).reshape(1, LANES)
    alog_c = pad(a_log).reshape(LANES, 1)
    dtb_c = pad(dt_bias).reshape(LANES, 1)
    f_resets = tuple(s // t for s in starts)
    b_resets = tuple(e // t - 1 for e in ends)
    fwd = lambda h, s: s
    bwd = lambda h, s: n_tiles - 1 - s

    kw = n_pairs * GDN_DH
    key_blocks = GDN_NK // n_pairs

    def specs(tile):
        return [pl.BlockSpec((t, kw), lambda h, s: (tile(h, s), h)),
                pl.BlockSpec((t, kw), lambda h, s: (tile(h, s), key_blocks + h)),
                pl.BlockSpec((t, 2 * kw), lambda h, s: (tile(h, s), key_blocks + h)),
                pl.BlockSpec((t, LANES), lambda h, s: (tile(h, s), 0)),
                pl.BlockSpec((LANES, t), lambda h, s: (0, tile(h, s)))]

    small = lambda a: pl.BlockSpec(a.shape, lambda h, s: (0, 0))
    out = jax.ShapeDtypeStruct((m, GDN_NV * GDN_DH), F32)
    return pl.pallas_call(
        functools.partial(_gdn_kernel, n_tiles=n_tiles, f_resets=f_resets, b_resets=b_resets, n_pairs=n_pairs),
        grid=(key_blocks, n_tiles),
        in_specs=specs(fwd) + specs(bwd) + [small(alog_r), small(dtb_r), small(alog_c), small(dtb_c)],
        out_specs=[pl.BlockSpec((t, 2 * kw), lambda h, s: (s, h)),
                   pl.BlockSpec((t, 2 * kw), lambda h, s: (n_tiles - 1 - s, h))],
        out_shape=[out, out],
        scratch_shapes=[pltpu.VMEM((2 * n_pairs, GDN_DH, GDN_DH), F32),
                        pltpu.VMEM((2 * n_pairs, GDN_DH, GDN_DH), F32)],
        compiler_params=_cparams(("parallel", "arbitrary")),
        name="gdn_scan",
    )(qkv, qkv, qkv, ba, bat, qkv, qkv, qkv, ba, bat, alog_r, dtb_r, alog_c, dtb_c)


def _gdn_out_kernel(of_ref, ob_ref, z_ref, nw_ref, w_ref, res_ref, o_ref, y_ref):
    nw = nw_ref[...]
    for h in range(GDN_NV):
        sl = slice(h * GDN_DH, (h + 1) * GDN_DH)
        o = of_ref[:, sl] + ob_ref[:, sl]
        o = o * lax.rsqrt(jnp.mean(o * o, axis=-1, keepdims=True) + RMS_EPS) * nw
        z = z_ref[:, sl]
        y_ref[:, sl] = (o * (z * jax.nn.sigmoid(z))).astype(BF16)
    o_ref[...] = res_ref[...] + jnp.dot(y_ref[...], w_ref[...], preferred_element_type=F32)


def _gdn_out(o_f, o_b, z, norm_w, w, res, *, tm=512):
    m, kv = o_f.shape
    n = w.shape[1]
    tm = min(tm, m)
    row = lambda width: pl.BlockSpec((tm, width), lambda i: (i, 0))
    return pl.pallas_call(
        _gdn_out_kernel,
        grid=(m // tm,),
        in_specs=[row(kv), row(kv), row(kv),
                  pl.BlockSpec((1, GDN_DH), lambda i: (0, 0)),
                  pl.BlockSpec((kv, n), lambda i: (0, 0)),
                  row(n)],
        out_specs=row(n),
        out_shape=jax.ShapeDtypeStruct((m, n), F32),
        scratch_shapes=[pltpu.VMEM((tm, kv), BF16)],
        compiler_params=_cparams(("parallel",)),
        name="gdn_out",
    )(o_f, o_b, z, norm_w.astype(F32).reshape(1, GDN_DH), w.astype(BF16), res)


def _swiglu_tile(h, wg, wu, wd):
    a = jnp.dot(h, wg, preferred_element_type=F32)
    u = jnp.dot(h, wu, preferred_element_type=F32)
    act = (a * jax.nn.sigmoid(a) * u).astype(BF16)
    return jnp.dot(act, wd, preferred_element_type=F32)


def _ffn_kernel(x_ref, g_ref, wg_ref, wu_ref, wd_ref, gf_ref, o_ref, h_ref, acc_ref, *, final_norm):
    f = pl.program_id(1)

    @pl.when(f == 0)
    def _():
        h_ref[...] = _norm_rows(x_ref[...], g_ref[...]).astype(BF16)
        acc_ref[...] = jnp.zeros_like(acc_ref)

    acc_ref[...] += _swiglu_tile(h_ref[...], wg_ref[...], wu_ref[...], wd_ref[...])

    @pl.when(f == pl.num_programs(1) - 1)
    def _():
        out = x_ref[...] + acc_ref[...]
        if final_norm:
            out = _norm_rows(out, gf_ref[...])
        o_ref[...] = out


def _ffn(x, gain, w_gate, w_up, w_down, *, final_gain=None, tm=1024, tf=512):
    m, d = x.shape
    ff = w_gate.shape[1]
    tm = min(tm, m)
    tf = min(tf, ff)
    final_norm = final_gain is not None
    gf = (final_gain if final_norm else gain).astype(F32).reshape(1, d)
    return pl.pallas_call(
        functools.partial(_ffn_kernel, final_norm=final_norm),
        grid=(m // tm, ff // tf),
        in_specs=[pl.BlockSpec((tm, d), lambda i, f: (i, 0)),
                  pl.BlockSpec((1, d), lambda i, f: (0, 0)),
                  pl.BlockSpec((d, tf), lambda i, f: (0, f)),
                  pl.BlockSpec((d, tf), lambda i, f: (0, f)),
                  pl.BlockSpec((tf, d), lambda i, f: (f, 0)),
                  pl.BlockSpec((1, d), lambda i, f: (0, 0))],
        out_specs=pl.BlockSpec((tm, d), lambda i, f: (i, 0)),
        out_shape=jax.ShapeDtypeStruct((m, d), F32),
        scratch_shapes=[pltpu.VMEM((tm, d), BF16), pltpu.VMEM((tm, d), F32)],
        compiler_params=_cparams(("parallel", "arbitrary")),
        name="ffn_mixer",
    )(x, gain.astype(F32).reshape(1, d), w_gate.astype(BF16), w_up.astype(BF16), w_down.astype(BF16), gf)


def _moe_kernel(x_ref, g_ref, r_ref, rb_ref, wg_ref, wu_ref, wd_ref, gf_ref, o_ref,
                h_ref, acc_ref, gates_ref, posc_ref, posr_ref, hx_ref, y_ref, nblk_ref, *,
                n_exp, final_norm, rb):
    e = pl.program_id(1)
    f = pl.program_id(2)
    tm = x_ref.shape[0]
    lane = lax.broadcasted_iota(jnp.int32, (tm, LANES), 1)

    @pl.when((e == 0) & (f == 0))
    def _():
        h = _norm_rows(x_ref[...], g_ref[...])
        h_ref[...] = h.astype(BF16)
        acc_ref[...] = jnp.zeros_like(acc_ref)
        logits = jnp.dot(h, r_ref[...], preferred_element_type=F32, precision=HIGHEST) + rb_ref[...]
        logits = jnp.where(lane < n_exp, logits, -jnp.inf)
        m1 = jnp.max(logits, axis=1, keepdims=True)
        i1 = jnp.min(jnp.where(logits == m1, lane, LANES), axis=1, keepdims=True)
        rest = jnp.where(lane == i1, -jnp.inf, logits)
        m2 = jnp.max(rest, axis=1, keepdims=True)
        i2 = jnp.min(jnp.where(rest == m2, lane, LANES), axis=1, keepdims=True)
        e2 = jnp.exp(m2 - m1)
        w1 = 1.0 / (1.0 + e2)
        gates_ref[...] = jnp.where(lane == i1, w1, 0.0) + jnp.where(lane == i2, e2 * w1, 0.0)
        chosen = jnp.where((lane == i1) | (lane == i2), 1.0, 0.0)
        ri = lax.broadcasted_iota(jnp.int32, (tm, tm), 0)
        ci = lax.broadcasted_iota(jnp.int32, (tm, tm), 1)
        before = jnp.where(ci < ri, 1.0, 0.0).astype(BF16)
        rank = jnp.dot(before, chosen.astype(BF16), preferred_element_type=F32)
        posc = jnp.where(chosen > 0.5, rank, -1.0)
        posc_ref[...] = posc
        posr_ref[...] = posc.T
        for ee in range(n_exp):
            cnt = jnp.sum(chosen[:, ee:ee + 1]).astype(jnp.int32)
            nblk_ref[ee] = (cnt + rb - 1) // rb

    nblk = nblk_ref[e]

    @pl.when(f == 0)
    def _():
        pos_row = posr_ref[pl.ds(e, 1), :]
        h = h_ref[...]

        def gather(b, carry):
            r0 = pl.multiple_of(b * rb, rb)
            rows = (lax.broadcasted_iota(jnp.int32, (rb, 1), 0) + r0).astype(F32)
            onehot = jnp.where(pos_row == rows, 1.0, 0.0).astype(BF16)
            hx_ref[pl.ds(r0, rb), :] = jnp.dot(onehot, h, preferred_element_type=F32).astype(BF16)
            y_ref[pl.ds(r0, rb), :] = jnp.zeros((rb, y_ref.shape[1]), F32)
            return carry

        lax.fori_loop(0, nblk, gather, 0)

    def expert(b, carry):
        r0 = pl.multiple_of(b * rb, rb)
        y_ref[pl.ds(r0, rb), :] += _swiglu_tile(hx_ref[pl.ds(r0, rb), :], wg_ref[0], wu_ref[0], wd_ref[0])
        return carry

    lax.fori_loop(0, nblk, expert, 0)

    @pl.when(f == pl.num_programs(2) - 1)
    def _():
        pos_col = jnp.sum(jnp.where(lane == e, posc_ref[...], 0.0), axis=1, keepdims=True)
        gate = jnp.sum(jnp.where(lane == e, gates_ref[...], 0.0), axis=1, keepdims=True)

        def scatter(b, carry):
            r0 = pl.multiple_of(b * rb, rb)
            cols = (lax.broadcasted_iota(jnp.int32, (1, rb), 1) + r0).astype(F32)
            onehot = jnp.where(pos_col == cols, 1.0, 0.0).astype(BF16)
            back = jnp.dot(onehot, y_ref[pl.ds(r0, rb), :].astype(BF16), preferred_element_type=F32)
            acc_ref[...] += gate * back
            return carry

        lax.fori_loop(0, nblk, scatter, 0)

    @pl.when((e == n_exp - 1) & (f == pl.num_programs(2) - 1))
    def _():
        out = x_ref[...] + acc_ref[...]
        if final_norm:
            out = _norm_rows(out, gf_ref[...])
        o_ref[...] = out


def _moe(x, gain, router, router_bias, w_gate, w_up, w_down, *, final_gain=None, tm=1024, tf=896, rb=128):
    m, d = x.shape
    n_exp, _, ff = w_gate.shape
    tm = min(tm, m)
    tf = tf if ff % tf == 0 else min(512, ff)
    r = jnp.pad(router.astype(F32), ((0, 0), (0, LANES - n_exp)))
    rbias = jnp.pad(router_bias.astype(F32).reshape(1, n_exp), ((0, 0), (0, LANES - n_exp)))
    final_norm = final_gain is not None
    gf = (final_gain if final_norm else gain).astype(F32).reshape(1, d)
    return pl.pallas_call(
        functools.partial(_moe_kernel, n_exp=n_exp, final_norm=final_norm, rb=rb),
        grid=(m // tm, n_exp, ff // tf),
        in_specs=[pl.BlockSpec((tm, d), lambda i, e, f: (i, 0)),
                  pl.BlockSpec((1, d), lambda i, e, f: (0, 0)),
                  pl.BlockSpec((d, LANES), lambda i, e, f: (0, 0)),
                  pl.BlockSpec((1, LANES), lambda i, e, f: (0, 0)),
                  pl.BlockSpec((1, d, tf), lambda i, e, f: (e, 0, f)),
                  pl.BlockSpec((1, d, tf), lambda i, e, f: (e, 0, f)),
                  pl.BlockSpec((1, tf, d), lambda i, e, f: (e, f, 0)),
                  pl.BlockSpec((1, d), lambda i, e, f: (0, 0))],
        out_specs=pl.BlockSpec((tm, d), lambda i, e, f: (i, 0)),
        out_shape=jax.ShapeDtypeStruct((m, d), F32),
        scratch_shapes=[pltpu.VMEM((tm, d), BF16), pltpu.VMEM((tm, d), F32), pltpu.VMEM((tm, LANES), F32),
                        pltpu.VMEM((tm, LANES), F32), pltpu.VMEM((LANES, tm), F32),
                        pltpu.VMEM((tm, d), BF16), pltpu.VMEM((tm, d), F32), pltpu.SMEM((n_exp,), jnp.int32)],
        compiler_params=_cparams(("parallel", "arbitrary", "arbitrary")),
        name="moe_mixer",
    )(x, gain.astype(F32).reshape(1, d), r, rbias, w_gate.astype(BF16), w_up.astype(BF16),
      w_down.astype(BF16), gf)


def _hyena_layer(x, gain, lp, ls, nb_p, nb_s, w_in, conv_w, f_w1, f_b1, f_w2, f_b2, f_w3, f_b3, f_w4,
                 f_freq, skip, w_out, *, starts, ends):
    m, d = x.shape
    assert nb_p == 1 and lp % (2 * DFT_N1) == 0 and ls % (2 * DFT_N1) == 0
    hp, hs = lp // DFT_N1, ls // DFT_N1
    ut = _proj(x, gain, w_in, conv_w, transpose_out=True, starts=starts, ends=ends)
    u3 = ut.reshape(3 * d, m // DFT_N1, DFT_N1)
    filt = (f_w1, f_b1, f_w2, f_b2, f_w3, f_b3, f_w4, f_freq)
    kp = _spectrum(_hyena_taps(lp, *filt, d))
    ks = _spectrum(_hyena_taps(ls, *filt, d))
    conv = functools.partial(_fftconv, d_model=d, hp=hp, hs=hs, nb=nb_s)
    z1 = conv(u3, 0, u3, d, skip[0], kp, 0, ks, 0)
    z2 = conv(z1, 0, u3, 2 * d, skip[1], kp, d, ks, d)
    return _outproj_t(z2.reshape(d, m), w_out, x)


def _gdn_layer(x, gain, w_in, conv_w, a_log, dt_bias, norm_w, w_out, *, starts, ends):
    key, val = GDN_NK * GDN_DH, GDN_NV * GDN_DH
    cd = 2 * key + val
    qkv = _proj(x, gain, w_in[:, :cd], conv_w, act="silu", starts=starts, ends=ends)
    z = _proj(x, gain, w_in[:, cd:cd + val])
    w_ba = jnp.pad(w_in[:, cd + val:], ((0, 0), (0, LANES - 4 * GDN_NV)))
    ba = _proj(x, gain, w_ba)
    o_f, o_b = _gdn_scan(qkv, ba, a_log, dt_bias, starts=starts, ends=ends)
    return _gdn_out(o_f, o_b, z, norm_w, w_out, x)


def kernel(x_prompt, x_sample, norm_mix, norm_ffn, norm_final, hy_w_in, hy_conv, hy_f_w1, hy_f_b1, hy_f_w2, hy_f_b2, hy_f_w3, hy_f_b3, hy_f_w4, hy_f_freq, hy_skip, hy_w_out, gdn_w_in, gdn_conv, gdn_a_log, gdn_dt_bias, gdn_norm, gdn_w_out, ffn_w_gate, ffn_w_up, ffn_w_down, moe_router, moe_router_bias, moe_w_gate, moe_w_up, moe_w_down):
    bp, lp, d = x_prompt.shape
    bs, ls, _ = x_sample.shape
    x = jnp.concatenate([x_prompt.reshape(bp * lp, d), x_sample.reshape(bs * ls, d)], axis=0)
    starts = tuple(b * lp for b in range(bp)) + tuple(bp * lp + b * ls for b in range(bs))
    ends = tuple(s + lp for s in starts[:bp]) + tuple(s + ls for s in starts[bp:])
    depth = norm_mix.shape[0]
    for i in range(depth):
        j = i // 2
        last = i == depth - 1
        if i % 2 == 0:
            x = _hyena_layer(x, norm_mix[i], lp, ls, bp, bs, hy_w_in[j], hy_conv[j], hy_f_w1[j], hy_f_b1[j],
                             hy_f_w2[j], hy_f_b2[j], hy_f_w3[j], hy_f_b3[j], hy_f_w4[j], hy_f_freq[j],
                             hy_skip[j], hy_w_out[j], starts=starts, ends=ends)
            x = _ffn(x, norm_ffn[i], ffn_w_gate[j], ffn_w_up[j], ffn_w_down[j],
                     final_gain=norm_final if last else None)
        else:
            x = _gdn_layer(x, norm_mix[i], gdn_w_in[j], gdn_conv[j], gdn_a_log[j], gdn_dt_bias[j],
                           gdn_norm[j], gdn_w_out[j], starts=starts, ends=ends)
            x = _moe(x, norm_ffn[i], moe_router[j], moe_router_bias[j], moe_w_gate[j], moe_w_up[j],
                     moe_w_down[j], final_gain=norm_final if last else None)
    y_prompt = x[:bp * lp].reshape(bp, lp, d)
    y_sample = x[bp * lp:].reshape(bs, ls, d)
    return (y_prompt, y_sample)
```

```python
import functools
import math

import numpy as np
import jax
import jax.numpy as jnp
from jax import lax
from jax.experimental import pallas as pl
from jax.experimental.pallas import tpu as pltpu

F32 = jnp.float32
BF16 = jnp.bfloat16

RMS_EPS = 1e-6
HY_BANDS = 16
HY_FAST_PCT = 0.3
HY_SLOW_PCT = 1.5
HY_TARGET = 1e-2
GDN_NK = 8
GDN_NV = 16
GDN_DH = 128
TOP_K = 2

LANES = 128
HALO = 16
DFT_N1 = 256
GDN_CHUNK = 64
GDN_TILE = 256
VMEM_LIMIT = 56 * 1024 * 1024

HIGHEST = lax.Precision.HIGHEST


def _cparams(sem):
    return pltpu.CompilerParams(dimension_semantics=sem, vmem_limit_bytes=VMEM_LIMIT)


def _in_set(v, values):
    r = v == values[0]
    for b in values[1:]:
        r = jnp.logical_or(r, v == b)
    return r


def _norm_rows(x, g):
    ms = jnp.mean(x * x, axis=-1, keepdims=True)
    return x * lax.rsqrt(ms + RMS_EPS) * g


def _proj_conv_kernel(xp_ref, x_ref, xn_ref, g_ref, w_ref, cw_ref, o_ref, h_ref, *,
                      tm, starts, ends, act, transpose_out):
    i = pl.program_id(0)

    @pl.when(pl.program_id(1) == 0)
    def _():
        g = g_ref[...]
        row0 = i * tm
        hp = jnp.where(_in_set(row0, starts), 0.0, _norm_rows(xp_ref[...], g))
        hn = jnp.where(_in_set(row0 + tm, ends), 0.0, _norm_rows(xn_ref[...], g))
        h_ref[0:HALO, :] = hp.astype(BF16)
        h_ref[HALO:HALO + tm, :] = _norm_rows(x_ref[...], g).astype(BF16)
        h_ref[HALO + tm:HALO + tm + HALO, :] = hn.astype(BF16)

    n = tm + 2 * HALO
    tn = w_ref.shape[1]
    sub = min(tn, 2 * LANES)

    def finish(p, c0):
        up = pltpu.roll(p, 1, axis=0)
        dn = pltpu.roll(p, n - 1, axis=0)
        cw = cw_ref[:, c0:c0 + sub]
        y = (cw[0:1, :] * up[HALO:HALO + tm, :] + cw[1:2, :] * p[HALO:HALO + tm, :]
             + cw[2:3, :] * dn[HALO:HALO + tm, :])
        if act == "silu":
            y = y * jax.nn.sigmoid(y)
        if transpose_out:
            o_ref[c0:c0 + sub, :] = y.T
        else:
            o_ref[:, c0:c0 + sub] = y

    h = h_ref[...]
    prev = None
    for c0 in range(0, tn, sub):
        p = jnp.dot(h, w_ref[:, c0:c0 + sub], preferred_element_type=F32)
        if prev is not None:
            finish(*prev)
        prev = (p, c0)
    finish(*prev)


def _proj_kernel(x_ref, g_ref, w_ref, o_ref, h_ref):
    @pl.when(pl.program_id(1) == 0)
    def _():
        h_ref[...] = _norm_rows(x_ref[...], g_ref[...]).astype(BF16)

    o_ref[...] = jnp.dot(h_ref[...], w_ref[...], preferred_element_type=F32)


def _proj(x, gain, w, conv_w=None, *, act=None, transpose_out=False, starts=(), ends=(),
          tm=1024, tn=512):
    m, d = x.shape
    n = w.shape[1]
    tm = min(tm, m)
    if conv_w is not None and n % (2 * tn) == 0:
        tn = 2 * tn
    tn = min(tn, n)
    grid = (m // tm, n // tn)
    gain = gain.reshape(1, d).astype(F32)
    w = w.astype(BF16)
    if conv_w is None:
        return pl.pallas_call(
            _proj_kernel,
            grid=grid,
            in_specs=[pl.BlockSpec((tm, d), lambda i, j: (i, 0)),
                      pl.BlockSpec((1, d), lambda i, j: (0, 0)),
                      pl.BlockSpec((d, tn), lambda i, j: (0, j))],
            out_specs=pl.BlockSpec((tm, tn), lambda i, j: (i, j)),
            out_shape=jax.ShapeDtypeStruct((m, n), F32),
            scratch_shapes=[pltpu.VMEM((tm, d), BF16)],
            compiler_params=_cparams(("parallel", "arbitrary")),
            name="proj",
        )(x, gain, w)
    hb = tm // HALO
    last = m // HALO - 1
    kern = functools.partial(_proj_conv_kernel, tm=tm, starts=tuple(starts), ends=tuple(ends),
                             act=act, transpose_out=transpose_out)
    if transpose_out:
        out_spec = pl.BlockSpec((tn, tm), lambda i, j: (j, i))
        out_shape = jax.ShapeDtypeStruct((n, m), F32)
    else:
        out_spec = pl.BlockSpec((tm, tn), lambda i, j: (i, j))
        out_shape = jax.ShapeDtypeStruct((m, n), F32)
    return pl.pallas_call(
        kern,
        grid=grid,
        in_specs=[pl.BlockSpec((HALO, d), lambda i, j: (jnp.maximum(i * hb - 1, 0), 0)),
                  pl.BlockSpec((tm, d), lambda i, j: (i, 0)),
                  pl.BlockSpec((HALO, d), lambda i, j: (jnp.minimum((i + 1) * hb, last), 0)),
                  pl.BlockSpec((1, d), lambda i, j: (0, 0)),
                  pl.BlockSpec((d, tn), lambda i, j: (0, j)),
                  pl.BlockSpec((3, tn), lambda i, j: (0, j))],
        out_specs=out_spec,
        out_shape=out_shape,
        scratch_shapes=[pltpu.VMEM((tm + 2 * HALO, d), BF16)],
        compiler_params=_cparams(("parallel", "arbitrary")),
        name="proj_conv",
    )(x, x, x, gain, w, conv_w.astype(F32))


@functools.lru_cache(maxsize=None)
def _dft_consts(n2):
    n1 = DFT_N1
    n = n1 * n2
    h = n2 // 2
    k1 = np.arange(n1)
    f1 = np.exp(-2j * np.pi * np.outer(k1, k1) / n1)
    f1_fwd = np.block([[f1.real, f1.imag], [-f1.imag, f1.real]])
    f1_inv = np.block([[f1.real, -f1.imag], [f1.imag, f1.real]])
    k2 = np.arange(n2)
    f2 = np.exp(-2j * np.pi * np.outer(k2, k2) / n2)
    tw = np.exp(-2j * np.pi * np.outer(k2, k1) / n)
    lf_half = np.concatenate([f2.real[:, :h], f2.imag[:, :h]], axis=0)
    lf_full = np.concatenate([f2.real, f2.imag], axis=0)
    li_half = np.concatenate([f2.real[:h, :], f2.imag[:h, :]], axis=1) / n
    return dict(f1_fwd=f1_fwd, f1_inv=f1_inv, twre=tw.real, twim=tw.imag,
                lf_half=lf_half, lf_full=lf_full, li_half=li_half)


def _consts_dev(n2):
    c = _dft_consts(n2)
    return dict(
        f1_fwd=jnp.asarray(c["f1_fwd"], BF16), f1_inv=jnp.asarray(c["f1_inv"], BF16),
        twre=jnp.asarray(c["twre"], F32), twim=jnp.asarray(c["twim"], F32),
        lf_half=jnp.asarray(c["lf_half"], BF16), lf_full=jnp.asarray(c["lf_full"], BF16),
        li_half=jnp.asarray(c["li_half"], BF16))


def _dft_fwd(sig, lf, twre, twim, f1f, a_scr):
    ns, two_n2, _ = a_scr.shape
    n2 = two_n2 // 2
    for s in range(ns):
        a_scr[s] = jnp.dot(lf, sig(s).astype(BF16), preferred_element_type=F32)
    are = a_scr[:, 0:n2, :]
    aim = a_scr[:, n2:two_n2, :]
    bre = (are * twre - aim * twim).reshape(ns * n2, DFT_N1)
    bim = (are * twim + aim * twre).reshape(ns * n2, DFT_N1)
    bcat = jnp.concatenate([bre, bim], axis=1).astype(BF16)
    cc = jnp.dot(bcat, f1f, preferred_element_type=F32)
    return cc[:, 0:DFT_N1], cc[:, DFT_N1:2 * DFT_N1]


def _dft_inv_real(yre, yim, li, twre, twim, f1i, d_scr, emit):
    ns, two_n2, _ = d_scr.shape
    n2 = two_n2 // 2
    ycat = jnp.concatenate([yre, yim], axis=1).astype(BF16)
    bb = jnp.dot(ycat, f1i, preferred_element_type=F32)
    bre = bb[:, 0:DFT_N1].reshape(ns, n2, DFT_N1)
    bim = bb[:, DFT_N1:2 * DFT_N1].reshape(ns, n2, DFT_N1)
    d_scr[:, 0:n2, :] = (bre * twre + bim * twim).astype(BF16)
    d_scr[:, n2:two_n2, :] = (bim * twre - bre * twim).astype(BF16)
    for s in range(ns):
        emit(s, jnp.dot(li, d_scr[s], preferred_element_type=F32))


def _taps_kernel(bands_ref, w1t_ref, w1c_ref, w1s_ref, b1_ref, w2_ref, b2_ref, w3_ref, b3_ref,
                 fq_ref, w4_ref, dl_ref, o_ref, h_ref, tm_ref, *, seq_len, tl):
    nt = pl.program_id(0)

    @pl.when(pl.program_id(1) == 0)
    def _():
        col = nt * tl + lax.broadcasted_iota(jnp.int32, (1, tl), 1)
        pos = jnp.where(col < seq_len, col, 2 * seq_len - col)
        posf = pos.astype(F32)
        t = posf * np.float32(1.0 / (seq_len - 1))
        ang = bands_ref[...] * (posf * np.float32(2.0 * math.pi / seq_len))
        fq = fq_ref[...]
        pre = (w1t_ref[...] * t
               + jnp.dot(w1c_ref[...], jnp.cos(ang), preferred_element_type=F32, precision=HIGHEST)
               - jnp.dot(w1s_ref[...], jnp.sin(ang), preferred_element_type=F32, precision=HIGHEST))
        h = jnp.sin(fq * (pre + b1_ref[...]))
        h = jnp.sin(fq * (jnp.dot(w2_ref[...], h, preferred_element_type=F32, precision=HIGHEST) + b2_ref[...]))
        h = jnp.sin(fq * (jnp.dot(w3_ref[...], h, preferred_element_type=F32, precision=HIGHEST) + b3_ref[...]))
        h_hi = h.astype(BF16)
        h_ref[0] = h_hi
        h_ref[1] = (h - h_hi.astype(F32)).astype(BF16)
        tm_ref[0:1, :] = t
        tm_ref[1:2, :] = jnp.where(col == seq_len, 0.0, 1.0)

    w_hi = w4_ref[0]
    h_hi = h_ref[0]
    taps = (jnp.dot(w_hi, h_hi, preferred_element_type=F32) + jnp.dot(w_hi, h_ref[1], preferred_element_type=F32)
            + jnp.dot(w4_ref[1], h_hi, preferred_element_type=F32))
    window = jnp.exp(-(dl_ref[...] * tm_ref[0:1, :]))
    o_ref[...] = taps * window * tm_ref[1:2, :]


def _hyena_taps(seq_len, f_w1, f_b1, f_w2, f_b2, f_w3, f_b3, f_w4, f_freq, d_model, *, tl=2048, td=512):
    fh = f_w1.shape[1]
    n_ord = f_w4.shape[1] // (2 * d_model)
    tl = min(tl, seq_len)
    ndt = d_model // td
    nlt = seq_len // tl
    col = lambda v: v.reshape(-1, 1).astype(F32)
    bands = jnp.linspace(1e-4, HY_BANDS - 1, HY_BANDS, dtype=F32).reshape(-1, 1)
    w1 = f_w1.astype(F32).T
    w4t = f_w4.astype(F32).reshape(fh, n_ord, 2, d_model).transpose(1, 2, 3, 0).reshape(n_ord * 2 * d_model, fh)
    w4_hi = w4t.astype(BF16)
    w4t = jnp.stack([w4_hi, (w4t - w4_hi.astype(F32)).astype(BF16)])
    max_decay = math.log(HY_TARGET) / HY_FAST_PCT
    min_decay = math.log(HY_TARGET) / HY_SLOW_PCT
    deltas = jnp.abs(jnp.linspace(min_decay, max_decay, d_model, dtype=F32))
    deltas = jnp.tile(deltas, n_ord).reshape(-1, 1)
    small = lambda a: pl.BlockSpec(a.shape, lambda n, j: (0, 0))
    args = [bands, w1[:, 0:1], w1[:, 1:1 + HY_BANDS], w1[:, 1 + HY_BANDS:1 + 2 * HY_BANDS], col(f_b1),
            f_w2.astype(F32).T, col(f_b2), f_w3.astype(F32).T, col(f_b3), col(f_freq)]
    return pl.pallas_call(
        functools.partial(_taps_kernel, seq_len=seq_len, tl=tl),
        grid=(2 * nlt, n_ord * ndt),
        in_specs=[small(a) for a in args] + [
            pl.BlockSpec((2, td, fh),
                         lambda n, j: (0, (j // ndt * 2 + (n >= nlt).astype(jnp.int32)) * ndt + j % ndt, 0)),
            pl.BlockSpec((td, 1), lambda n, j: (j, 0))],
        out_specs=pl.BlockSpec((td, tl), lambda n, j: (j, n)),
        out_shape=jax.ShapeDtypeStruct((n_ord * d_model, 2 * seq_len), F32),
        scratch_shapes=[pltpu.VMEM((2, fh, tl), BF16), pltpu.VMEM((8, tl), F32)],
        compiler_params=_cparams(("parallel", "arbitrary")),
        name="hyena_taps",
    )(*args, w4t, deltas)


def _spectrum_kernel(x_ref, lf_ref, twre_ref, twim_ref, f1f_ref, re_ref, im_ref, a_scr):
    ct, n2, _ = x_ref.shape
    cre, cim = _dft_fwd(lambda s: x_ref[s], lf_ref[...], twre_ref[...][None], twim_ref[...][None],
                        f1f_ref[...], a_scr)
    re_ref[...] = cre.reshape(ct, n2, DFT_N1)
    im_ref[...] = cim.reshape(ct, n2, DFT_N1)


def _spectrum(taps, *, ct=8):
    c, n = taps.shape
    n2 = n // DFT_N1
    cs = _consts_dev(n2)
    x = taps.reshape(c, n2, DFT_N1)
    blk = pl.BlockSpec((ct, n2, DFT_N1), lambda i: (i, 0, 0))
    full = lambda a: pl.BlockSpec(a.shape, lambda i: (0,) * a.ndim)
    consts = [cs["lf_full"], cs["twre"], cs["twim"], cs["f1_fwd"]]
    return pl.pallas_call(
        _spectrum_kernel,
        grid=(c // ct,),
        in_specs=[blk] + [full(a) for a in consts],
        out_specs=[blk, blk],
        out_shape=[jax.ShapeDtypeStruct((c, n2, DFT_N1), F32)] * 2,
        scratch_shapes=[pltpu.VMEM((ct, 2 * n2, DFT_N1), F32)],
        compiler_params=_cparams(("parallel",)),
        name="hyena_spectrum",
    )(x, *consts)


def _fftconv_kernel(x_ref, gate_ref, skip_ref,
                    kpre_ref, kpim_ref, lfp_ref, lip_ref, twpre_ref, twpim_ref,
                    ksre_ref, ksim_ref, lfs_ref, lis_ref, twsre_ref, twsim_ref,
                    f1f_ref, f1i_ref, o_ref, ap_scr, dp_scr, as_scr, ds_scr, *, hp, hs, nb):
    ct = x_ref.shape[0]
    f1f = f1f_ref[...]
    f1i = f1i_ref[...]

    def run(rows, n_sig, kre, kim, lf, li, twre, twim, a_scr, d_scr):
        def sig(s):
            c, r0, h = rows(s)
            return x_ref[c, r0:r0 + h, :]

        cre, cim = _dft_fwd(sig, lf, twre, twim, f1f, a_scr)
        yre = cre * kre - cim * kim
        yim = cre * kim + cim * kre

        def emit(s, y):
            c, r0, h = rows(s)
            xs = x_ref[c, r0:r0 + h, :]
            o_ref[c, r0:r0 + h, :] = gate_ref[c, r0:r0 + h, :] * (y + skip_ref[c] * xs)

        _dft_inv_real(yre, yim, li, twre, twim, f1i, d_scr, emit)

    n2p = 2 * hp
    run(lambda s: (s, 0, hp), ct,
        kpre_ref[...].reshape(ct * n2p, DFT_N1), kpim_ref[...].reshape(ct * n2p, DFT_N1),
        lfp_ref[...], lip_ref[...], twpre_ref[...][None], twpim_ref[...][None], ap_scr, dp_scr)
    n2s = 2 * hs
    rep = lambda k: jnp.broadcast_to(k[:, None], (ct, nb, n2s, DFT_N1)).reshape(ct * nb * n2s, DFT_N1)
    run(lambda s: (s // nb, hp + (s % nb) * hs, hs), ct * nb,
        rep(ksre_ref[...]), rep(ksim_ref[...]),
        lfs_ref[...], lis_ref[...], twsre_ref[...][None], twsim_ref[...][None], as_scr, ds_scr)


def _fftconv(x, x_off, gate, gate_off, skip, kp, kp_off, ks, ks_off, *, d_model, hp, hs, nb, ct=8):
    r = hp + nb * hs
    cp, cs_ = _consts_dev(2 * hp), _consts_dev(2 * hs)
    cb = lambda off: (lambda i: (off // ct + i, 0, 0))
    full = lambda a: pl.BlockSpec(a.shape, lambda i: (0,) * a.ndim)
    kspec = lambda n2, off: pl.BlockSpec((ct, n2, DFT_N1), cb(off))
    skip3 = jnp.broadcast_to(skip.astype(F32).reshape(d_model, 1, 1), (d_model, 1, DFT_N1))
    pc = [cp["lf_half"], cp["li_half"], cp["twre"], cp["twim"]]
    sc = [cs_["lf_half"], cs_["li_half"], cs_["twre"], cs_["twim"]]
    return pl.pallas_call(
        functools.partial(_fftconv_kernel, hp=hp, hs=hs, nb=nb),
        grid=(d_model // ct,),
        in_specs=[pl.BlockSpec((ct, r, DFT_N1), cb(x_off)), pl.BlockSpec((ct, r, DFT_N1), cb(gate_off)),
                  pl.BlockSpec((ct, 1, DFT_N1), cb(0)),
                  kspec(2 * hp, kp_off), kspec(2 * hp, kp_off)] + [full(a) for a in pc]
                 + [kspec(2 * hs, ks_off), kspec(2 * hs, ks_off)] + [full(a) for a in sc]
                 + [full(cp["f1_fwd"]), full(cp["f1_inv"])],
        out_specs=pl.BlockSpec((ct, r, DFT_N1), cb(0)),
        out_shape=jax.ShapeDtypeStruct((d_model, r, DFT_N1), F32),
        scratch_shapes=[pltpu.VMEM((ct, 4 * hp, DFT_N1), F32), pltpu.VMEM((ct, 4 * hp, DFT_N1), BF16),
                        pltpu.VMEM((ct * nb, 4 * hs, DFT_N1), F32), pltpu.VMEM((ct * nb, 4 * hs, DFT_N1), BF16)],
        compiler_params=_cparams(("parallel",)),
        name="hyena_fftconv",
    )(x, gate, skip3, kp[0], kp[1], *pc, ks[0], ks[1], *sc, cp["f1_fwd"], cp["f1_inv"])


def _outproj_t_kernel(zt_ref, w_ref, res_ref, o_ref):
    z = zt_ref[...].astype(BF16)
    y = lax.dot_general(z, w_ref[...], (((0,), (0,)), ((), ())), preferred_element_type=F32)
    o_ref[...] = res_ref[...] + y


def _outproj_t(zt, w, res, *, tm=512):
    k, m = zt.shape
    n = w.shape[1]
    tm = min(tm, m)
    return pl.pallas_call(
        _outproj_t_kernel,
        grid=(m // tm,),
        in_specs=[pl.BlockSpec((k, tm), lambda i: (0, i)),
                  pl.BlockSpec((k, n), lambda i: (0, 0)),
                  pl.BlockSpec((tm, n), lambda i: (i, 0))],
        out_specs=pl.BlockSpec((tm, n), lambda i: (i, 0)),
        out_shape=jax.ShapeDtypeStruct((m, n), F32),
        compiler_params=_cparams(("parallel",)),
        name="hyena_outproj",
    )(zt, w.astype(BF16), res)


def _softplus(x):
    return jnp.maximum(x, 0.0) + jnp.log1p(jnp.exp(-jnp.abs(x)))


def _gdn_kernel(qf_ref, kf_ref, vf_ref, baf_ref, batf_ref, qb_ref, kb_ref, vb_ref, bab_ref, batb_ref,
                alog_r_ref, dtb_r_ref, alog_c_ref, dtb_c_ref, of_ref, ob_ref, sf_ref, sb_ref, *,
                n_tiles, f_resets, b_resets, n_pairs):
    hp = pl.program_id(0)
    step = pl.program_id(1)
    t = GDN_TILE
    c = GDN_CHUNK
    nc = t // c
    dh = GDN_DH

    @pl.when(_in_set(step, f_resets))
    def _():
        sf_ref[...] = jnp.zeros_like(sf_ref)

    @pl.when(_in_set(n_tiles - 1 - step, b_resets))
    def _():
        sb_ref[...] = jnp.zeros_like(sb_ref)

    ri = lax.broadcasted_iota(jnp.int32, (t, t), 0)
    ci = lax.broadcasted_iota(jnp.int32, (t, t), 1)
    same = (ri // c) == (ci // c)
    rw = lax.broadcasted_iota(jnp.int32, (c, t), 0)
    lw = lax.broadcasted_iota(jnp.int32, (c, t), 1)
    lblk = lw // c
    lsub = lw % c
    rsub = lax.broadcasted_iota(jnp.int32, (t, LANES), 0) % c
    lane = lax.broadcasted_iota(jnp.int32, (t, LANES), 1)
    lrow = lax.broadcasted_iota(jnp.int32, (8, t), 1) % c

    def diag_to_wide(full):
        parts = [jnp.where(lblk == j, full[j * c:(j + 1) * c, :], 0.0) for j in range(nc)]
        return functools.reduce(lambda x, y: x + y, parts)

    def wide_to_diag16(wide):
        return jnp.where(same, jnp.concatenate([wide] * nc, axis=0), 0.0).astype(BF16)

    def l2n(x):
        return x * lax.rsqrt(jnp.sum(x * x, axis=-1, keepdims=True) + RMS_EPS)

    def mm(a, b):
        return jnp.dot(a, b, preferred_element_type=F32)

    dirs = [(0, qf_ref, kf_ref, vf_ref, baf_ref, batf_ref), (1, qb_ref, kb_ref, vb_ref, bab_ref, batb_ref)]

    def stage1_dir(d):
        _, _, _, v_ref, ba_ref, bat_ref = dirs[d]
        back = d == 1
        ba = ba_ref[...]
        g_all = -jnp.exp(alog_r_ref[...]) * _softplus(ba + dtb_r_ref[...])
        sh = 1
        while sh < c:
            if back:
                g_all = g_all + jnp.where(rsub < c - sh, pltpu.roll(g_all, t - sh, axis=0), 0.0)
            else:
                g_all = g_all + jnp.where(rsub >= sh, pltpu.roll(g_all, sh, axis=0), 0.0)
            sh *= 2
        return dict(back=back, incl=(lsub >= rw) if back else (lsub <= rw),
                    strict=(lsub > rw) if back else (lsub < rw),
                    beta_all=jax.nn.sigmoid(ba), g_all=g_all, bat_ref=bat_ref, v_ref=v_ref)

    def stage1_pair(d, pp):
        _, q_ref, k_ref, _, _, _ = dirs[d]
        q = l2n(q_ref[:, pp * dh:(pp + 1) * dh]) * np.float32(GDN_DH ** -0.5)
        k = l2n(k_ref[:, pp * dh:(pp + 1) * dh])
        k16 = k.astype(BF16)
        qk_kk = lax.dot_general(jnp.concatenate([q.astype(BF16), k16], axis=0), k16, (((1,), (1,)), ((), ())),
                                preferred_element_type=F32)
        return dict(q=q, k=k, qk=diag_to_wide(qk_kk[0:t]), kk=diag_to_wide(qk_kk[t:2 * t]))

    def stage2(d, pd, pr, ph):
        back = pd["back"]
        head = 2 * n_pairs * hp + ph
        jb = d * GDN_NV + head
        ja = 2 * GDN_NV + d * GDN_NV + head
        beta_c = jnp.sum(jnp.where(lane == jb, pd["beta_all"], 0.0), axis=1, keepdims=True)
        gc_c = jnp.sum(jnp.where(lane == ja, pd["g_all"], 0.0), axis=1, keepdims=True)
        a_row = pd["bat_ref"][pl.ds(ja, 1), :]
        g_row = -jnp.exp(alog_c_ref[pl.ds(ja, 1), :]) * _softplus(a_row + dtb_c_ref[pl.ds(ja, 1), :])
        g_row = jnp.broadcast_to(g_row, (8, t))
        sh = 1
        while sh < c:
            if back:
                g_row = g_row + jnp.where(lrow < c - sh, pltpu.roll(g_row, t - sh, axis=1), 0.0)
            else:
                g_row = g_row + jnp.where(lrow >= sh, pltpu.roll(g_row, sh, axis=1), 0.0)
            sh *= 2
        gc_r = g_row[0:1, :]
        incl = pd["incl"]
        decay = jnp.where(incl, jnp.exp(jnp.where(incl, diag_to_wide(gc_c) - gc_r, 0.0)), 0.0)
        p = jnp.where(pd["strict"], -(diag_to_wide(beta_c) * pr["kk"] * decay), 0.0)
        return dict(pd=pd, pr=pr, ph=ph, beta_c=beta_c, gc_c=gc_c, decay=decay, p=p)

    def stage3(group):
        for v in group:
            v["nn"] = v["p"]
            v["pm"] = mm(v["p"].astype(BF16), wide_to_diag16(v["p"]))
            yield
        m = 2
        while 2 * m < c:
            for v in group:
                both = mm(jnp.concatenate([v["pm"], v["nn"]], axis=0).astype(BF16), wide_to_diag16(v["pm"]))
                v["nn"] = v["nn"] + v["pm"] + both[c:2 * c]
                v["pm"] = both[0:c]
                yield
            m *= 2
        for v in group:
            v["nn"] = v["nn"] + v["pm"] + mm(v["nn"].astype(BF16), wide_to_diag16(v["pm"]))
            yield

    def stage4(group):
        for v in group:
            pd, pr, ph, beta_c, gc_c = v["pd"], v["pr"], v["ph"], v["beta_c"], v["gc_c"]
            vv = pd["v_ref"][:, ph * dh:(ph + 1) * dh]
            rhs = jnp.concatenate([vv * beta_c, pr["k"] * (beta_c * jnp.exp(gc_c))], axis=1)
            v["uw16"] = (rhs + mm(wide_to_diag16(v["nn"]), rhs.astype(BF16))).astype(BF16)
        for v in group:
            pd, pr, gc_c = v["pd"], v["pr"], v["gc_c"]
            qq = mm(wide_to_diag16(jnp.where(pd["incl"], pr["qk"] * v["decay"], 0.0)), v["uw16"])
            gc3 = gc_c.reshape(nc, c, 1)
            gl3 = gc3[:, 0:1, :] if pd["back"] else gc3[:, c - 1:c, :]
            kg16 = (pr["k"] * jnp.exp(jnp.broadcast_to(gl3, (nc, c, 1)).reshape(t, 1) - gc_c)).astype(BF16)
            v["qp16"] = (pr["q"] * jnp.exp(gc_c) - qq[:, dh:2 * dh]).astype(BF16)
            v["qq"] = qq
            v["gl3"] = gl3
            v["rp"] = [lax.dot_general(kg16[n * c:(n + 1) * c], v["uw16"][n * c:(n + 1) * c],
                                       (((0,), (0,)), ((), ())), preferred_element_type=F32)
                       for n in range(nc)]

    var = {}
    groups = [[], []]

    def setup(d):
        pd = stage1_dir(d)
        yield
        for pp in range(n_pairs):
            pr = stage1_pair(d, pp)
            yield
            for ph in (2 * pp, 2 * pp + 1):
                var[(d, ph)] = stage2(d, pd, pr, ph)
                groups[d].append(var[(d, ph)])
                yield

    def run(main, filler=None, every=1):
        for i, _ in enumerate(main):
            if filler is not None and i % every == every - 1:
                next(filler, None)
        if filler is not None:
            for _ in filler:
                pass

    n_setup = 1 + 3 * n_pairs
    n_doubling = 2 * n_pairs * (c.bit_length() - 1)
    run(setup(0))
    run(stage3(groups[0]), setup(1), every=max(1, n_doubling // n_setup))
    run(stage3(groups[1]))
    stage4(groups[0])
    stage4(groups[1])

    zero = jnp.zeros((dh, dh), F32)
    s_refs = [sf_ref, sb_ref]
    outs = [of_ref, ob_ref]
    state = [[s_refs[d][ph] for ph in range(2 * n_pairs)] for d in range(2)]
    for i in range(nc):
        for d in range(2):
            n = nc - 1 - i if d == 1 else i
            sl = slice(n * c, (n + 1) * c)
            for pp in range(n_pairs):
                v0, v1 = var[(d, 2 * pp)], var[(d, 2 * pp + 1)]
                s0, s1 = state[d][2 * pp], state[d][2 * pp + 1]
                s_d = jnp.concatenate([jnp.concatenate([s0, zero], axis=1),
                                       jnp.concatenate([zero, s1], axis=1)], axis=0).astype(BF16)
                lhs = jnp.concatenate([
                    jnp.concatenate([v0["rp"][n][:, dh:2 * dh], v1["rp"][n][:, dh:2 * dh]], axis=1).astype(BF16),
                    jnp.concatenate([v0["qp16"][sl], v1["qp16"][sl]], axis=1)], axis=0)
                z = mm(lhs, s_d)
                c0 = 2 * pp * dh
                outs[d][sl, c0:c0 + dh] = z[dh:dh + c, 0:dh] + v0["qq"][sl, 0:dh]
                outs[d][sl, c0 + dh:c0 + 2 * dh] = z[dh:dh + c, dh:2 * dh] + v1["qq"][sl, 0:dh]
                state[d][2 * pp] = jnp.exp(v0["gl3"][n]) * s0 - z[0:dh, 0:dh] + v0["rp"][n][:, 0:dh]
                state[d][2 * pp + 1] = jnp.exp(v1["gl3"][n]) * s1 - z[0:dh, dh:2 * dh] + v1["rp"][n][:, 0:dh]
    for d in range(2):
        for ph in range(2 * n_pairs):
            s_refs[d][ph] = state[d][ph]


def _gdn_scan(qkv, ba, a_log, dt_bias, *, starts, ends, n_pairs=4):
    m = qkv.shape[0]
    t = GDN_TILE
    n_tiles = m // t
    bat = ba.T
    pad = lambda a: jnp.pad(a.astype(F32).reshape(-1), (2 * GDN_NV, LANES - 4 * GDN_NV))
    alog_r = pad(a_log).reshape(1, LANES)
    dtb_r = pad(dt_bias).reshape(1, LANES)
    alog_c = pad(a_log).reshape(LANES, 1)
    dtb_c = pad(dt_bias).reshape(LANES, 1)
    f_resets = tuple(s // t for s in starts)
    b_resets = tuple(e // t - 1 for e in ends)
    fwd = lambda h, s: s
    bwd = lambda h, s: n_tiles - 1 - s

    kw = n_pairs * GDN_DH
    key_blocks = GDN_NK // n_pairs

    def specs(tile):
        return [pl.BlockSpec((t, kw), lambda h, s: (tile(h, s), h)),
                pl.BlockSpec((t, kw), lambda h, s: (tile(h, s), key_blocks + h)),
                pl.BlockSpec((t, 2 * kw), lambda h, s: (tile(h, s), key_blocks + h)),
                pl.BlockSpec((t, LANES), lambda h, s: (tile(h, s), 0)),
                pl.BlockSpec((LANES, t), lambda h, s: (0, tile(h, s)))]

    small = lambda a: pl.BlockSpec(a.shape, lambda h, s: (0, 0))
    out = jax.ShapeDtypeStruct((m, GDN_NV * GDN_DH), F32)
    return pl.pallas_call(
        functools.partial(_gdn_kernel, n_tiles=n_tiles, f_resets=f_resets, b_resets=b_resets, n_pairs=n_pairs),
        grid=(key_blocks, n_tiles),
        in_specs=specs(fwd) + specs(bwd) + [small(alog_r), small(dtb_r), small(alog_c), small(dtb_c)],
        out_specs=[pl.BlockSpec((t, 2 * kw), lambda h, s: (s, h)),
                   pl.BlockSpec((t, 2 * kw), lambda h, s: (n_tiles - 1 - s, h))],
        out_shape=[out, out],
        scratch_shapes=[pltpu.VMEM((2 * n_pairs, GDN_DH, GDN_DH), F32),
                        pltpu.VMEM((2 * n_pairs, GDN_DH, GDN_DH), F32)],
        compiler_params=_cparams(("parallel", "arbitrary")),
        name="gdn_scan",
    )(qkv, qkv, qkv, ba, bat, qkv, qkv, qkv, ba, bat, alog_r, dtb_r, alog_c, dtb_c)


def _gdn_out_kernel(of_ref, ob_ref, z_ref, nw_ref, w_ref, res_ref, o_ref, y_ref):
    nw = nw_ref[...]
    for h in range(GDN_NV):
        sl = slice(h * GDN_DH, (h + 1) * GDN_DH)
        o = of_ref[:, sl] + ob_ref[:, sl]
        o = o * lax.rsqrt(jnp.mean(o * o, axis=-1, keepdims=True) + RMS_EPS) * nw
        z = z_ref[:, sl]
        y_ref[:, sl] = (o * (z * jax.nn.sigmoid(z))).astype(BF16)
    o_ref[...] = res_ref[...] + jnp.dot(y_ref[...], w_ref[...], preferred_element_type=F32)


def _gdn_out(o_f, o_b, z, norm_w, w, res, *, tm=512):
    m, kv = o_f.shape
    n = w.shape[1]
    tm = min(tm, m)
    row = lambda width: pl.BlockSpec((tm, width), lambda i: (i, 0))
    return pl.pallas_call(
        _gdn_out_kernel,
        grid=(m // tm,),
        in_specs=[row(kv), row(kv), row(kv),
                  pl.BlockSpec((1, GDN_DH), lambda i: (0, 0)),
                  pl.BlockSpec((kv, n), lambda i: (0, 0)),
                  row(n)],
        out_specs=row(n),
        out_shape=jax.ShapeDtypeStruct((m, n), F32),
        scratch_shapes=[pltpu.VMEM((tm, kv), BF16)],
        compiler_params=_cparams(("parallel",)),
        name="gdn_out",
    )(o_f, o_b, z, norm_w.astype(F32).reshape(1, GDN_DH), w.astype(BF16), res)


def _swiglu_tile(h, wg, wu, wd):
    a = jnp.dot(h, wg, preferred_element_type=F32)
    u = jnp.dot(h, wu, preferred_element_type=F32)
    act = (a * jax.nn.sigmoid(a) * u).astype(BF16)
    return jnp.dot(act, wd, preferred_element_type=F32)


def _ffn_kernel(x_ref, g_ref, wg_ref, wu_ref, wd_ref, gf_ref, o_ref, h_ref, acc_ref, *, final_norm):
    f = pl.program_id(1)

    @pl.when(f == 0)
    def _():
        h_ref[...] = _norm_rows(x_ref[...], g_ref[...]).astype(BF16)
        acc_ref[...] = jnp.zeros_like(acc_ref)

    acc_ref[...] += _swiglu_tile(h_ref[...], wg_ref[...], wu_ref[...], wd_ref[...])

    @pl.when(f == pl.num_programs(1) - 1)
    def _():
        out = x_ref[...] + acc_ref[...]
        if final_norm:
            out = _norm_rows(out, gf_ref[...])
        o_ref[...] = out


def _ffn(x, gain, w_gate, w_up, w_down, *, final_gain=None, tm=1024, tf=512):
    m, d = x.shape
    ff = w_gate.shape[1]
    tm = min(tm, m)
    tf = min(tf, ff)
    final_norm = final_gain is not None
    gf = (final_gain if final_norm else gain).astype(F32).reshape(1, d)
    return pl.pallas_call(
        functools.partial(_ffn_kernel, final_norm=final_norm),
        grid=(m // tm, ff // tf),
        in_specs=[pl.BlockSpec((tm, d), lambda i, f: (i, 0)),
                  pl.BlockSpec((1, d), lambda i, f: (0, 0)),
                  pl.BlockSpec((d, tf), lambda i, f: (0, f)),
                  pl.BlockSpec((d, tf), lambda i, f: (0, f)),
                  pl.BlockSpec((tf, d), lambda i, f: (f, 0)),
                  pl.BlockSpec((1, d), lambda i, f: (0, 0))],
        out_specs=pl.BlockSpec((tm, d), lambda i, f: (i, 0)),
        out_shape=jax.ShapeDtypeStruct((m, d), F32),
        scratch_shapes=[pltpu.VMEM((tm, d), BF16), pltpu.VMEM((tm, d), F32)],
        compiler_params=_cparams(("parallel", "arbitrary")),
        name="ffn_mixer",
    )(x, gain.astype(F32).reshape(1, d), w_gate.astype(BF16), w_up.astype(BF16), w_down.astype(BF16), gf)


def _moe_kernel(x_ref, g_ref, r_ref, rb_ref, wg_ref, wu_ref, wd_ref, gf_ref, o_ref,
                h_ref, acc_ref, gates_ref, posc_ref, posr_ref, hx_ref, y_ref, nblk_ref, *,
                n_exp, final_norm, rb):
    e = pl.program_id(1)
    f = pl.program_id(2)
    tm = x_ref.shape[0]
    lane = lax.broadcasted_iota(jnp.int32, (tm, LANES), 1)

    @pl.when((e == 0) & (f == 0))
    def _():
        h = _norm_rows(x_ref[...], g_ref[...])
        h_ref[...] = h.astype(BF16)
        acc_ref[...] = jnp.zeros_like(acc_ref)
        h_hi = h_ref[...]
        h_lo = (h - h_hi.astype(F32)).astype(BF16)
        r_hi = r_ref[0]
        logits = (jnp.dot(h_hi, r_hi, preferred_element_type=F32)
                  + jnp.dot(h_hi, r_ref[1], preferred_element_type=F32)
                  + jnp.dot(h_lo, r_hi, preferred_element_type=F32)) + rb_ref[...]
        logits = jnp.where(lane < n_exp, logits, -jnp.inf)
        m1 = jnp.max(logits, axis=1, keepdims=True)
        i1 = jnp.min(jnp.where(logits == m1, lane, LANES), axis=1, keepdims=True)
        rest = jnp.where(lane == i1, -jnp.inf, logits)
        m2 = jnp.max(rest, axis=1, keepdims=True)
        i2 = jnp.min(jnp.where(rest == m2, lane, LANES), axis=1, keepdims=True)
        e2 = jnp.exp(m2 - m1)
        w1 = 1.0 / (1.0 + e2)
        gates_ref[...] = jnp.where(lane == i1, w1, 0.0) + jnp.where(lane == i2, e2 * w1, 0.0)
        chosen = jnp.where((lane == i1) | (lane == i2), 1.0, 0.0)
        ri = lax.broadcasted_iota(jnp.int32, (tm, tm), 0)
        ci = lax.broadcasted_iota(jnp.int32, (tm, tm), 1)
        before = jnp.where(ci < ri, 1.0, 0.0).astype(BF16)
        rank = jnp.dot(before, chosen.astype(BF16), preferred_element_type=F32)
        posc = jnp.where(chosen > 0.5, rank, -1.0)
        posc_ref[...] = posc
        posr_ref[...] = posc.T
        for ee in range(n_exp):
            cnt = jnp.sum(chosen[:, ee:ee + 1]).astype(jnp.int32)
            nblk_ref[ee] = (cnt + rb - 1) // rb

    nblk = nblk_ref[e]

    @pl.when(f == 0)
    def _():
        pos_row = posr_ref[pl.ds(e, 1), :]
        h = h_ref[...]

        def gather(b, carry):
            r0 = pl.multiple_of(b * rb, rb)
            rows = (lax.broadcasted_iota(jnp.int32, (rb, 1), 0) + r0).astype(F32)
            onehot = jnp.where(pos_row == rows, 1.0, 0.0).astype(BF16)
            hx_ref[pl.ds(r0, rb), :] = jnp.dot(onehot, h, preferred_element_type=F32).astype(BF16)
            y_ref[pl.ds(r0, rb), :] = jnp.zeros((rb, y_ref.shape[1]), F32)
            return carry

        lax.fori_loop(0, nblk, gather, 0)

    def expert(b, carry):
        r0 = pl.multiple_of(b * rb, rb)
        y_ref[pl.ds(r0, rb), :] += _swiglu_tile(hx_ref[pl.ds(r0, rb), :], wg_ref[0], wu_ref[0], wd_ref[0])
        return carry

    lax.fori_loop(0, nblk, expert, 0)

    @pl.when(f == pl.num_programs(2) - 1)
    def _():
        pos_col = jnp.sum(jnp.where(lane == e, posc_ref[...], 0.0), axis=1, keepdims=True)
        gate = jnp.sum(jnp.where(lane == e, gates_ref[...], 0.0), axis=1, keepdims=True)

        @pl.when(nblk % 2 == 1)
        def _():
            y_ref[pl.ds(pl.multiple_of(nblk * rb, rb), rb), :] = jnp.zeros((rb, y_ref.shape[1]), F32)

        def scatter(b, carry):
            r0 = pl.multiple_of(b * 2 * rb, 2 * rb)
            cols = (lax.broadcasted_iota(jnp.int32, (1, 2 * rb), 1) + r0).astype(F32)
            onehot = jnp.where(pos_col == cols, 1.0, 0.0).astype(BF16)
            back = jnp.dot(onehot, y_ref[pl.ds(r0, 2 * rb), :].astype(BF16), preferred_element_type=F32)
            acc_ref[...] += gate * back
            return carry

        lax.fori_loop(0, (nblk + 1) // 2, scatter, 0)

    @pl.when((e == n_exp - 1) & (f == pl.num_programs(2) - 1))
    def _():
        out = x_ref[...] + acc_ref[...]
        if final_norm:
            out = _norm_rows(out, gf_ref[...])
        o_ref[...] = out


def _moe(x, gain, router, router_bias, w_gate, w_up, w_down, *, final_gain=None, tm=1024, tf=896, rb=128):
    m, d = x.shape
    n_exp, _, ff = w_gate.shape
    tm = min(tm, m)
    tf = tf if ff % tf == 0 else min(512, ff)
    assert (tm // rb) % 2 == 0
    r = jnp.pad(router.astype(F32), ((0, 0), (0, LANES - n_exp)))
    r_hi = r.astype(BF16)
    r = jnp.stack([r_hi, (r - r_hi.astype(F32)).astype(BF16)])
    rbias = jnp.pad(router_bias.astype(F32).reshape(1, n_exp), ((0, 0), (0, LANES - n_exp)))
    final_norm = final_gain is not None
    gf = (final_gain if final_norm else gain).astype(F32).reshape(1, d)
    return pl.pallas_call(
        functools.partial(_moe_kernel, n_exp=n_exp, final_norm=final_norm, rb=rb),
        grid=(m // tm, n_exp, ff // tf),
        in_specs=[pl.BlockSpec((tm, d), lambda i, e, f: (i, 0)),
                  pl.BlockSpec((1, d), lambda i, e, f: (0, 0)),
                  pl.BlockSpec((2, d, LANES), lambda i, e, f: (0, 0, 0)),
                  pl.BlockSpec((1, LANES), lambda i, e, f: (0, 0)),
                  pl.BlockSpec((1, d, tf), lambda i, e, f: (e, 0, f)),
                  pl.BlockSpec((1, d, tf), lambda i, e, f: (e, 0, f)),
                  pl.BlockSpec((1, tf, d), lambda i, e, f: (e, f, 0)),
                  pl.BlockSpec((1, d), lambda i, e, f: (0, 0))],
        out_specs=pl.BlockSpec((tm, d), lambda i, e, f: (i, 0)),
        out_shape=jax.ShapeDtypeStruct((m, d), F32),
        scratch_shapes=[pltpu.VMEM((tm, d), BF16), pltpu.VMEM((tm, d), F32), pltpu.VMEM((tm, LANES), F32),
                        pltpu.VMEM((tm, LANES), F32), pltpu.VMEM((LANES, tm), F32),
                        pltpu.VMEM((tm, d), BF16), pltpu.VMEM((tm, d), F32), pltpu.SMEM((n_exp,), jnp.int32)],
        compiler_params=_cparams(("parallel", "arbitrary", "arbitrary")),
        name="moe_mixer",
    )(x, gain.astype(F32).reshape(1, d), r, rbias, w_gate.astype(BF16), w_up.astype(BF16),
      w_down.astype(BF16), gf)


def _hyena_layer(x, gain, lp, ls, nb_p, nb_s, w_in, conv_w, f_w1, f_b1, f_w2, f_b2, f_w3, f_b3, f_w4,
                 f_freq, skip, w_out, *, starts, ends):
    m, d = x.shape
    assert nb_p == 1 and lp % (2 * DFT_N1) == 0 and ls % (2 * DFT_N1) == 0
    hp, hs = lp // DFT_N1, ls // DFT_N1
    ut = _proj(x, gain, w_in, conv_w, transpose_out=True, starts=starts, ends=ends)
    u3 = ut.reshape(3 * d, m // DFT_N1, DFT_N1)
    filt = (f_w1, f_b1, f_w2, f_b2, f_w3, f_b3, f_w4, f_freq)
    kp = _spectrum(_hyena_taps(lp, *filt, d))
    ks = _spectrum(_hyena_taps(ls, *filt, d))
    conv = functools.partial(_fftconv, d_model=d, hp=hp, hs=hs, nb=nb_s)
    z1 = conv(u3, 0, u3, d, skip[0], kp, 0, ks, 0)
    z2 = conv(z1, 0, u3, 2 * d, skip[1], kp, d, ks, d)
    return _outproj_t(z2.reshape(d, m), w_out, x)


def _gdn_layer(x, gain, w_in, conv_w, a_log, dt_bias, norm_w, w_out, *, starts, ends):
    key, val = GDN_NK * GDN_DH, GDN_NV * GDN_DH
    cd = 2 * key + val
    qkv = _proj(x, gain, w_in[:, :cd], conv_w, act="silu", starts=starts, ends=ends)
    z = _proj(x, gain, w_in[:, cd:cd + val])
    w_ba = jnp.pad(w_in[:, cd + val:], ((0, 0), (0, LANES - 4 * GDN_NV)))
    ba = _proj(x, gain, w_ba)
    o_f, o_b = _gdn_scan(qkv, ba, a_log, dt_bias, starts=starts, ends=ends)
    return _gdn_out(o_f, o_b, z, norm_w, w_out, x)


def kernel(x_prompt, x_sample, norm_mix, norm_ffn, norm_final, hy_w_in, hy_conv, hy_f_w1, hy_f_b1, hy_f_w2, hy_f_b2, hy_f_w3, hy_f_b3, hy_f_w4, hy_f_freq, hy_skip, hy_w_out, gdn_w_in, gdn_conv, gdn_a_log, gdn_dt_bias, gdn_norm, gdn_w_out, ffn_w_gate, ffn_w_up, ffn_w_down, moe_router, moe_router_bias, moe_w_gate, moe_w_up, moe_w_down):
    bp, lp, d = x_prompt.shape
    bs, ls, _ = x_sample.shape
    x = jnp.concatenate([x_prompt.reshape(bp * lp, d), x_sample.reshape(bs * ls, d)], axis=0)
    starts = tuple(b * lp for b in range(bp)) + tuple(bp * lp + b * ls for b in range(bs))
    ends = tuple(s + lp for s in starts[:bp]) + tuple(s + ls for s in starts[bp:])
    depth = norm_mix.shape[0]
    for i in range(depth):
        j = i // 2
        last = i == depth - 1
        if i % 2 == 0:
            x = _hyena_layer(x, norm_mix[i], lp, ls, bp, bs, hy_w_in[j], hy_conv[j], hy_f_w1[j], hy_f_b1[j],
                             hy_f_w2[j], hy_f_b2[j], hy_f_w3[j], hy_f_b3[j], hy_f_w4[j], hy_f_freq[j],
                             hy_skip[j], hy_w_out[j], starts=starts, ends=ends)
            x = _ffn(x, norm_ffn[i], ffn_w_gate[j], ffn_w_up[j], ffn_w_down[j],
                     final_gain=norm_final if last else None)
        else:
            x = _gdn_layer(x, norm_mix[i], gdn_w_in[j], gdn_conv[j], gdn_a_log[j], gdn_dt_bias[j],
                           gdn_norm[j], gdn_w_out[j], starts=starts, ends=ends)
            x = _moe(x, norm_ffn[i], moe_router[j], moe_router_bias[j], moe_w_gate[j], moe_w_up[j],
                     moe_w_down[j], final_gain=norm_final if last else None)
    y_prompt = x[:bp * lp].reshape(bp, lp, d)
    y_sample = x[bp * lp:].reshape(bs, ls, d)
    return (y_prompt, y_sample)
```

```python
import functools
import math

import numpy as np
import jax
import jax.numpy as jnp
from jax import lax
from jax.experimental import pallas as pl
from jax.experimental.pallas import tpu as pltpu

F32 = jnp.float32
BF16 = jnp.bfloat16

RMS_EPS = 1e-6
HY_BANDS = 16
HY_FAST_PCT = 0.3
HY_SLOW_PCT = 1.5
HY_TARGET = 1e-2
GDN_NK = 8
GDN_NV = 16
GDN_DH = 128
TOP_K = 2

LANES = 128
HALO = 16
DFT_N1 = 256
GDN_CHUNK = 64
GDN_TILE = 256
VMEM_LIMIT = 56 * 1024 * 1024

HIGHEST = lax.Precision.HIGHEST


def _cparams(sem):
    return pltpu.CompilerParams(dimension_semantics=sem, vmem_limit_bytes=VMEM_LIMIT)


def _in_set(v, values):
    r = v == values[0]
    for b in values[1:]:
        r = jnp.logical_or(r, v == b)
    return r


def _norm_rows(x, g):
    ms = jnp.mean(x * x, axis=-1, keepdims=True)
    return x * lax.rsqrt(ms + RMS_EPS) * g


def _proj_conv_kernel(xp_ref, x_ref, xn_ref, g_ref, w_ref, cw_ref, o_ref, h_ref, *,
                      tm, starts, ends, act, transpose_out):
    i = pl.program_id(0)

    @pl.when(pl.program_id(1) == 0)
    def _():
        g = g_ref[...]
        row0 = i * tm
        hp = jnp.where(_in_set(row0, starts), 0.0, _norm_rows(xp_ref[...], g))
        hn = jnp.where(_in_set(row0 + tm, ends), 0.0, _norm_rows(xn_ref[...], g))
        h_ref[0:HALO, :] = hp.astype(BF16)
        h_ref[HALO:HALO + tm, :] = _norm_rows(x_ref[...], g).astype(BF16)
        h_ref[HALO + tm:HALO + tm + HALO, :] = hn.astype(BF16)

    n = tm + 2 * HALO
    tn = w_ref.shape[1]
    sub = min(tn, 2 * LANES)

    def finish(p, c0):
        up = pltpu.roll(p, 1, axis=0)
        dn = pltpu.roll(p, n - 1, axis=0)
        cw = cw_ref[:, c0:c0 + sub]
        y = (cw[0:1, :] * up[HALO:HALO + tm, :] + cw[1:2, :] * p[HALO:HALO + tm, :]
             + cw[2:3, :] * dn[HALO:HALO + tm, :])
        if act == "silu":
            y = y * jax.nn.sigmoid(y)
        if transpose_out:
            o_ref[c0:c0 + sub, :] = y.T
        else:
            o_ref[:, c0:c0 + sub] = y

    h = h_ref[...]
    prev = None
    for c0 in range(0, tn, sub):
        p = jnp.dot(h, w_ref[:, c0:c0 + sub], preferred_element_type=F32)
        if prev is not None:
            finish(*prev)
        prev = (p, c0)
    finish(*prev)


def _proj_kernel(x_ref, g_ref, w_ref, o_ref, h_ref):
    @pl.when(pl.program_id(1) == 0)
    def _():
        h_ref[...] = _norm_rows(x_ref[...], g_ref[...]).astype(BF16)

    o_ref[...] = jnp.dot(h_ref[...], w_ref[...], preferred_element_type=F32)


def _proj(x, gain, w, conv_w=None, *, act=None, transpose_out=False, starts=(), ends=(),
          tm=1024, tn=512):
    m, d = x.shape
    n = w.shape[1]
    tm = min(tm, m)
    if conv_w is not None and n % (2 * tn) == 0:
        tn = 2 * tn
    tn = min(tn, n)
    grid = (m // tm, n // tn)
    gain = gain.reshape(1, d).astype(F32)
    w = w.astype(BF16)
    if conv_w is None:
        return pl.pallas_call(
            _proj_kernel,
            grid=grid,
            in_specs=[pl.BlockSpec((tm, d), lambda i, j: (i, 0)),
                      pl.BlockSpec((1, d), lambda i, j: (0, 0)),
                      pl.BlockSpec((d, tn), lambda i, j: (0, j))],
            out_specs=pl.BlockSpec((tm, tn), lambda i, j: (i, j)),
            out_shape=jax.ShapeDtypeStruct((m, n), F32),
            scratch_shapes=[pltpu.VMEM((tm, d), BF16)],
            compiler_params=_cparams(("parallel", "arbitrary")),
            name="proj",
        )(x, gain, w)
    hb = tm // HALO
    last = m // HALO - 1
    kern = functools.partial(_proj_conv_kernel, tm=tm, starts=tuple(starts), ends=tuple(ends),
                             act=act, transpose_out=transpose_out)
    if transpose_out:
        out_spec = pl.BlockSpec((tn, tm), lambda i, j: (j, i))
        out_shape = jax.ShapeDtypeStruct((n, m), F32)
    else:
        out_spec = pl.BlockSpec((tm, tn), lambda i, j: (i, j))
        out_shape = jax.ShapeDtypeStruct((m, n), F32)
    return pl.pallas_call(
        kern,
        grid=grid,
        in_specs=[pl.BlockSpec((HALO, d), lambda i, j: (jnp.maximum(i * hb - 1, 0), 0)),
                  pl.BlockSpec((tm, d), lambda i, j: (i, 0)),
                  pl.BlockSpec((HALO, d), lambda i, j: (jnp.minimum((i + 1) * hb, last), 0)),
                  pl.BlockSpec((1, d), lambda i, j: (0, 0)),
                  pl.BlockSpec((d, tn), lambda i, j: (0, j)),
                  pl.BlockSpec((3, tn), lambda i, j: (0, j))],
        out_specs=out_spec,
        out_shape=out_shape,
        scratch_shapes=[pltpu.VMEM((tm + 2 * HALO, d), BF16)],
        compiler_params=_cparams(("parallel", "arbitrary")),
        name="proj_conv",
    )(x, x, x, gain, w, conv_w.astype(F32))


def _dft_rows(n2):
    return n2 // 2 + 8


@functools.lru_cache(maxsize=None)
def _dft_consts(n2):
    n1 = DFT_N1
    n = n1 * n2
    h = n2 // 2
    kept = h + 1
    hr = _dft_rows(n2)
    k1 = np.arange(n1)
    f1 = np.exp(-2j * np.pi * np.outer(k1, k1) / n1)
    f1_fwd = np.block([[f1.real, f1.imag], [-f1.imag, f1.real]])
    f1_inv = np.block([[f1.real, -f1.imag], [f1.imag, f1.real]])
    k2 = np.arange(n2)
    f2 = np.exp(-2j * np.pi * np.outer(k2, k2) / n2)
    pad_rows = lambda a: np.concatenate([a[:kept], np.zeros((hr - kept,) + a.shape[1:])], axis=0)
    tw = pad_rows(np.exp(-2j * np.pi * np.outer(k2, k1) / n))
    lf_full = np.concatenate([pad_rows(f2.real), pad_rows(f2.imag)], axis=0)
    lf_half = lf_full[:, :h]
    weight = np.where((k2 == 0) | (k2 == h), 1.0, 2.0)[None, :] / n
    li_re = pad_rows((f2.real[:h, :] * weight).T).T
    li_im = pad_rows((f2.imag[:h, :] * weight).T).T
    li_half = np.concatenate([li_re, li_im], axis=1)
    return dict(f1_fwd=f1_fwd, f1_inv=f1_inv, twre=tw.real, twim=tw.imag,
                lf_half=lf_half, lf_full=lf_full, li_half=li_half)


def _consts_dev(n2):
    c = _dft_consts(n2)
    return dict(
        f1_fwd=jnp.asarray(c["f1_fwd"], BF16), f1_inv=jnp.asarray(c["f1_inv"], BF16),
        twre=jnp.asarray(c["twre"], F32), twim=jnp.asarray(c["twim"], F32),
        lf_half=jnp.asarray(c["lf_half"], BF16), lf_full=jnp.asarray(c["lf_full"], BF16),
        li_half=jnp.asarray(c["li_half"], BF16))


def _dft_fwd(sig, lf, twre, twim, f1f, a_scr):
    ns, two_hr, _ = a_scr.shape
    hr = two_hr // 2
    for s in range(ns):
        a_scr[s] = jnp.dot(lf, sig(s).astype(BF16), preferred_element_type=F32)
    are = a_scr[:, 0:hr, :]
    aim = a_scr[:, hr:two_hr, :]
    bre = (are * twre - aim * twim).reshape(ns * hr, DFT_N1)
    bim = (are * twim + aim * twre).reshape(ns * hr, DFT_N1)
    bcat = jnp.concatenate([bre, bim], axis=1).astype(BF16)
    cc = jnp.dot(bcat, f1f, preferred_element_type=F32)
    return cc[:, 0:DFT_N1], cc[:, DFT_N1:2 * DFT_N1]


def _dft_inv_real(yre, yim, li, twre, twim, f1i, d_scr, emit):
    ns, two_hr, _ = d_scr.shape
    hr = two_hr // 2
    ycat = jnp.concatenate([yre, yim], axis=1).astype(BF16)
    bb = jnp.dot(ycat, f1i, preferred_element_type=F32)
    bre = bb[:, 0:DFT_N1].reshape(ns, hr, DFT_N1)
    bim = bb[:, DFT_N1:2 * DFT_N1].reshape(ns, hr, DFT_N1)
    d_scr[:, 0:hr, :] = bre * twre + bim * twim
    d_scr[:, hr:two_hr, :] = bim * twre - bre * twim
    for s in range(ns):
        emit(s, jnp.dot(li, d_scr[s].astype(BF16), preferred_element_type=F32))


def _taps_kernel(bands_ref, w1t_ref, w1c_ref, w1s_ref, b1_ref, w2_ref, b2_ref, w3_ref, b3_ref,
                 fq_ref, w4_ref, dl_ref, o_ref, h_ref, tm_ref, *, seq_len, tl):
    nt = pl.program_id(0)

    @pl.when(pl.program_id(1) == 0)
    def _():
        col = nt * tl + lax.broadcasted_iota(jnp.int32, (1, tl), 1)
        pos = jnp.where(col < seq_len, col, 2 * seq_len - col)
        posf = pos.astype(F32)
        t = posf * np.float32(1.0 / (seq_len - 1))
        ang = bands_ref[...] * (posf * np.float32(2.0 * math.pi / seq_len))
        fq = fq_ref[...]
        pre = (w1t_ref[...] * t
               + jnp.dot(w1c_ref[...], jnp.cos(ang), preferred_element_type=F32, precision=HIGHEST)
               - jnp.dot(w1s_ref[...], jnp.sin(ang), preferred_element_type=F32, precision=HIGHEST))
        h = jnp.sin(fq * (pre + b1_ref[...]))
        h = jnp.sin(fq * (jnp.dot(w2_ref[...], h, preferred_element_type=F32, precision=HIGHEST) + b2_ref[...]))
        h = jnp.sin(fq * (jnp.dot(w3_ref[...], h, preferred_element_type=F32, precision=HIGHEST) + b3_ref[...]))
        h_hi = h.astype(BF16)
        h_ref[0] = h_hi
        h_ref[1] = (h - h_hi.astype(F32)).astype(BF16)
        tm_ref[0:1, :] = t
        tm_ref[1:2, :] = jnp.where(col == seq_len, 0.0, 1.0)

    w_hi = w4_ref[0]
    h_hi = h_ref[0]
    taps = (jnp.dot(w_hi, h_hi, preferred_element_type=F32) + jnp.dot(w_hi, h_ref[1], preferred_element_type=F32)
            + jnp.dot(w4_ref[1], h_hi, preferred_element_type=F32))
    window = jnp.exp(-(dl_ref[...] * tm_ref[0:1, :]))
    o_ref[...] = taps * window * tm_ref[1:2, :]


def _hyena_taps(seq_len, f_w1, f_b1, f_w2, f_b2, f_w3, f_b3, f_w4, f_freq, d_model, *, tl=2048, td=512):
    fh = f_w1.shape[1]
    n_ord = f_w4.shape[1] // (2 * d_model)
    tl = min(tl, seq_len)
    ndt = d_model // td
    nlt = seq_len // tl
    col = lambda v: v.reshape(-1, 1).astype(F32)
    bands = jnp.linspace(1e-4, HY_BANDS - 1, HY_BANDS, dtype=F32).reshape(-1, 1)
    w1 = f_w1.astype(F32).T
    w4t = f_w4.astype(F32).reshape(fh, n_ord, 2, d_model).transpose(1, 2, 3, 0).reshape(n_ord * 2 * d_model, fh)
    w4_hi = w4t.astype(BF16)
    w4t = jnp.stack([w4_hi, (w4t - w4_hi.astype(F32)).astype(BF16)])
    max_decay = math.log(HY_TARGET) / HY_FAST_PCT
    min_decay = math.log(HY_TARGET) / HY_SLOW_PCT
    deltas = jnp.abs(jnp.linspace(min_decay, max_decay, d_model, dtype=F32))
    deltas = jnp.tile(deltas, n_ord).reshape(-1, 1)
    small = lambda a: pl.BlockSpec(a.shape, lambda n, j: (0, 0))
    args = [bands, w1[:, 0:1], w1[:, 1:1 + HY_BANDS], w1[:, 1 + HY_BANDS:1 + 2 * HY_BANDS], col(f_b1),
            f_w2.astype(F32).T, col(f_b2), f_w3.astype(F32).T, col(f_b3), col(f_freq)]
    return pl.pallas_call(
        functools.partial(_taps_kernel, seq_len=seq_len, tl=tl),
        grid=(2 * nlt, n_ord * ndt),
        in_specs=[small(a) for a in args] + [
            pl.BlockSpec((2, td, fh),
                         lambda n, j: (0, (j // ndt * 2 + (n >= nlt).astype(jnp.int32)) * ndt + j % ndt, 0)),
            pl.BlockSpec((td, 1), lambda n, j: (j, 0))],
        out_specs=pl.BlockSpec((td, tl), lambda n, j: (j, n)),
        out_shape=jax.ShapeDtypeStruct((n_ord * d_model, 2 * seq_len), F32),
        scratch_shapes=[pltpu.VMEM((2, fh, tl), BF16), pltpu.VMEM((8, tl), F32)],
        compiler_params=_cparams(("parallel", "arbitrary")),
        name="hyena_taps",
    )(*args, w4t, deltas)


def _spectrum_kernel(x_ref, lf_ref, twre_ref, twim_ref, f1f_ref, re_ref, im_ref, a_scr):
    ct, hr, _ = re_ref.shape
    cre, cim = _dft_fwd(lambda s: x_ref[s], lf_ref[...], twre_ref[...][None], twim_ref[...][None],
                        f1f_ref[...], a_scr)
    re_ref[...] = cre.reshape(ct, hr, DFT_N1)
    im_ref[...] = cim.reshape(ct, hr, DFT_N1)


def _spectrum(taps, *, ct=8):
    c, n = taps.shape
    n2 = n // DFT_N1
    hr = _dft_rows(n2)
    cs = _consts_dev(n2)
    x = taps.reshape(c, n2, DFT_N1)
    blk = pl.BlockSpec((ct, hr, DFT_N1), lambda i: (i, 0, 0))
    full = lambda a: pl.BlockSpec(a.shape, lambda i: (0,) * a.ndim)
    consts = [cs["lf_full"], cs["twre"], cs["twim"], cs["f1_fwd"]]
    return pl.pallas_call(
        _spectrum_kernel,
        grid=(c // ct,),
        in_specs=[pl.BlockSpec((ct, n2, DFT_N1), lambda i: (i, 0, 0))] + [full(a) for a in consts],
        out_specs=[blk, blk],
        out_shape=[jax.ShapeDtypeStruct((c, hr, DFT_N1), F32)] * 2,
        scratch_shapes=[pltpu.VMEM((ct, 2 * hr, DFT_N1), F32)],
        compiler_params=_cparams(("parallel",)),
        name="hyena_spectrum",
    )(x, *consts)


def _fftconv_kernel(x_ref, gate_ref, skip_ref,
                    kpre_ref, kpim_ref, lfp_ref, lip_ref, twpre_ref, twpim_ref,
                    ksre_ref, ksim_ref, lfs_ref, lis_ref, twsre_ref, twsim_ref,
                    f1f_ref, f1i_ref, o_ref, ap_scr, dp_scr, as_scr, ds_scr, *, hp, hs, nb):
    ct = x_ref.shape[0]
    f1f = f1f_ref[...]
    f1i = f1i_ref[...]

    def run(rows, n_sig, kre, kim, lf, li, twre, twim, a_scr, d_scr):
        def sig(s):
            c, r0, h = rows(s)
            return x_ref[c, r0:r0 + h, :]

        cre, cim = _dft_fwd(sig, lf, twre, twim, f1f, a_scr)
        yre = cre * kre - cim * kim
        yim = cre * kim + cim * kre

        def emit(s, y):
            c, r0, h = rows(s)
            xs = x_ref[c, r0:r0 + h, :]
            o_ref[c, r0:r0 + h, :] = gate_ref[c, r0:r0 + h, :] * (y + skip_ref[c] * xs)

        _dft_inv_real(yre, yim, li, twre, twim, f1i, d_scr, emit)

    hrp = kpre_ref.shape[1]
    run(lambda s: (s, 0, hp), ct,
        kpre_ref[...].reshape(ct * hrp, DFT_N1), kpim_ref[...].reshape(ct * hrp, DFT_N1),
        lfp_ref[...], lip_ref[...], twpre_ref[...][None], twpim_ref[...][None], ap_scr, dp_scr)
    hrs = ksre_ref.shape[1]
    rep = lambda k: jnp.broadcast_to(k[:, None], (ct, nb, hrs, DFT_N1)).reshape(ct * nb * hrs, DFT_N1)
    run(lambda s: (s // nb, hp + (s % nb) * hs, hs), ct * nb,
        rep(ksre_ref[...]), rep(ksim_ref[...]),
        lfs_ref[...], lis_ref[...], twsre_ref[...][None], twsim_ref[...][None], as_scr, ds_scr)


def _fftconv(x, x_off, gate, gate_off, skip, kp, kp_off, ks, ks_off, *, d_model, hp, hs, nb, ct=8):
    r = hp + nb * hs
    cp, cs_ = _consts_dev(2 * hp), _consts_dev(2 * hs)
    hrp, hrs = _dft_rows(2 * hp), _dft_rows(2 * hs)
    cb = lambda off: (lambda i: (off // ct + i, 0, 0))
    full = lambda a: pl.BlockSpec(a.shape, lambda i: (0,) * a.ndim)
    kspec = lambda rows, off: pl.BlockSpec((ct, rows, DFT_N1), cb(off))
    skip3 = jnp.broadcast_to(skip.astype(F32).reshape(d_model, 1, 1), (d_model, 1, DFT_N1))
    pc = [cp["lf_half"], cp["li_half"], cp["twre"], cp["twim"]]
    sc = [cs_["lf_half"], cs_["li_half"], cs_["twre"], cs_["twim"]]
    return pl.pallas_call(
        functools.partial(_fftconv_kernel, hp=hp, hs=hs, nb=nb),
        grid=(d_model // ct,),
        in_specs=[pl.BlockSpec((ct, r, DFT_N1), cb(x_off)), pl.BlockSpec((ct, r, DFT_N1), cb(gate_off)),
                  pl.BlockSpec((ct, 1, DFT_N1), cb(0)),
                  kspec(hrp, kp_off), kspec(hrp, kp_off)] + [full(a) for a in pc]
                 + [kspec(hrs, ks_off), kspec(hrs, ks_off)] + [full(a) for a in sc]
                 + [full(cp["f1_fwd"]), full(cp["f1_inv"])],
        out_specs=pl.BlockSpec((ct, r, DFT_N1), cb(0)),
        out_shape=jax.ShapeDtypeStruct((d_model, r, DFT_N1), F32),
        scratch_shapes=[pltpu.VMEM((ct, 2 * hrp, DFT_N1), F32), pltpu.VMEM((ct, 2 * hrp, DFT_N1), F32),
                        pltpu.VMEM((ct * nb, 2 * hrs, DFT_N1), F32), pltpu.VMEM((ct * nb, 2 * hrs, DFT_N1), F32)],
        compiler_params=_cparams(("parallel",)),
        name="hyena_fftconv",
    )(x, gate, skip3, kp[0], kp[1], *pc, ks[0], ks[1], *sc, cp["f1_fwd"], cp["f1_inv"])


def _outproj_t_kernel(zt_ref, w_ref, res_ref, o_ref):
    z = zt_ref[...].astype(BF16)
    y = lax.dot_general(z, w_ref[...], (((0,), (0,)), ((), ())), preferred_element_type=F32)
    o_ref[...] = res_ref[...] + y


def _outproj_t(zt, w, res, *, tm=512):
    k, m = zt.shape
    n = w.shape[1]
    tm = min(tm, m)
    return pl.pallas_call(
        _outproj_t_kernel,
        grid=(m // tm,),
        in_specs=[pl.BlockSpec((k, tm), lambda i: (0, i)),
                  pl.BlockSpec((k, n), lambda i: (0, 0)),
                  pl.BlockSpec((tm, n), lambda i: (i, 0))],
        out_specs=pl.BlockSpec((tm, n), lambda i: (i, 0)),
        out_shape=jax.ShapeDtypeStruct((m, n), F32),
        compiler_params=_cparams(("parallel",)),
        name="hyena_outproj",
    )(zt, w.astype(BF16), res)


def _softplus(x):
    return jnp.maximum(x, 0.0) + jnp.log1p(jnp.exp(-jnp.abs(x)))


def _gdn_kernel(qf_ref, kf_ref, vf_ref, baf_ref, batf_ref, qb_ref, kb_ref, vb_ref, bab_ref, batb_ref,
                alog_r_ref, dtb_r_ref, alog_c_ref, dtb_c_ref, of_ref, ob_ref, sf_ref, sb_ref, *,
                n_tiles, f_resets, b_resets, n_pairs):
    hp = pl.program_id(0)
    step = pl.program_id(1)
    t = GDN_TILE
    c = GDN_CHUNK
    nc = t // c
    dh = GDN_DH

    @pl.when(_in_set(step, f_resets))
    def _():
        sf_ref[...] = jnp.zeros_like(sf_ref)

    @pl.when(_in_set(n_tiles - 1 - step, b_resets))
    def _():
        sb_ref[...] = jnp.zeros_like(sb_ref)

    ri = lax.broadcasted_iota(jnp.int32, (t, t), 0)
    ci = lax.broadcasted_iota(jnp.int32, (t, t), 1)
    same = (ri // c) == (ci // c)
    rw = lax.broadcasted_iota(jnp.int32, (c, t), 0)
    lw = lax.broadcasted_iota(jnp.int32, (c, t), 1)
    lblk = lw // c
    lsub = lw % c
    rsub = lax.broadcasted_iota(jnp.int32, (t, LANES), 0) % c
    lane = lax.broadcasted_iota(jnp.int32, (t, LANES), 1)
    lrow = lax.broadcasted_iota(jnp.int32, (8, t), 1) % c

    def diag_to_wide(full):
        parts = [jnp.where(lblk == j, full[j * c:(j + 1) * c, :], 0.0) for j in range(nc)]
        return functools.reduce(lambda x, y: x + y, parts)

    def wide_to_diag16(wide):
        return jnp.where(same, jnp.concatenate([wide] * nc, axis=0), 0.0).astype(BF16)

    def l2n(x):
        return x * lax.rsqrt(jnp.sum(x * x, axis=-1, keepdims=True) + RMS_EPS)

    def mm(a, b):
        return jnp.dot(a, b, preferred_element_type=F32)

    dirs = [(0, qf_ref, kf_ref, vf_ref, baf_ref, batf_ref), (1, qb_ref, kb_ref, vb_ref, bab_ref, batb_ref)]

    def stage1_dir(d):
        _, _, _, v_ref, ba_ref, bat_ref = dirs[d]
        back = d == 1
        ba = ba_ref[...]
        g_all = -jnp.exp(alog_r_ref[...]) * _softplus(ba + dtb_r_ref[...])
        sh = 1
        while sh < c:
            if back:
                g_all = g_all + jnp.where(rsub < c - sh, pltpu.roll(g_all, t - sh, axis=0), 0.0)
            else:
                g_all = g_all + jnp.where(rsub >= sh, pltpu.roll(g_all, sh, axis=0), 0.0)
            sh *= 2
        return dict(back=back, incl=(lsub >= rw) if back else (lsub <= rw),
                    strict=(lsub > rw) if back else (lsub < rw),
                    beta_all=jax.nn.sigmoid(ba), g_all=g_all, bat_ref=bat_ref, v_ref=v_ref)

    def stage1_pair(d, pp):
        _, q_ref, k_ref, _, _, _ = dirs[d]
        q = l2n(q_ref[:, pp * dh:(pp + 1) * dh]) * np.float32(GDN_DH ** -0.5)
        k = l2n(k_ref[:, pp * dh:(pp + 1) * dh])
        k16 = k.astype(BF16)
        qk_kk = lax.dot_general(jnp.concatenate([q.astype(BF16), k16], axis=0), k16, (((1,), (1,)), ((), ())),
                                preferred_element_type=F32)
        return dict(q=q, k=k, qk=diag_to_wide(qk_kk[0:t]), kk=diag_to_wide(qk_kk[t:2 * t]))

    def stage2(d, pd, pr, ph):
        back = pd["back"]
        head = 2 * n_pairs * hp + ph
        jb = d * GDN_NV + head
        ja = 2 * GDN_NV + d * GDN_NV + head
        beta_c = jnp.sum(jnp.where(lane == jb, pd["beta_all"], 0.0), axis=1, keepdims=True)
        gc_c = jnp.sum(jnp.where(lane == ja, pd["g_all"], 0.0), axis=1, keepdims=True)
        a_row = pd["bat_ref"][pl.ds(ja, 1), :]
        g_row = -jnp.exp(alog_c_ref[pl.ds(ja, 1), :]) * _softplus(a_row + dtb_c_ref[pl.ds(ja, 1), :])
        g_row = jnp.broadcast_to(g_row, (8, t))
        sh = 1
        while sh < c:
            if back:
                g_row = g_row + jnp.where(lrow < c - sh, pltpu.roll(g_row, t - sh, axis=1), 0.0)
            else:
                g_row = g_row + jnp.where(lrow >= sh, pltpu.roll(g_row, sh, axis=1), 0.0)
            sh *= 2
        gc_r = g_row[0:1, :]
        incl = pd["incl"]
        decay = jnp.where(incl, jnp.exp(jnp.where(incl, diag_to_wide(gc_c) - gc_r, 0.0)), 0.0)
        p = jnp.where(pd["strict"], -(diag_to_wide(beta_c) * pr["kk"] * decay), 0.0)
        return dict(pd=pd, pr=pr, ph=ph, beta_c=beta_c, gc_c=gc_c, decay=decay, p=p)

    def stage3(group):
        for v in group:
            v["nn"] = v["p"]
            v["pm"] = mm(v["p"].astype(BF16), wide_to_diag16(v["p"]))
            yield
        m = 2
        while 2 * m < c:
            for v in group:
                both = mm(jnp.concatenate([v["pm"], v["nn"]], axis=0).astype(BF16), wide_to_diag16(v["pm"]))
                v["nn"] = v["nn"] + v["pm"] + both[c:2 * c]
                v["pm"] = both[0:c]
                yield
            m *= 2
        for v in group:
            v["nn"] = v["nn"] + v["pm"] + mm(v["nn"].astype(BF16), wide_to_diag16(v["pm"]))
            yield

    def stage4(group):
        for v in group:
            pd, pr, ph, beta_c, gc_c = v["pd"], v["pr"], v["ph"], v["beta_c"], v["gc_c"]
            vv = pd["v_ref"][:, ph * dh:(ph + 1) * dh]
            rhs = jnp.concatenate([vv * beta_c, pr["k"] * (beta_c * jnp.exp(gc_c))], axis=1)
            v["uw16"] = (rhs + mm(wide_to_diag16(v["nn"]), rhs.astype(BF16))).astype(BF16)
        for v in group:
            pd, pr, gc_c = v["pd"], v["pr"], v["gc_c"]
            qq = mm(wide_to_diag16(jnp.where(pd["incl"], pr["qk"] * v["decay"], 0.0)), v["uw16"])
            gc3 = gc_c.reshape(nc, c, 1)
            gl3 = gc3[:, 0:1, :] if pd["back"] else gc3[:, c - 1:c, :]
            kg16 = (pr["k"] * jnp.exp(jnp.broadcast_to(gl3, (nc, c, 1)).reshape(t, 1) - gc_c)).astype(BF16)
            v["qp16"] = (pr["q"] * jnp.exp(gc_c) - qq[:, dh:2 * dh]).astype(BF16)
            v["qq"] = qq
            v["gl3"] = gl3
            v["rp"] = [lax.dot_general(kg16[n * c:(n + 1) * c], v["uw16"][n * c:(n + 1) * c],
                                       (((0,), (0,)), ((), ())), preferred_element_type=F32)
                       for n in range(nc)]

    var = {}
    groups = [[], []]

    def setup(d):
        pd = stage1_dir(d)
        yield
        for pp in range(n_pairs):
            pr = stage1_pair(d, pp)
            yield
            for ph in (2 * pp, 2 * pp + 1):
                var[(d, ph)] = stage2(d, pd, pr, ph)
                groups[d].append(var[(d, ph)])
                yield

    def run(main, filler=None, every=1):
        for i, _ in enumerate(main):
            if filler is not None and i % every == every - 1:
                next(filler, None)
        if filler is not None:
            for _ in filler:
                pass

    n_setup = 1 + 3 * n_pairs
    n_doubling = 2 * n_pairs * (c.bit_length() - 1)
    run(setup(0))
    run(stage3(groups[0]), setup(1), every=max(1, n_doubling // n_setup))
    run(stage3(groups[1]))
    stage4(groups[0])
    stage4(groups[1])

    zero = jnp.zeros((dh, dh), F32)
    s_refs = [sf_ref, sb_ref]
    outs = [of_ref, ob_ref]
    state = [[s_refs[d][ph] for ph in range(2 * n_pairs)] for d in range(2)]
    for i in range(nc):
        for d in range(2):
            n = nc - 1 - i if d == 1 else i
            sl = slice(n * c, (n + 1) * c)
            for pp in range(n_pairs):
                v0, v1 = var[(d, 2 * pp)], var[(d, 2 * pp + 1)]
                s0, s1 = state[d][2 * pp], state[d][2 * pp + 1]
                s_d = jnp.concatenate([jnp.concatenate([s0, zero], axis=1),
                                       jnp.concatenate([zero, s1], axis=1)], axis=0).astype(BF16)
                lhs = jnp.concatenate([
                    jnp.concatenate([v0["rp"][n][:, dh:2 * dh], v1["rp"][n][:, dh:2 * dh]], axis=1).astype(BF16),
                    jnp.concatenate([v0["qp16"][sl], v1["qp16"][sl]], axis=1)], axis=0)
                z = mm(lhs, s_d)
                c0 = 2 * pp * dh
                outs[d][sl, c0:c0 + dh] = z[dh:dh + c, 0:dh] + v0["qq"][sl, 0:dh]
                outs[d][sl, c0 + dh:c0 + 2 * dh] = z[dh:dh + c, dh:2 * dh] + v1["qq"][sl, 0:dh]
                state[d][2 * pp] = jnp.exp(v0["gl3"][n]) * s0 - z[0:dh, 0:dh] + v0["rp"][n][:, 0:dh]
                state[d][2 * pp + 1] = jnp.exp(v1["gl3"][n]) * s1 - z[0:dh, dh:2 * dh] + v1["rp"][n][:, 0:dh]
    for d in range(2):
        for ph in range(2 * n_pairs):
            s_refs[d][ph] = state[d][ph]


def _gdn_scan(qkv, ba, a_log, dt_bias, *, starts, ends, n_pairs=4):
    m = qkv.shape[0]
    t = GDN_TILE
    n_tiles = m // t
    bat = ba.T
    pad = lambda a: jnp.pad(a.astype(F32).reshape(-1), (2 * GDN_NV, LANES - 4 * GDN_NV))
    alog_r = pad(a_log).reshape(1, LANES)
    dtb_r = pad(dt_bias).reshape(1, LANES)
    alog_c = pad(a_log).reshape(LANES, 1)
    dtb_c = pad(dt_bias).reshape(LANES, 1)
    f_resets = tuple(s // t for s in starts)
    b_resets = tuple(e // t - 1 for e in ends)
    fwd = lambda h, s: s
    bwd = lambda h, s: n_tiles - 1 - s

    kw = n_pairs * GDN_DH
    key_blocks = GDN_NK // n_pairs

    def specs(tile):
        return [pl.BlockSpec((t, kw), lambda h, s: (tile(h, s), h)),
                pl.BlockSpec((t, kw), lambda h, s: (tile(h, s), key_blocks + h)),
                pl.BlockSpec((t, 2 * kw), lambda h, s: (tile(h, s), key_blocks + h)),
                pl.BlockSpec((t, LANES), lambda h, s: (tile(h, s), 0)),
                pl.BlockSpec((LANES, t), lambda h, s: (0, tile(h, s)))]

    small = lambda a: pl.BlockSpec(a.shape, lambda h, s: (0, 0))
    out = jax.ShapeDtypeStruct((m, GDN_NV * GDN_DH), F32)
    return pl.pallas_call(
        functools.partial(_gdn_kernel, n_tiles=n_tiles, f_resets=f_resets, b_resets=b_resets, n_pairs=n_pairs),
        grid=(key_blocks, n_tiles),
        in_specs=specs(fwd) + specs(bwd) + [small(alog_r), small(dtb_r), small(alog_c), small(dtb_c)],
        out_specs=[pl.BlockSpec((t, 2 * kw), lambda h, s: (s, h)),
                   pl.BlockSpec((t, 2 * kw), lambda h, s: (n_tiles - 1 - s, h))],
        out_shape=[out, out],
        scratch_shapes=[pltpu.VMEM((2 * n_pairs, GDN_DH, GDN_DH), F32),
                        pltpu.VMEM((2 * n_pairs, GDN_DH, GDN_DH), F32)],
        compiler_params=_cparams(("parallel", "arbitrary")),
        name="gdn_scan",
    )(qkv, qkv, qkv, ba, bat, qkv, qkv, qkv, ba, bat, alog_r, dtb_r, alog_c, dtb_c)


def _gdn_out_kernel(of_ref, ob_ref, z_ref, nw_ref, w_ref, res_ref, o_ref, y_ref):
    nw = nw_ref[...]
    for h in range(GDN_NV):
        sl = slice(h * GDN_DH, (h + 1) * GDN_DH)
        o = of_ref[:, sl] + ob_ref[:, sl]
        o = o * lax.rsqrt(jnp.mean(o * o, axis=-1, keepdims=True) + RMS_EPS) * nw
        z = z_ref[:, sl]
        y_ref[:, sl] = (o * (z * jax.nn.sigmoid(z))).astype(BF16)
    o_ref[...] = res_ref[...] + jnp.dot(y_ref[...], w_ref[...], preferred_element_type=F32)


def _gdn_out(o_f, o_b, z, norm_w, w, res, *, tm=512):
    m, kv = o_f.shape
    n = w.shape[1]
    tm = min(tm, m)
    row = lambda width: pl.BlockSpec((tm, width), lambda i: (i, 0))
    return pl.pallas_call(
        _gdn_out_kernel,
        grid=(m // tm,),
        in_specs=[row(kv), row(kv), row(kv),
                  pl.BlockSpec((1, GDN_DH), lambda i: (0, 0)),
                  pl.BlockSpec((kv, n), lambda i: (0, 0)),
                  row(n)],
        out_specs=row(n),
        out_shape=jax.ShapeDtypeStruct((m, n), F32),
        scratch_shapes=[pltpu.VMEM((tm, kv), BF16)],
        compiler_params=_cparams(("parallel",)),
        name="gdn_out",
    )(o_f, o_b, z, norm_w.astype(F32).reshape(1, GDN_DH), w.astype(BF16), res)


def _swiglu_tile(h, wg_ref, wu_ref, wd_ref, lead=()):
    tf = wg_ref.shape[-1]
    sub = 2 * LANES if tf % (2 * LANES) == 0 else tf
    y = None
    prev = None

    def finish(a, u, c0):
        act = (a * jax.nn.sigmoid(a) * u).astype(BF16)
        return jnp.dot(act, wd_ref[lead + (slice(c0, c0 + sub), slice(None))], preferred_element_type=F32)

    for c0 in range(0, tf, sub):
        cols = lead + (slice(None), slice(c0, c0 + sub))
        a = jnp.dot(h, wg_ref[cols], preferred_element_type=F32)
        u = jnp.dot(h, wu_ref[cols], preferred_element_type=F32)
        if prev is not None:
            part = finish(*prev)
            y = part if y is None else y + part
        prev = (a, u, c0)
    part = finish(*prev)
    return part if y is None else y + part


def _ffn_kernel(x_ref, g_ref, wg_ref, wu_ref, wd_ref, gf_ref, o_ref, h_ref, acc_ref, *, final_norm):
    f = pl.program_id(1)

    @pl.when(f == 0)
    def _():
        h_ref[...] = _norm_rows(x_ref[...], g_ref[...]).astype(BF16)
        acc_ref[...] = jnp.zeros_like(acc_ref)

    acc_ref[...] += _swiglu_tile(h_ref[...], wg_ref, wu_ref, wd_ref)

    @pl.when(f == pl.num_programs(1) - 1)
    def _():
        out = x_ref[...] + acc_ref[...]
        if final_norm:
            out = _norm_rows(out, gf_ref[...])
        o_ref[...] = out


def _ffn(x, gain, w_gate, w_up, w_down, *, final_gain=None, tm=1024, tf=512):
    m, d = x.shape
    ff = w_gate.shape[1]
    tm = min(tm, m)
    tf = min(tf, ff)
    final_norm = final_gain is not None
    gf = (final_gain if final_norm else gain).astype(F32).reshape(1, d)
    return pl.pallas_call(
        functools.partial(_ffn_kernel, final_norm=final_norm),
        grid=(m // tm, ff // tf),
        in_specs=[pl.BlockSpec((tm, d), lambda i, f: (i, 0)),
                  pl.BlockSpec((1, d), lambda i, f: (0, 0)),
                  pl.BlockSpec((d, tf), lambda i, f: (0, f)),
                  pl.BlockSpec((d, tf), lambda i, f: (0, f)),
                  pl.BlockSpec((tf, d), lambda i, f: (f, 0)),
                  pl.BlockSpec((1, d), lambda i, f: (0, 0))],
        out_specs=pl.BlockSpec((tm, d), lambda i, f: (i, 0)),
        out_shape=jax.ShapeDtypeStruct((m, d), F32),
        scratch_shapes=[pltpu.VMEM((tm, d), BF16), pltpu.VMEM((tm, d), F32)],
        compiler_params=_cparams(("parallel", "arbitrary")),
        name="ffn_mixer",
    )(x, gain.astype(F32).reshape(1, d), w_gate.astype(BF16), w_up.astype(BF16), w_down.astype(BF16), gf)


def _moe_kernel(x_ref, g_ref, r_ref, rb_ref, wg_ref, wu_ref, wd_ref, gf_ref, o_ref,
                h_ref, acc_ref, gates_ref, posc_ref, posr_ref, hx_ref, y_ref, nblk_ref, *,
                n_exp, final_norm, rb):
    e = pl.program_id(1)
    f = pl.program_id(2)
    tm = x_ref.shape[0]
    lane = lax.broadcasted_iota(jnp.int32, (tm, LANES), 1)

    @pl.when((e == 0) & (f == 0))
    def _():
        h = _norm_rows(x_ref[...], g_ref[...])
        h_ref[...] = h.astype(BF16)
        acc_ref[...] = jnp.zeros_like(acc_ref)
        h_hi = h_ref[...]
        h_lo = (h - h_hi.astype(F32)).astype(BF16)
        r_hi = r_ref[0]
        logits = (jnp.dot(h_hi, r_hi, preferred_element_type=F32)
                  + jnp.dot(h_hi, r_ref[1], preferred_element_type=F32)
                  + jnp.dot(h_lo, r_hi, preferred_element_type=F32)) + rb_ref[...]
        logits = jnp.where(lane < n_exp, logits, -jnp.inf)
        m1 = jnp.max(logits, axis=1, keepdims=True)
        i1 = jnp.min(jnp.where(logits == m1, lane, LANES), axis=1, keepdims=True)
        rest = jnp.where(lane == i1, -jnp.inf, logits)
        m2 = jnp.max(rest, axis=1, keepdims=True)
        i2 = jnp.min(jnp.where(rest == m2, lane, LANES), axis=1, keepdims=True)
        e2 = jnp.exp(m2 - m1)
        w1 = 1.0 / (1.0 + e2)
        gates_ref[...] = jnp.where(lane == i1, w1, 0.0) + jnp.where(lane == i2, e2 * w1, 0.0)
        chosen = jnp.where((lane == i1) | (lane == i2), 1.0, 0.0)
        ri = lax.broadcasted_iota(jnp.int32, (tm, tm), 0)
        ci = lax.broadcasted_iota(jnp.int32, (tm, tm), 1)
        before = jnp.where(ci < ri, 1.0, 0.0).astype(BF16)
        rank = jnp.dot(before, chosen.astype(BF16), preferred_element_type=F32)
        posc = jnp.where(chosen > 0.5, rank, -1.0)
        posc_ref[...] = posc
        posr_ref[...] = posc.T
        for ee in range(n_exp):
            cnt = jnp.sum(chosen[:, ee:ee + 1]).astype(jnp.int32)
            nblk_ref[ee] = (cnt + rb - 1) // rb

    nblk = nblk_ref[e]

    @pl.when(f == 0)
    def _():
        pos_row = posr_ref[pl.ds(e, 1), :]
        h = h_ref[...]

        def gather(b, carry):
            r0 = pl.multiple_of(b * 2 * rb, 2 * rb)
            rows = (lax.broadcasted_iota(jnp.int32, (2 * rb, 1), 0) + r0).astype(F32)
            onehot = jnp.where(pos_row == rows, 1.0, 0.0).astype(BF16)
            hx_ref[pl.ds(r0, 2 * rb), :] = jnp.dot(onehot, h, preferred_element_type=F32).astype(BF16)
            y_ref[pl.ds(r0, 2 * rb), :] = jnp.zeros((2 * rb, y_ref.shape[1]), F32)
            return carry

        lax.fori_loop(0, (nblk + 1) // 2, gather, 0)

    def expert(r0, rows):
        y_ref[pl.ds(r0, rows), :] += _swiglu_tile(hx_ref[pl.ds(r0, rows), :], wg_ref, wu_ref, wd_ref, lead=(0,))

    def expert_pair(b, carry):
        expert(pl.multiple_of(b * 2 * rb, 2 * rb), 2 * rb)
        return carry

    lax.fori_loop(0, nblk // 2, expert_pair, 0)

    @pl.when(nblk % 2 == 1)
    def _():
        expert(pl.multiple_of((nblk - 1) * rb, rb), rb)

    @pl.when(f == pl.num_programs(2) - 1)
    def _():
        pos_col = jnp.sum(jnp.where(lane == e, posc_ref[...], 0.0), axis=1, keepdims=True)
        gate = jnp.sum(jnp.where(lane == e, gates_ref[...], 0.0), axis=1, keepdims=True)

        def scatter(b, carry):
            r0 = pl.multiple_of(b * 2 * rb, 2 * rb)
            cols = (lax.broadcasted_iota(jnp.int32, (1, 2 * rb), 1) + r0).astype(F32)
            onehot = jnp.where(pos_col == cols, 1.0, 0.0).astype(BF16)
            back = jnp.dot(onehot, y_ref[pl.ds(r0, 2 * rb), :].astype(BF16), preferred_element_type=F32)
            acc_ref[...] += gate * back
            return carry

        lax.fori_loop(0, (nblk + 1) // 2, scatter, 0)

    @pl.when((e == n_exp - 1) & (f == pl.num_programs(2) - 1))
    def _():
        out = x_ref[...] + acc_ref[...]
        if final_norm:
            out = _norm_rows(out, gf_ref[...])
        o_ref[...] = out


def _moe(x, gain, router, router_bias, w_gate, w_up, w_down, *, final_gain=None, tm=1024, tf=1792, rb=128):
    m, d = x.shape
    n_exp, _, ff = w_gate.shape
    tm = min(tm, m)
    tf = tf if ff % tf == 0 else min(512, ff)
    assert (tm // rb) % 2 == 0
    r = jnp.pad(router.astype(F32), ((0, 0), (0, LANES - n_exp)))
    r_hi = r.astype(BF16)
    r = jnp.stack([r_hi, (r - r_hi.astype(F32)).astype(BF16)])
    rbias = jnp.pad(router_bias.astype(F32).reshape(1, n_exp), ((0, 0), (0, LANES - n_exp)))
    final_norm = final_gain is not None
    gf = (final_gain if final_norm else gain).astype(F32).reshape(1, d)
    return pl.pallas_call(
        functools.partial(_moe_kernel, n_exp=n_exp, final_norm=final_norm, rb=rb),
        grid=(m // tm, n_exp, ff // tf),
        in_specs=[pl.BlockSpec((tm, d), lambda i, e, f: (i, 0), pipeline_mode=pl.Buffered(1)),
                  pl.BlockSpec((1, d), lambda i, e, f: (0, 0)),
                  pl.BlockSpec((2, d, LANES), lambda i, e, f: (0, 0, 0)),
                  pl.BlockSpec((1, LANES), lambda i, e, f: (0, 0)),
                  pl.BlockSpec((1, d, tf), lambda i, e, f: (e, 0, f)),
                  pl.BlockSpec((1, d, tf), lambda i, e, f: (e, 0, f)),
                  pl.BlockSpec((1, tf, d), lambda i, e, f: (e, f, 0)),
                  pl.BlockSpec((1, d), lambda i, e, f: (0, 0))],
        out_specs=pl.BlockSpec((tm, d), lambda i, e, f: (i, 0)),
        out_shape=jax.ShapeDtypeStruct((m, d), F32),
        scratch_shapes=[pltpu.VMEM((tm, d), BF16), pltpu.VMEM((tm, d), F32), pltpu.VMEM((tm, LANES), F32),
                        pltpu.VMEM((tm, LANES), F32), pltpu.VMEM((LANES, tm), F32),
                        pltpu.VMEM((tm, d), BF16), pltpu.VMEM((tm, d), F32), pltpu.SMEM((n_exp,), jnp.int32)],
        compiler_params=_cparams(("parallel", "arbitrary", "arbitrary")),
        name="moe_mixer",
    )(x, gain.astype(F32).reshape(1, d), r, rbias, w_gate.astype(BF16), w_up.astype(BF16),
      w_down.astype(BF16), gf)


def _hyena_layer(x, gain, lp, ls, nb_p, nb_s, w_in, conv_w, f_w1, f_b1, f_w2, f_b2, f_w3, f_b3, f_w4,
                 f_freq, skip, w_out, *, starts, ends):
    m, d = x.shape
    assert nb_p == 1 and lp % (2 * DFT_N1) == 0 and ls % (2 * DFT_N1) == 0
    hp, hs = lp // DFT_N1, ls // DFT_N1
    ut = _proj(x, gain, w_in, conv_w, transpose_out=True, starts=starts, ends=ends)
    u3 = ut.reshape(3 * d, m // DFT_N1, DFT_N1)
    filt = (f_w1, f_b1, f_w2, f_b2, f_w3, f_b3, f_w4, f_freq)
    kp = _spectrum(_hyena_taps(lp, *filt, d))
    ks = _spectrum(_hyena_taps(ls, *filt, d))
    conv = functools.partial(_fftconv, d_model=d, hp=hp, hs=hs, nb=nb_s)
    z1 = conv(u3, 0, u3, d, skip[0], kp, 0, ks, 0)
    z2 = conv(z1, 0, u3, 2 * d, skip[1], kp, d, ks, d)
    return _outproj_t(z2.reshape(d, m), w_out, x)


def _gdn_layer(x, gain, w_in, conv_w, a_log, dt_bias, norm_w, w_out, *, starts, ends):
    key, val = GDN_NK * GDN_DH, GDN_NV * GDN_DH
    cd = 2 * key + val
    qkv = _proj(x, gain, w_in[:, :cd], conv_w, act="silu", starts=starts, ends=ends)
    z = _proj(x, gain, w_in[:, cd:cd + val])
    w_ba = jnp.pad(w_in[:, cd + val:], ((0, 0), (0, LANES - 4 * GDN_NV)))
    ba = _proj(x, gain, w_ba)
    o_f, o_b = _gdn_scan(qkv, ba, a_log, dt_bias, starts=starts, ends=ends)
    return _gdn_out(o_f, o_b, z, norm_w, w_out, x)


def kernel(x_prompt, x_sample, norm_mix, norm_ffn, norm_final, hy_w_in, hy_conv, hy_f_w1, hy_f_b1, hy_f_w2, hy_f_b2, hy_f_w3, hy_f_b3, hy_f_w4, hy_f_freq, hy_skip, hy_w_out, gdn_w_in, gdn_conv, gdn_a_log, gdn_dt_bias, gdn_norm, gdn_w_out, ffn_w_gate, ffn_w_up, ffn_w_down, moe_router, moe_router_bias, moe_w_gate, moe_w_up, moe_w_down):
    bp, lp, d = x_prompt.shape
    bs, ls, _ = x_sample.shape
    x = jnp.concatenate([x_prompt.reshape(bp * lp, d), x_sample.reshape(bs * ls, d)], axis=0)
    starts = tuple(b * lp for b in range(bp)) + tuple(bp * lp + b * ls for b in range(bs))
    ends = tuple(s + lp for s in starts[:bp]) + tuple(s + ls for s in starts[bp:])
    depth = norm_mix.shape[0]
    for i in range(depth):
        j = i // 2
        last = i == depth - 1
        if i % 2 == 0:
            x = _hyena_layer(x, norm_mix[i], lp, ls, bp, bs, hy_w_in[j], hy_conv[j], hy_f_w1[j], hy_f_b1[j],
                             hy_f_w2[j], hy_f_b2[j], hy_f_w3[j], hy_f_b3[j], hy_f_w4[j], hy_f_freq[j],
                             hy_skip[j], hy_w_out[j], starts=starts, ends=ends)
            x = _ffn(x, norm_ffn[i], ffn_w_gate[j], ffn_w_up[j], ffn_w_down[j],
                     final_gain=norm_final if last else None)
        else:
            x = _gdn_layer(x, norm_mix[i], gdn_w_in[j], gdn_conv[j], gdn_a_log[j], gdn_dt_bias[j],
                           gdn_norm[j], gdn_w_out[j], starts=starts, ends=ends)
            x = _moe(x, norm_ffn[i], moe_router[j], moe_router_bias[j], moe_w_gate[j], moe_w_up[j],
                     moe_w_down[j], final_gain=norm_final if last else None)
    y_prompt = x[:bp * lp].reshape(bp, lp, d)
    y_sample = x[bp * lp:].reshape(bs, ls, d)
    return (y_prompt, y_sample)
```

```python
import functools
import math

import numpy as np
import jax
import jax.numpy as jnp
from jax import lax
from jax.experimental import pallas as pl
from jax.experimental.pallas import tpu as pltpu

F32 = jnp.float32
BF16 = jnp.bfloat16

RMS_EPS = 1e-6
HY_BANDS = 16
HY_FAST_PCT = 0.3
HY_SLOW_PCT = 1.5
HY_TARGET = 1e-2
GDN_NK = 8
GDN_NV = 16
GDN_DH = 128
TOP_K = 2

LANES = 128
HALO = 16
DFT_N1 = 256
GDN_CHUNK = 64
GDN_TILE = 256
VMEM_LIMIT = 56 * 1024 * 1024

HIGHEST = lax.Precision.HIGHEST


def _cparams(sem):
    return pltpu.CompilerParams(dimension_semantics=sem, vmem_limit_bytes=VMEM_LIMIT)


def _in_set(v, values):
    r = v == values[0]
    for b in values[1:]:
        r = jnp.logical_or(r, v == b)
    return r


def _norm_rows(x, g):
    ms = jnp.mean(x * x, axis=-1, keepdims=True)
    return x * lax.rsqrt(ms + RMS_EPS) * g


def _proj_conv_kernel(*refs, tm, part_tiles, starts, ends, act, transpose_out):
    n_parts = len(part_tiles)
    g_ref, w_ref, cw_ref, o_ref, h_ref = refs[3 * n_parts:]
    i = pl.program_id(0)
    first = 0
    for p, tiles in enumerate(part_tiles):
        xp_ref, x_ref, xn_ref = refs[3 * p:3 * p + 3]

        @pl.when((pl.program_id(1) == 0) & (i >= first) & (i < first + tiles))
        def _(xp_ref=xp_ref, x_ref=x_ref, xn_ref=xn_ref):
            g = g_ref[...]
            row0 = i * tm
            hp = jnp.where(_in_set(row0, starts), 0.0, _norm_rows(xp_ref[...], g))
            hn = jnp.where(_in_set(row0 + tm, ends), 0.0, _norm_rows(xn_ref[...], g))
            h_ref[0:HALO, :] = hp.astype(BF16)
            h_ref[HALO:HALO + tm, :] = _norm_rows(x_ref[...], g).astype(BF16)
            h_ref[HALO + tm:HALO + tm + HALO, :] = hn.astype(BF16)

        first += tiles

    n = tm + 2 * HALO
    tn = w_ref.shape[1]
    sub = min(tn, 2 * LANES)

    def finish(p, c0):
        up = pltpu.roll(p, 1, axis=0)
        dn = pltpu.roll(p, n - 1, axis=0)
        cw = cw_ref[:, c0:c0 + sub]
        y = (cw[0:1, :] * up[HALO:HALO + tm, :] + cw[1:2, :] * p[HALO:HALO + tm, :]
             + cw[2:3, :] * dn[HALO:HALO + tm, :])
        if act == "silu":
            y = y * jax.nn.sigmoid(y)
        if transpose_out:
            o_ref[c0:c0 + sub, :] = y.T
        else:
            o_ref[:, c0:c0 + sub] = y

    h = h_ref[...]
    prev = None
    for c0 in range(0, tn, sub):
        p = jnp.dot(h, w_ref[:, c0:c0 + sub], preferred_element_type=F32)
        if prev is not None:
            finish(*prev)
        prev = (p, c0)
    finish(*prev)


def _proj_kernel(x_ref, g_ref, w_ref, o_ref, h_ref):
    @pl.when(pl.program_id(1) == 0)
    def _():
        h_ref[...] = _norm_rows(x_ref[...], g_ref[...]).astype(BF16)

    o_ref[...] = jnp.dot(h_ref[...], w_ref[...], preferred_element_type=F32)


def _proj(x, gain, w, conv_w=None, *, act=None, transpose_out=False, starts=(), ends=(),
          tm=1024, tn=512):
    parts = x if isinstance(x, tuple) else (x,)
    m = sum(p.shape[0] for p in parts)
    d = parts[0].shape[1]
    n = w.shape[1]
    tm = min(tm, m)
    if conv_w is not None and n % (2 * tn) == 0:
        tn = 2 * tn
    tn = min(tn, n)
    grid = (m // tm, n // tn)
    gain = gain.reshape(1, d).astype(F32)
    w = w.astype(BF16)
    if conv_w is None:
        assert len(parts) == 1
        return pl.pallas_call(
            _proj_kernel,
            grid=grid,
            in_specs=[pl.BlockSpec((tm, d), lambda i, j: (i, 0)),
                      pl.BlockSpec((1, d), lambda i, j: (0, 0)),
                      pl.BlockSpec((d, tn), lambda i, j: (0, j))],
            out_specs=pl.BlockSpec((tm, tn), lambda i, j: (i, j)),
            out_shape=jax.ShapeDtypeStruct((m, n), F32),
            scratch_shapes=[pltpu.VMEM((tm, d), BF16)],
            compiler_params=_cparams(("parallel", "arbitrary")),
            name="proj",
        )(parts[0], gain, w)
    hb = tm // HALO
    part_tiles = tuple(p.shape[0] // tm for p in parts)
    assert all(p.shape[0] % tm == 0 for p in parts)
    assert all(sum(part_tiles[:k + 1]) * tm in ends for k in range(len(parts)))
    kern = functools.partial(_proj_conv_kernel, tm=tm, part_tiles=part_tiles, starts=tuple(starts),
                             ends=tuple(ends), act=act, transpose_out=transpose_out)
    x_specs, x_args, first = [], [], 0
    for p, tiles in zip(parts, part_tiles):
        last = p.shape[0] // HALO - 1
        loc = lambda i, first=first, tiles=tiles: jnp.clip(i - first, 0, tiles - 1)
        x_specs += [pl.BlockSpec((HALO, d), lambda i, j, loc=loc: (jnp.maximum(loc(i) * hb - 1, 0), 0)),
                    pl.BlockSpec((tm, d), lambda i, j, loc=loc: (loc(i), 0)),
                    pl.BlockSpec((HALO, d), lambda i, j, loc=loc, last=last: (jnp.minimum((loc(i) + 1) * hb, last), 0))]
        x_args += [p, p, p]
        first += tiles
    if transpose_out:
        out_spec = pl.BlockSpec((tn, tm), lambda i, j: (j, i))
        out_shape = jax.ShapeDtypeStruct((n, m), F32)
    else:
        out_spec = pl.BlockSpec((tm, tn), lambda i, j: (i, j))
        out_shape = jax.ShapeDtypeStruct((m, n), F32)
    return pl.pallas_call(
        kern,
        grid=grid,
        in_specs=x_specs + [pl.BlockSpec((1, d), lambda i, j: (0, 0)),
                            pl.BlockSpec((d, tn), lambda i, j: (0, j)),
                            pl.BlockSpec((3, tn), lambda i, j: (0, j))],
        out_specs=out_spec,
        out_shape=out_shape,
        scratch_shapes=[pltpu.VMEM((tm + 2 * HALO, d), BF16)],
        compiler_params=_cparams(("parallel", "arbitrary")),
        name="proj_conv",
    )(*x_args, gain, w, conv_w.astype(F32))


def _dft_rows(n2):
    return n2 // 2 + 8


@functools.lru_cache(maxsize=None)
def _dft_consts(n2):
    n1 = DFT_N1
    n = n1 * n2
    h = n2 // 2
    kept = h + 1
    hr = _dft_rows(n2)
    k1 = np.arange(n1)
    f1 = np.exp(-2j * np.pi * np.outer(k1, k1) / n1)
    f1_fwd = np.block([[f1.real, f1.imag], [-f1.imag, f1.real]])
    f1_inv = np.block([[f1.real, -f1.imag], [f1.imag, f1.real]])
    k2 = np.arange(n2)
    f2 = np.exp(-2j * np.pi * np.outer(k2, k2) / n2)
    pad_rows = lambda a: np.concatenate([a[:kept], np.zeros((hr - kept,) + a.shape[1:])], axis=0)
    tw = pad_rows(np.exp(-2j * np.pi * np.outer(k2, k1) / n))
    lf_full = np.concatenate([pad_rows(f2.real), pad_rows(f2.imag)], axis=0)
    lf_half = lf_full[:, :h]
    weight = np.where((k2 == 0) | (k2 == h), 1.0, 2.0)[None, :] / n
    li_re = pad_rows((f2.real[:h, :] * weight).T).T
    li_im = pad_rows((f2.imag[:h, :] * weight).T).T
    li_half = np.concatenate([li_re, li_im], axis=1)
    return dict(f1_fwd=f1_fwd, f1_inv=f1_inv, twre=tw.real, twim=tw.imag,
                lf_half=lf_half, lf_full=lf_full, li_half=li_half)


def _consts_dev(n2):
    c = _dft_consts(n2)
    return dict(
        f1_fwd=jnp.asarray(c["f1_fwd"], BF16), f1_inv=jnp.asarray(c["f1_inv"], BF16),
        twre=jnp.asarray(c["twre"], F32), twim=jnp.asarray(c["twim"], F32),
        lf_half=jnp.asarray(c["lf_half"], BF16), lf_full=jnp.asarray(c["lf_full"], BF16),
        li_half=jnp.asarray(c["li_half"], BF16))


def _dft_fwd(sig, lf, twre, twim, f1f, a_scr):
    ns, two_hr, _ = a_scr.shape
    hr = two_hr // 2
    for s in range(ns):
        a_scr[s] = jnp.dot(lf, sig(s).astype(BF16), preferred_element_type=F32)
    are = a_scr[:, 0:hr, :]
    aim = a_scr[:, hr:two_hr, :]
    bre = (are * twre - aim * twim).reshape(ns * hr, DFT_N1)
    bim = (are * twim + aim * twre).reshape(ns * hr, DFT_N1)
    bcat = jnp.concatenate([bre, bim], axis=1).astype(BF16)
    cc = jnp.dot(bcat, f1f, preferred_element_type=F32)
    return cc[:, 0:DFT_N1], cc[:, DFT_N1:2 * DFT_N1]


def _dft_inv_real(yre, yim, li, twre, twim, f1i, d_scr, emit):
    ns, two_hr, _ = d_scr.shape
    hr = two_hr // 2
    ycat = jnp.concatenate([yre, yim], axis=1).astype(BF16)
    bb = jnp.dot(ycat, f1i, preferred_element_type=F32)
    bre = bb[:, 0:DFT_N1].reshape(ns, hr, DFT_N1)
    bim = bb[:, DFT_N1:2 * DFT_N1].reshape(ns, hr, DFT_N1)
    d_scr[:, 0:hr, :] = bre * twre + bim * twim
    d_scr[:, hr:two_hr, :] = bim * twre - bre * twim
    for s in range(ns):
        emit(s, jnp.dot(li, d_scr[s].astype(BF16), preferred_element_type=F32))


def _taps_kernel(bands_ref, w1t_ref, w1c_ref, w1s_ref, b1_ref, w2_ref, b2_ref, w3_ref, b3_ref,
                 fq_ref, w4_ref, dl_ref, o_ref, h_ref, tm_ref, *, seq_len, tl):
    nt = pl.program_id(0)

    @pl.when(pl.program_id(1) == 0)
    def _():
        col = nt * tl + lax.broadcasted_iota(jnp.int32, (1, tl), 1)
        pos = jnp.where(col < seq_len, col, 2 * seq_len - col)
        posf = pos.astype(F32)
        t = posf * np.float32(1.0 / (seq_len - 1))
        ang = bands_ref[...] * (posf * np.float32(2.0 * math.pi / seq_len))
        fq = fq_ref[...]
        pre = (w1t_ref[...] * t
               + jnp.dot(w1c_ref[...], jnp.cos(ang), preferred_element_type=F32, precision=HIGHEST)
               - jnp.dot(w1s_ref[...], jnp.sin(ang), preferred_element_type=F32, precision=HIGHEST))
        h = jnp.sin(fq * (pre + b1_ref[...]))
        h = jnp.sin(fq * (jnp.dot(w2_ref[...], h, preferred_element_type=F32, precision=HIGHEST) + b2_ref[...]))
        h = jnp.sin(fq * (jnp.dot(w3_ref[...], h, preferred_element_type=F32, precision=HIGHEST) + b3_ref[...]))
        h_hi = h.astype(BF16)
        h_ref[0] = h_hi
        h_ref[1] = (h - h_hi.astype(F32)).astype(BF16)
        tm_ref[0:1, :] = t
        tm_ref[1:2, :] = jnp.where(col == seq_len, 0.0, 1.0)

    w_hi = w4_ref[0]
    h_hi = h_ref[0]
    taps = (jnp.dot(w_hi, h_hi, preferred_element_type=F32) + jnp.dot(w_hi, h_ref[1], preferred_element_type=F32)
            + jnp.dot(w4_ref[1], h_hi, preferred_element_type=F32))
    window = jnp.exp(-(dl_ref[...] * tm_ref[0:1, :]))
    o_ref[...] = taps * window * tm_ref[1:2, :]


def _hyena_taps(seq_len, f_w1, f_b1, f_w2, f_b2, f_w3, f_b3, f_w4, f_freq, d_model, *, tl=2048, td=512):
    fh = f_w1.shape[1]
    n_ord = f_w4.shape[1] // (2 * d_model)
    tl = min(tl, seq_len)
    ndt = d_model // td
    nlt = seq_len // tl
    col = lambda v: v.reshape(-1, 1).astype(F32)
    bands = jnp.linspace(1e-4, HY_BANDS - 1, HY_BANDS, dtype=F32).reshape(-1, 1)
    w1 = f_w1.astype(F32).T
    w4t = f_w4.astype(F32).reshape(fh, n_ord, 2, d_model).transpose(1, 2, 3, 0).reshape(n_ord * 2 * d_model, fh)
    w4_hi = w4t.astype(BF16)
    w4t = jnp.stack([w4_hi, (w4t - w4_hi.astype(F32)).astype(BF16)])
    max_decay = math.log(HY_TARGET) / HY_FAST_PCT
    min_decay = math.log(HY_TARGET) / HY_SLOW_PCT
    deltas = jnp.abs(jnp.linspace(min_decay, max_decay, d_model, dtype=F32))
    deltas = jnp.tile(deltas, n_ord).reshape(-1, 1)
    small = lambda a: pl.BlockSpec(a.shape, lambda n, j: (0, 0))
    args = [bands, w1[:, 0:1], w1[:, 1:1 + HY_BANDS], w1[:, 1 + HY_BANDS:1 + 2 * HY_BANDS], col(f_b1),
            f_w2.astype(F32).T, col(f_b2), f_w3.astype(F32).T, col(f_b3), col(f_freq)]
    return pl.pallas_call(
        functools.partial(_taps_kernel, seq_len=seq_len, tl=tl),
        grid=(2 * nlt, n_ord * ndt),
        in_specs=[small(a) for a in args] + [
            pl.BlockSpec((2, td, fh),
                         lambda n, j: (0, (j // ndt * 2 + (n >= nlt).astype(jnp.int32)) * ndt + j % ndt, 0)),
            pl.BlockSpec((td, 1), lambda n, j: (j, 0))],
        out_specs=pl.BlockSpec((td, tl), lambda n, j: (j, n)),
        out_shape=jax.ShapeDtypeStruct((n_ord * d_model, 2 * seq_len), F32),
        scratch_shapes=[pltpu.VMEM((2, fh, tl), BF16), pltpu.VMEM((8, tl), F32)],
        compiler_params=_cparams(("parallel", "arbitrary")),
        name="hyena_taps",
    )(*args, w4t, deltas)


def _spectrum_kernel(x_ref, lf_ref, twre_ref, twim_ref, f1f_ref, re_ref, im_ref, a_scr):
    ct, hr, _ = re_ref.shape
    cre, cim = _dft_fwd(lambda s: x_ref[s], lf_ref[...], twre_ref[...][None], twim_ref[...][None],
                        f1f_ref[...], a_scr)
    re_ref[...] = cre.reshape(ct, hr, DFT_N1)
    im_ref[...] = cim.reshape(ct, hr, DFT_N1)


def _spectrum(taps, *, ct=32):
    c, n = taps.shape
    n2 = n // DFT_N1
    hr = _dft_rows(n2)
    cs = _consts_dev(n2)
    x = taps.reshape(c, n2, DFT_N1)
    blk = pl.BlockSpec((ct, hr, DFT_N1), lambda i: (i, 0, 0))
    full = lambda a: pl.BlockSpec(a.shape, lambda i: (0,) * a.ndim)
    consts = [cs["lf_full"], cs["twre"], cs["twim"], cs["f1_fwd"]]
    return pl.pallas_call(
        _spectrum_kernel,
        grid=(c // ct,),
        in_specs=[pl.BlockSpec((ct, n2, DFT_N1), lambda i: (i, 0, 0))] + [full(a) for a in consts],
        out_specs=[blk, blk],
        out_shape=[jax.ShapeDtypeStruct((c, hr, DFT_N1), F32)] * 2,
        scratch_shapes=[pltpu.VMEM((ct, 2 * hr, DFT_N1), F32)],
        compiler_params=_cparams(("parallel",)),
        name="hyena_spectrum",
    )(x, *consts)


def _fftconv_kernel(x_ref, gate_ref, skip_ref,
                    kpre_ref, kpim_ref, lfp_ref, lip_ref, twpre_ref, twpim_ref,
                    ksre_ref, ksim_ref, lfs_ref, lis_ref, twsre_ref, twsim_ref,
                    f1f_ref, f1i_ref, o_ref, ap_scr, dp_scr, as_scr, ds_scr, *, hp, hs, nb):
    ct = x_ref.shape[0]
    f1f = f1f_ref[...]
    f1i = f1i_ref[...]

    def run(rows, n_sig, kre, kim, lf, li, twre, twim, a_scr, d_scr):
        def sig(s):
            c, r0, h = rows(s)
            return x_ref[c, r0:r0 + h, :]

        cre, cim = _dft_fwd(sig, lf, twre, twim, f1f, a_scr)
        yre = cre * kre - cim * kim
        yim = cre * kim + cim * kre

        def emit(s, y):
            c, r0, h = rows(s)
            xs = x_ref[c, r0:r0 + h, :]
            o_ref[c, r0:r0 + h, :] = gate_ref[c, r0:r0 + h, :] * (y + skip_ref[c] * xs)

        _dft_inv_real(yre, yim, li, twre, twim, f1i, d_scr, emit)

    hrp = kpre_ref.shape[1]
    run(lambda s: (s, 0, hp), ct,
        kpre_ref[...].reshape(ct * hrp, DFT_N1), kpim_ref[...].reshape(ct * hrp, DFT_N1),
        lfp_ref[...], lip_ref[...], twpre_ref[...][None], twpim_ref[...][None], ap_scr, dp_scr)
    hrs = ksre_ref.shape[1]
    rep = lambda k: jnp.broadcast_to(k[:, None], (ct, nb, hrs, DFT_N1)).reshape(ct * nb * hrs, DFT_N1)
    run(lambda s: (s // nb, hp + (s % nb) * hs, hs), ct * nb,
        rep(ksre_ref[...]), rep(ksim_ref[...]),
        lfs_ref[...], lis_ref[...], twsre_ref[...][None], twsim_ref[...][None], as_scr, ds_scr)


def _fftconv(x, x_off, gate, gate_off, skip, kp, kp_off, ks, ks_off, *, d_model, hp, hs, nb, ct=8):
    r = hp + nb * hs
    cp, cs_ = _consts_dev(2 * hp), _consts_dev(2 * hs)
    hrp, hrs = _dft_rows(2 * hp), _dft_rows(2 * hs)
    cb = lambda off: (lambda i: (off // ct + i, 0, 0))
    full = lambda a: pl.BlockSpec(a.shape, lambda i: (0,) * a.ndim)
    kspec = lambda rows, off: pl.BlockSpec((ct, rows, DFT_N1), cb(off))
    skip3 = jnp.broadcast_to(skip.astype(F32).reshape(d_model, 1, 1), (d_model, 1, DFT_N1))
    pc = [cp["lf_half"], cp["li_half"], cp["twre"], cp["twim"]]
    sc = [cs_["lf_half"], cs_["li_half"], cs_["twre"], cs_["twim"]]
    return pl.pallas_call(
        functools.partial(_fftconv_kernel, hp=hp, hs=hs, nb=nb),
        grid=(d_model // ct,),
        in_specs=[pl.BlockSpec((ct, r, DFT_N1), cb(x_off)), pl.BlockSpec((ct, r, DFT_N1), cb(gate_off)),
                  pl.BlockSpec((ct, 1, DFT_N1), cb(0)),
                  kspec(hrp, kp_off), kspec(hrp, kp_off)] + [full(a) for a in pc]
                 + [kspec(hrs, ks_off), kspec(hrs, ks_off)] + [full(a) for a in sc]
                 + [full(cp["f1_fwd"]), full(cp["f1_inv"])],
        out_specs=pl.BlockSpec((ct, r, DFT_N1), cb(0)),
        out_shape=jax.ShapeDtypeStruct((d_model, r, DFT_N1), F32),
        scratch_shapes=[pltpu.VMEM((ct, 2 * hrp, DFT_N1), F32), pltpu.VMEM((ct, 2 * hrp, DFT_N1), F32),
                        pltpu.VMEM((ct * nb, 2 * hrs, DFT_N1), F32), pltpu.VMEM((ct * nb, 2 * hrs, DFT_N1), F32)],
        compiler_params=_cparams(("parallel",)),
        name="hyena_fftconv",
    )(x, gate, skip3, kp[0], kp[1], *pc, ks[0], ks[1], *sc, cp["f1_fwd"], cp["f1_inv"])


def _outproj_t_kernel(zt_ref, w_ref, *refs, part_tiles):
    res_refs, o_ref = refs[:-1], refs[-1]
    i = pl.program_id(0)
    z = zt_ref[...].astype(BF16)
    y = lax.dot_general(z, w_ref[...], (((0,), (0,)), ((), ())), preferred_element_type=F32)
    first = 0
    for res_ref, tiles in zip(res_refs, part_tiles):
        @pl.when((i >= first) & (i < first + tiles))
        def _(res_ref=res_ref):
            o_ref[...] = res_ref[...] + y

        first += tiles


def _outproj_t(zt, w, res, *, tm=512):
    parts = res if isinstance(res, tuple) else (res,)
    k, m = zt.shape
    n = w.shape[1]
    tm = min(tm, m)
    part_tiles = tuple(p.shape[0] // tm for p in parts)
    assert all(p.shape[0] % tm == 0 for p in parts) and sum(part_tiles) * tm == m
    res_specs, first = [], 0
    for tiles in part_tiles:
        res_specs.append(pl.BlockSpec((tm, n), lambda i, first=first, tiles=tiles: (jnp.clip(i - first, 0, tiles - 1), 0)))
        first += tiles
    return pl.pallas_call(
        functools.partial(_outproj_t_kernel, part_tiles=part_tiles),
        grid=(m // tm,),
        in_specs=[pl.BlockSpec((k, tm), lambda i: (0, i)),
                  pl.BlockSpec((k, n), lambda i: (0, 0))] + res_specs,
        out_specs=pl.BlockSpec((tm, n), lambda i: (i, 0)),
        out_shape=jax.ShapeDtypeStruct((m, n), F32),
        compiler_params=_cparams(("parallel",)),
        name="hyena_outproj",
    )(zt, w.astype(BF16), *parts)


def _softplus(x):
    return jnp.maximum(x, 0.0) + jnp.log1p(jnp.exp(-jnp.abs(x)))


def _gdn_kernel(qf_ref, kf_ref, vf_ref, baf_ref, batf_ref, qb_ref, kb_ref, vb_ref, bab_ref, batb_ref,
                alog_r_ref, dtb_r_ref, alog_c_ref, dtb_c_ref, of_ref, ob_ref, sf_ref, sb_ref, *,
                n_steps, f_resets, b_resets, n_pairs, n_sub):
    hp = pl.program_id(0)
    step = pl.program_id(1)
    t = GDN_TILE
    c = GDN_CHUNK
    nc = t // c
    dh = GDN_DH

    @pl.when(_in_set(step, f_resets))
    def _():
        sf_ref[...] = jnp.zeros_like(sf_ref)

    @pl.when(_in_set(n_steps - 1 - step, b_resets))
    def _():
        sb_ref[...] = jnp.zeros_like(sb_ref)

    ri = lax.broadcasted_iota(jnp.int32, (t, t), 0)
    ci = lax.broadcasted_iota(jnp.int32, (t, t), 1)
    same = (ri // c) == (ci // c)
    rw = lax.broadcasted_iota(jnp.int32, (c, t), 0)
    lw = lax.broadcasted_iota(jnp.int32, (c, t), 1)
    lblk = lw // c
    lsub = lw % c
    rsub = lax.broadcasted_iota(jnp.int32, (t, LANES), 0) % c
    lane = lax.broadcasted_iota(jnp.int32, (t, LANES), 1)
    lrow = lax.broadcasted_iota(jnp.int32, (8, t), 1) % c

    def diag_to_wide(full):
        parts = [jnp.where(lblk == j, full[j * c:(j + 1) * c, :], 0.0) for j in range(nc)]
        return functools.reduce(lambda x, y: x + y, parts)

    def wide_to_diag16(wide):
        return jnp.where(same, jnp.concatenate([wide] * nc, axis=0), 0.0).astype(BF16)

    def l2n(x):
        return x * lax.rsqrt(jnp.sum(x * x, axis=-1, keepdims=True) + RMS_EPS)

    def mm(a, b):
        return jnp.dot(a, b, preferred_element_type=F32)

    dirs = [(0, qf_ref, kf_ref, vf_ref, baf_ref, batf_ref), (1, qb_ref, kb_ref, vb_ref, bab_ref, batb_ref)]

    def stage1_dir(d, r0):
        _, _, _, v_ref, ba_ref, bat_ref = dirs[d]
        back = d == 1
        ba = ba_ref[r0:r0 + t, :]
        g_all = -jnp.exp(alog_r_ref[...]) * _softplus(ba + dtb_r_ref[...])
        sh = 1
        while sh < c:
            if back:
                g_all = g_all + jnp.where(rsub < c - sh, pltpu.roll(g_all, t - sh, axis=0), 0.0)
            else:
                g_all = g_all + jnp.where(rsub >= sh, pltpu.roll(g_all, sh, axis=0), 0.0)
            sh *= 2
        return dict(back=back, incl=(lsub >= rw) if back else (lsub <= rw),
                    strict=(lsub > rw) if back else (lsub < rw),
                    beta_all=jax.nn.sigmoid(ba), g_all=g_all, bat_ref=bat_ref, v_ref=v_ref, r0=r0)

    def stage1_pair(d, pp, r0):
        _, q_ref, k_ref, _, _, _ = dirs[d]
        q = l2n(q_ref[r0:r0 + t, pp * dh:(pp + 1) * dh]) * np.float32(GDN_DH ** -0.5)
        k = l2n(k_ref[r0:r0 + t, pp * dh:(pp + 1) * dh])
        k16 = k.astype(BF16)
        qk_kk = lax.dot_general(jnp.concatenate([q.astype(BF16), k16], axis=0), k16, (((1,), (1,)), ((), ())),
                                preferred_element_type=F32)
        return dict(q=q, k=k, qk=diag_to_wide(qk_kk[0:t]), kk=diag_to_wide(qk_kk[t:2 * t]))

    def stage2(d, pd, pr, ph):
        back = pd["back"]
        head = 2 * n_pairs * hp + ph
        jb = d * GDN_NV + head
        ja = 2 * GDN_NV + d * GDN_NV + head
        beta_c = jnp.sum(jnp.where(lane == jb, pd["beta_all"], 0.0), axis=1, keepdims=True)
        gc_c = jnp.sum(jnp.where(lane == ja, pd["g_all"], 0.0), axis=1, keepdims=True)
        a_row = pd["bat_ref"][pl.ds(ja, 1), pd["r0"]:pd["r0"] + t]
        g_row = -jnp.exp(alog_c_ref[pl.ds(ja, 1), :]) * _softplus(a_row + dtb_c_ref[pl.ds(ja, 1), :])
        g_row = jnp.broadcast_to(g_row, (8, t))
        sh = 1
        while sh < c:
            if back:
                g_row = g_row + jnp.where(lrow < c - sh, pltpu.roll(g_row, t - sh, axis=1), 0.0)
            else:
                g_row = g_row + jnp.where(lrow >= sh, pltpu.roll(g_row, sh, axis=1), 0.0)
            sh *= 2
        gc_r = g_row[0:1, :]
        incl = pd["incl"]
        decay = jnp.where(incl, jnp.exp(jnp.where(incl, diag_to_wide(gc_c) - gc_r, 0.0)), 0.0)
        p = jnp.where(pd["strict"], -(diag_to_wide(beta_c) * pr["kk"] * decay), 0.0)
        return dict(pd=pd, pr=pr, ph=ph, beta_c=beta_c, gc_c=gc_c, decay=decay, p=p)

    def stage3(group):
        for v in group:
            v["nn"] = v["p"]
            v["pm"] = mm(v["p"].astype(BF16), wide_to_diag16(v["p"]))
            yield
        m = 2
        while 2 * m < c:
            for v in group:
                both = mm(jnp.concatenate([v["pm"], v["nn"]], axis=0).astype(BF16), wide_to_diag16(v["pm"]))
                v["nn"] = v["nn"] + v["pm"] + both[c:2 * c]
                v["pm"] = both[0:c]
                yield
            m *= 2
        for v in group:
            v["nn"] = v["nn"] + v["pm"] + mm(v["nn"].astype(BF16), wide_to_diag16(v["pm"]))
            yield

    def stage4(group):
        for v in group:
            pd, pr, ph, beta_c, gc_c = v["pd"], v["pr"], v["ph"], v["beta_c"], v["gc_c"]
            vv = pd["v_ref"][pd["r0"]:pd["r0"] + t, ph * dh:(ph + 1) * dh]
            rhs = jnp.concatenate([vv * beta_c, pr["k"] * (beta_c * jnp.exp(gc_c))], axis=1)
            v["uw16"] = (rhs + mm(wide_to_diag16(v["nn"]), rhs.astype(BF16))).astype(BF16)
        for v in group:
            pd, pr, gc_c = v["pd"], v["pr"], v["gc_c"]
            qq = mm(wide_to_diag16(jnp.where(pd["incl"], pr["qk"] * v["decay"], 0.0)), v["uw16"])
            gc3 = gc_c.reshape(nc, c, 1)
            gl3 = gc3[:, 0:1, :] if pd["back"] else gc3[:, c - 1:c, :]
            kg16 = (pr["k"] * jnp.exp(jnp.broadcast_to(gl3, (nc, c, 1)).reshape(t, 1) - gc_c)).astype(BF16)
            v["qp16"] = (pr["q"] * jnp.exp(gc_c) - qq[:, dh:2 * dh]).astype(BF16)
            v["qq"] = qq
            v["gl3"] = gl3
            v["rp"] = [lax.dot_general(kg16[n * c:(n + 1) * c], v["uw16"][n * c:(n + 1) * c],
                                       (((0,), (0,)), ((), ())), preferred_element_type=F32)
                       for n in range(nc)]

    units = [(sub, d) for sub in range(n_sub) for d in range(2)]
    row0 = lambda sub, d: (n_sub - 1 - sub if d == 1 else sub) * t
    var = {}
    groups = {u: [] for u in units}

    def setup(u):
        sub, d = u
        pd = stage1_dir(d, row0(sub, d))
        yield
        for pp in range(n_pairs):
            pr = stage1_pair(d, pp, row0(sub, d))
            yield
            for ph in (2 * pp, 2 * pp + 1):
                var[(sub, d, ph)] = stage2(d, pd, pr, ph)
                groups[u].append(var[(sub, d, ph)])
                yield

    def run(main, filler=None, every=1):
        for i, _ in enumerate(main):
            if filler is not None and i % every == every - 1:
                next(filler, None)
        if filler is not None:
            for _ in filler:
                pass

    zero = jnp.zeros((dh, dh), F32)
    s_refs = [sf_ref, sb_ref]
    outs = [of_ref, ob_ref]
    state = [[s_refs[d][ph] for ph in range(2 * n_pairs)] for d in range(2)]

    def stage5(sub):
        for i in range(nc):
            for d in range(2):
                n = nc - 1 - i if d == 1 else i
                sl = slice(n * c, (n + 1) * c)
                so = slice(row0(sub, d) + n * c, row0(sub, d) + (n + 1) * c)
                for pp in range(n_pairs):
                    v0, v1 = var[(sub, d, 2 * pp)], var[(sub, d, 2 * pp + 1)]
                    s0, s1 = state[d][2 * pp], state[d][2 * pp + 1]
                    s_d = jnp.concatenate([jnp.concatenate([s0, zero], axis=1),
                                           jnp.concatenate([zero, s1], axis=1)], axis=0).astype(BF16)
                    lhs = jnp.concatenate([
                        jnp.concatenate([v0["rp"][n][:, dh:2 * dh], v1["rp"][n][:, dh:2 * dh]], axis=1).astype(BF16),
                        jnp.concatenate([v0["qp16"][sl], v1["qp16"][sl]], axis=1)], axis=0)
                    z = mm(lhs, s_d)
                    c0 = 2 * pp * dh
                    outs[d][so, c0:c0 + dh] = z[dh:dh + c, 0:dh] + v0["qq"][sl, 0:dh]
                    outs[d][so, c0 + dh:c0 + 2 * dh] = z[dh:dh + c, dh:2 * dh] + v1["qq"][sl, 0:dh]
                    state[d][2 * pp] = jnp.exp(v0["gl3"][n]) * s0 - z[0:dh, 0:dh] + v0["rp"][n][:, 0:dh]
                    state[d][2 * pp + 1] = jnp.exp(v1["gl3"][n]) * s1 - z[0:dh, dh:2 * dh] + v1["rp"][n][:, 0:dh]

    n_setup = 1 + 3 * n_pairs
    n_doubling = 2 * n_pairs * (c.bit_length() - 1)
    run(setup(units[0]))
    for i, u in enumerate(units):
        filler = setup(units[i + 1]) if i + 1 < len(units) else None
        run(stage3(groups[u]), filler, every=max(1, n_doubling // n_setup))
        if u[1] == 1:
            stage4(groups[(u[0], 0)])
            stage4(groups[(u[0], 1)])
            stage5(u[0])
    for d in range(2):
        for ph in range(2 * n_pairs):
            s_refs[d][ph] = state[d][ph]


def _gdn_scan(qkv, ba, a_log, dt_bias, *, starts, ends, n_pairs=4, n_sub=2):
    m = qkv.shape[0]
    t = n_sub * GDN_TILE
    assert all(s % t == 0 for s in starts) and all(e % t == 0 for e in ends)
    n_tiles = m // t
    bat = ba.T
    pad = lambda a: jnp.pad(a.astype(F32).reshape(-1), (2 * GDN_NV, LANES - 4 * GDN_NV))
    alog_r = pad(a_log).reshape(1, LANES)
    dtb_r = pad(dt_bias).reshape(1, LANES)
    alog_c = pad(a_log).reshape(LANES, 1)
    dtb_c = pad(dt_bias).reshape(LANES, 1)
    f_resets = tuple(s // t for s in starts)
    b_resets = tuple(e // t - 1 for e in ends)
    fwd = lambda h, s: s
    bwd = lambda h, s: n_tiles - 1 - s

    kw = n_pairs * GDN_DH
    key_blocks = GDN_NK // n_pairs

    def specs(tile):
        return [pl.BlockSpec((t, kw), lambda h, s: (tile(h, s), h)),
                pl.BlockSpec((t, kw), lambda h, s: (tile(h, s), key_blocks + h)),
                pl.BlockSpec((t, 2 * kw), lambda h, s: (tile(h, s), key_blocks + h)),
                pl.BlockSpec((t, LANES), lambda h, s: (tile(h, s), 0)),
                pl.BlockSpec((LANES, t), lambda h, s: (0, tile(h, s)))]

    small = lambda a: pl.BlockSpec(a.shape, lambda h, s: (0, 0))
    out = jax.ShapeDtypeStruct((m, GDN_NV * GDN_DH), F32)
    return pl.pallas_call(
        functools.partial(_gdn_kernel, n_steps=n_tiles, f_resets=f_resets, b_resets=b_resets, n_pairs=n_pairs,
                          n_sub=n_sub),
        grid=(key_blocks, n_tiles),
        in_specs=specs(fwd) + specs(bwd) + [small(alog_r), small(dtb_r), small(alog_c), small(dtb_c)],
        out_specs=[pl.BlockSpec((t, 2 * kw), lambda h, s: (s, h)),
                   pl.BlockSpec((t, 2 * kw), lambda h, s: (n_tiles - 1 - s, h))],
        out_shape=[out, out],
        scratch_shapes=[pltpu.VMEM((2 * n_pairs, GDN_DH, GDN_DH), F32),
                        pltpu.VMEM((2 * n_pairs, GDN_DH, GDN_DH), F32)],
        compiler_params=_cparams(("parallel", "arbitrary")),
        name="gdn_scan",
    )(qkv, qkv, qkv, ba, bat, qkv, qkv, qkv, ba, bat, alog_r, dtb_r, alog_c, dtb_c)


def _gdn_out_kernel(of_ref, ob_ref, z_ref, nw_ref, w_ref, res_ref, o_ref, y_ref):
    nw = nw_ref[...]
    for h in range(GDN_NV):
        sl = slice(h * GDN_DH, (h + 1) * GDN_DH)
        o = of_ref[:, sl] + ob_ref[:, sl]
        o = o * lax.rsqrt(jnp.mean(o * o, axis=-1, keepdims=True) + RMS_EPS) * nw
        z = z_ref[:, sl]
        y_ref[:, sl] = (o * (z * jax.nn.sigmoid(z))).astype(BF16)
    o_ref[...] = res_ref[...] + jnp.dot(y_ref[...], w_ref[...], preferred_element_type=F32)


def _gdn_out(o_f, o_b, z, norm_w, w, res, *, tm=512):
    m, kv = o_f.shape
    n = w.shape[1]
    tm = min(tm, m)
    row = lambda width: pl.BlockSpec((tm, width), lambda i: (i, 0))
    return pl.pallas_call(
        _gdn_out_kernel,
        grid=(m // tm,),
        in_specs=[row(kv), row(kv), row(kv),
                  pl.BlockSpec((1, GDN_DH), lambda i: (0, 0)),
                  pl.BlockSpec((kv, n), lambda i: (0, 0)),
                  row(n)],
        out_specs=row(n),
        out_shape=jax.ShapeDtypeStruct((m, n), F32),
        scratch_shapes=[pltpu.VMEM((tm, kv), BF16)],
        compiler_params=_cparams(("parallel",)),
        name="gdn_out",
    )(o_f, o_b, z, norm_w.astype(F32).reshape(1, GDN_DH), w.astype(BF16), res)


def _swiglu_tile(h, wg_ref, wu_ref, wd_ref, lead=()):
    tf = wg_ref.shape[-1]
    sub = 2 * LANES if tf % (2 * LANES) == 0 else tf
    y = None
    prev = None

    def finish(a, u, c0):
        act = (a * jax.nn.sigmoid(a) * u).astype(BF16)
        return jnp.dot(act, wd_ref[lead + (slice(c0, c0 + sub), slice(None))], preferred_element_type=F32)

    for c0 in range(0, tf, sub):
        cols = lead + (slice(None), slice(c0, c0 + sub))
        a = jnp.dot(h, wg_ref[cols], preferred_element_type=F32)
        u = jnp.dot(h, wu_ref[cols], preferred_element_type=F32)
        if prev is not None:
            part = finish(*prev)
            y = part if y is None else y + part
        prev = (a, u, c0)
    part = finish(*prev)
    return part if y is None else y + part


def _ffn_kernel(x_ref, g_ref, wg_ref, wu_ref, wd_ref, gf_ref, o_ref, h_ref, acc_ref, *, final_norm):
    f = pl.program_id(1)

    @pl.when(f == 0)
    def _():
        h_ref[...] = _norm_rows(x_ref[...], g_ref[...]).astype(BF16)
        acc_ref[...] = jnp.zeros_like(acc_ref)

    acc_ref[...] += _swiglu_tile(h_ref[...], wg_ref, wu_ref, wd_ref)

    @pl.when(f == pl.num_programs(1) - 1)
    def _():
        out = x_ref[...] + acc_ref[...]
        if final_norm:
            out = _norm_rows(out, gf_ref[...])
        o_ref[...] = out


def _ffn(x, gain, w_gate, w_up, w_down, *, final_gain=None, tm=1024, tf=512):
    m, d = x.shape
    ff = w_gate.shape[1]
    tm = min(tm, m)
    tf = min(tf, ff)
    final_norm = final_gain is not None
    gf = (final_gain if final_norm else gain).astype(F32).reshape(1, d)
    return pl.pallas_call(
        functools.partial(_ffn_kernel, final_norm=final_norm),
        grid=(m // tm, ff // tf),
        in_specs=[pl.BlockSpec((tm, d), lambda i, f: (i, 0)),
                  pl.BlockSpec((1, d), lambda i, f: (0, 0)),
                  pl.BlockSpec((d, tf), lambda i, f: (0, f)),
                  pl.BlockSpec((d, tf), lambda i, f: (0, f)),
                  pl.BlockSpec((tf, d), lambda i, f: (f, 0)),
                  pl.BlockSpec((1, d), lambda i, f: (0, 0))],
        out_specs=pl.BlockSpec((tm, d), lambda i, f: (i, 0)),
        out_shape=jax.ShapeDtypeStruct((m, d), F32),
        scratch_shapes=[pltpu.VMEM((tm, d), BF16), pltpu.VMEM((tm, d), F32)],
        compiler_params=_cparams(("parallel", "arbitrary")),
        name="ffn_mixer",
    )(x, gain.astype(F32).reshape(1, d), w_gate.astype(BF16), w_up.astype(BF16), w_down.astype(BF16), gf)


def _moe_kernel(x_ref, g_ref, r_ref, rb_ref, wg_ref, wu_ref, wd_ref, gf_ref, o_ref,
                h_ref, acc_ref, gates_ref, posc_ref, posr_ref, hx_ref, y_ref, nblk_ref, *,
                n_exp, final_norm, rb):
    e = pl.program_id(1)
    f = pl.program_id(2)
    tm = x_ref.shape[0]
    lane = lax.broadcasted_iota(jnp.int32, (tm, LANES), 1)

    @pl.when((e == 0) & (f == 0))
    def _():
        h = _norm_rows(x_ref[...], g_ref[...])
        h_ref[...] = h.astype(BF16)
        acc_ref[...] = jnp.zeros_like(acc_ref)
        h_hi = h_ref[...]
        h_lo = (h - h_hi.astype(F32)).astype(BF16)
        r_hi = r_ref[0]
        logits = (jnp.dot(h_hi, r_hi, preferred_element_type=F32)
                  + jnp.dot(h_hi, r_ref[1], preferred_element_type=F32)
                  + jnp.dot(h_lo, r_hi, preferred_element_type=F32)) + rb_ref[...]
        logits = jnp.where(lane < n_exp, logits, -jnp.inf)
        m1 = jnp.max(logits, axis=1, keepdims=True)
        i1 = jnp.min(jnp.where(logits == m1, lane, LANES), axis=1, keepdims=True)
        rest = jnp.where(lane == i1, -jnp.inf, logits)
        m2 = jnp.max(rest, axis=1, keepdims=True)
        i2 = jnp.min(jnp.where(rest == m2, lane, LANES), axis=1, keepdims=True)
        e2 = jnp.exp(m2 - m1)
        w1 = 1.0 / (1.0 + e2)
        gates_ref[...] = jnp.where(lane == i1, w1, 0.0) + jnp.where(lane == i2, e2 * w1, 0.0)
        chosen = jnp.where((lane == i1) | (lane == i2), 1.0, 0.0)
        ri = lax.broadcasted_iota(jnp.int32, (tm, tm), 0)
        ci = lax.broadcasted_iota(jnp.int32, (tm, tm), 1)
        before = jnp.where(ci < ri, 1.0, 0.0).astype(BF16)
        rank = jnp.dot(before, chosen.astype(BF16), preferred_element_type=F32)
        posc = jnp.where(chosen > 0.5, rank, -1.0)
        posc_ref[...] = posc
        posr_ref[...] = posc.T
        for ee in range(n_exp):
            cnt = jnp.sum(chosen[:, ee:ee + 1]).astype(jnp.int32)
            nblk_ref[ee] = (cnt + rb - 1) // rb

    nblk = nblk_ref[e]

    @pl.when(f == 0)
    def _():
        pos_row = posr_ref[pl.ds(e, 1), :]
        h = h_ref[...]

        def gather(b, carry):
            r0 = pl.multiple_of(b * 2 * rb, 2 * rb)
            rows = (lax.broadcasted_iota(jnp.int32, (2 * rb, 1), 0) + r0).astype(F32)
            onehot = jnp.where(pos_row == rows, 1.0, 0.0).astype(BF16)
            hx_ref[pl.ds(r0, 2 * rb), :] = jnp.dot(onehot, h, preferred_element_type=F32).astype(BF16)
            y_ref[pl.ds(r0, 2 * rb), :] = jnp.zeros((2 * rb, y_ref.shape[1]), F32)
            return carry

        lax.fori_loop(0, (nblk + 1) // 2, gather, 0)

    def expert(r0, rows):
        y_ref[pl.ds(r0, rows), :] += _swiglu_tile(hx_ref[pl.ds(r0, rows), :], wg_ref, wu_ref, wd_ref, lead=(0,))

    def expert_pair(b, carry):
        expert(pl.multiple_of(b * 2 * rb, 2 * rb), 2 * rb)
        return carry

    lax.fori_loop(0, nblk // 2, expert_pair, 0)

    @pl.when(nblk % 2 == 1)
    def _():
        expert(pl.multiple_of((nblk - 1) * rb, rb), rb)

    @pl.when(f == pl.num_programs(2) - 1)
    def _():
        pos_col = jnp.sum(jnp.where(lane == e, posc_ref[...], 0.0), axis=1, keepdims=True)
        gate = jnp.sum(jnp.where(lane == e, gates_ref[...], 0.0), axis=1, keepdims=True)

        def scatter(b, carry):
            r0 = pl.multiple_of(b * 2 * rb, 2 * rb)
            cols = (lax.broadcasted_iota(jnp.int32, (1, 2 * rb), 1) + r0).astype(F32)
            onehot = jnp.where(pos_col == cols, 1.0, 0.0).astype(BF16)
            back = jnp.dot(onehot, y_ref[pl.ds(r0, 2 * rb), :].astype(BF16), preferred_element_type=F32)
            acc_ref[...] += gate * back
            return carry

        lax.fori_loop(0, (nblk + 1) // 2, scatter, 0)

    @pl.when((e == n_exp - 1) & (f == pl.num_programs(2) - 1))
    def _():
        out = x_ref[...] + acc_ref[...]
        if final_norm:
            out = _norm_rows(out, gf_ref[...])
        o_ref[...] = out


def _moe(x, gain, router, router_bias, w_gate, w_up, w_down, *, final_gain=None, tm=1024, tf=1792, rb=128):
    m, d = x.shape
    n_exp, _, ff = w_gate.shape
    tm = min(tm, m)
    tf = tf if ff % tf == 0 else min(512, ff)
    assert (tm // rb) % 2 == 0
    r = jnp.pad(router.astype(F32), ((0, 0), (0, LANES - n_exp)))
    r_hi = r.astype(BF16)
    r = jnp.stack([r_hi, (r - r_hi.astype(F32)).astype(BF16)])
    rbias = jnp.pad(router_bias.astype(F32).reshape(1, n_exp), ((0, 0), (0, LANES - n_exp)))
    final_norm = final_gain is not None
    gf = (final_gain if final_norm else gain).astype(F32).reshape(1, d)
    return pl.pallas_call(
        functools.partial(_moe_kernel, n_exp=n_exp, final_norm=final_norm, rb=rb),
        grid=(m // tm, n_exp, ff // tf),
        in_specs=[pl.BlockSpec((tm, d), lambda i, e, f: (i, 0), pipeline_mode=pl.Buffered(1)),
                  pl.BlockSpec((1, d), lambda i, e, f: (0, 0)),
                  pl.BlockSpec((2, d, LANES), lambda i, e, f: (0, 0, 0)),
                  pl.BlockSpec((1, LANES), lambda i, e, f: (0, 0)),
                  pl.BlockSpec((1, d, tf), lambda i, e, f: (e, 0, f)),
                  pl.BlockSpec((1, d, tf), lambda i, e, f: (e, 0, f)),
                  pl.BlockSpec((1, tf, d), lambda i, e, f: (e, f, 0)),
                  pl.BlockSpec((1, d), lambda i, e, f: (0, 0))],
        out_specs=pl.BlockSpec((tm, d), lambda i, e, f: (i, 0)),
        out_shape=jax.ShapeDtypeStruct((m, d), F32),
        scratch_shapes=[pltpu.VMEM((tm, d), BF16), pltpu.VMEM((tm, d), F32), pltpu.VMEM((tm, LANES), F32),
                        pltpu.VMEM((tm, LANES), F32), pltpu.VMEM((LANES, tm), F32),
                        pltpu.VMEM((tm, d), BF16), pltpu.VMEM((tm, d), F32), pltpu.SMEM((n_exp,), jnp.int32)],
        compiler_params=_cparams(("parallel", "arbitrary", "arbitrary")),
        name="moe_mixer",
    )(x, gain.astype(F32).reshape(1, d), r, rbias, w_gate.astype(BF16), w_up.astype(BF16),
      w_down.astype(BF16), gf)


def _hyena_layer(x, gain, lp, ls, nb_p, nb_s, w_in, conv_w, f_w1, f_b1, f_w2, f_b2, f_w3, f_b3, f_w4,
                 f_freq, skip, w_out, *, starts, ends):
    parts = x if isinstance(x, tuple) else (x,)
    m, d = sum(p.shape[0] for p in parts), parts[0].shape[1]
    assert nb_p == 1 and lp % (2 * DFT_N1) == 0 and ls % (2 * DFT_N1) == 0
    hp, hs = lp // DFT_N1, ls // DFT_N1
    ut = _proj(x, gain, w_in, conv_w, transpose_out=True, starts=starts, ends=ends)
    u3 = ut.reshape(3 * d, m // DFT_N1, DFT_N1)
    filt = (f_w1, f_b1, f_w2, f_b2, f_w3, f_b3, f_w4, f_freq)
    kp = _spectrum(_hyena_taps(lp, *filt, d))
    ks = _spectrum(_hyena_taps(ls, *filt, d))
    conv = functools.partial(_fftconv, d_model=d, hp=hp, hs=hs, nb=nb_s)
    z1 = conv(u3, 0, u3, d, skip[0], kp, 0, ks, 0)
    z2 = conv(z1, 0, u3, 2 * d, skip[1], kp, d, ks, d)
    return _outproj_t(z2.reshape(d, m), w_out, x)


def _gdn_layer(x, gain, w_in, conv_w, a_log, dt_bias, norm_w, w_out, *, starts, ends):
    key, val = GDN_NK * GDN_DH, GDN_NV * GDN_DH
    cd = 2 * key + val
    qkv = _proj(x, gain, w_in[:, :cd], conv_w, act="silu", starts=starts, ends=ends)
    z = _proj(x, gain, w_in[:, cd:cd + val])
    w_ba = jnp.pad(w_in[:, cd + val:], ((0, 0), (0, LANES - 4 * GDN_NV)))
    ba = _proj(x, gain, w_ba)
    o_f, o_b = _gdn_scan(qkv, ba, a_log, dt_bias, starts=starts, ends=ends)
    return _gdn_out(o_f, o_b, z, norm_w, w_out, x)


def kernel(x_prompt, x_sample, norm_mix, norm_ffn, norm_final, hy_w_in, hy_conv, hy_f_w1, hy_f_b1, hy_f_w2, hy_f_b2, hy_f_w3, hy_f_b3, hy_f_w4, hy_f_freq, hy_skip, hy_w_out, gdn_w_in, gdn_conv, gdn_a_log, gdn_dt_bias, gdn_norm, gdn_w_out, ffn_w_gate, ffn_w_up, ffn_w_down, moe_router, moe_router_bias, moe_w_gate, moe_w_up, moe_w_down):
    bp, lp, d = x_prompt.shape
    bs, ls, _ = x_sample.shape
    x = (x_prompt.reshape(bp * lp, d), x_sample.reshape(bs * ls, d))
    starts = tuple(b * lp for b in range(bp)) + tuple(bp * lp + b * ls for b in range(bs))
    ends = tuple(s + lp for s in starts[:bp]) + tuple(s + ls for s in starts[bp:])
    depth = norm_mix.shape[0]
    for i in range(depth):
        j = i // 2
        last = i == depth - 1
        if i % 2 == 0:
            x = _hyena_layer(x, norm_mix[i], lp, ls, bp, bs, hy_w_in[j], hy_conv[j], hy_f_w1[j], hy_f_b1[j],
                             hy_f_w2[j], hy_f_b2[j], hy_f_w3[j], hy_f_b3[j], hy_f_w4[j], hy_f_freq[j],
                             hy_skip[j], hy_w_out[j], starts=starts, ends=ends)
            x = _ffn(x, norm_ffn[i], ffn_w_gate[j], ffn_w_up[j], ffn_w_down[j],
                     final_gain=norm_final if last else None)
        else:
            if isinstance(x, tuple):
                x = jnp.concatenate(x, axis=0)
            x = _gdn_layer(x, norm_mix[i], gdn_w_in[j], gdn_conv[j], gdn_a_log[j], gdn_dt_bias[j],
                           gdn_norm[j], gdn_w_out[j], starts=starts, ends=ends)
            x = _moe(x, norm_ffn[i], moe_router[j], moe_router_bias[j], moe_w_gate[j], moe_w_up[j],
                     moe_w_down[j], final_gain=norm_final if last else None)
    y_prompt = x[:bp * lp].reshape(bp, lp, d)
    y_sample = x[bp * lp:].reshape(bs, ls, d)
    return (y_prompt, y_sample)
```

```python
import functools
import math

import numpy as np
import jax
import jax.numpy as jnp
from jax import lax
from jax.experimental import pallas as pl
from jax.experimental.pallas import tpu as pltpu

F32 = jnp.float32
BF16 = jnp.bfloat16

RMS_EPS = 1e-6
HY_BANDS = 16
HY_FAST_PCT = 0.3
HY_SLOW_PCT = 1.5
HY_TARGET = 1e-2
GDN_NK = 8
GDN_NV = 16
GDN_DH = 128
TOP_K = 2

LANES = 128
HALO = 16
DFT_N1 = 256
GDN_CHUNK = 64
GDN_TILE = 256
VMEM_LIMIT = 56 * 1024 * 1024

HIGHEST = lax.Precision.HIGHEST


def _cparams(sem):
    return pltpu.CompilerParams(dimension_semantics=sem, vmem_limit_bytes=VMEM_LIMIT)


def _in_set(v, values):
    r = v == values[0]
    for b in values[1:]:
        r = jnp.logical_or(r, v == b)
    return r


def _norm_rows(x, g):
    ms = jnp.mean(x * x, axis=-1, keepdims=True)
    return x * lax.rsqrt(ms + RMS_EPS) * g


def _proj_conv_kernel(*refs, tm, part_tiles, starts, ends, act, transpose_out):
    n_parts = len(part_tiles)
    g_ref, w_ref, cw_ref, o_ref, h_ref = refs[3 * n_parts:]
    i = pl.program_id(0)
    first = 0
    for p, tiles in enumerate(part_tiles):
        xp_ref, x_ref, xn_ref = refs[3 * p:3 * p + 3]

        @pl.when((pl.program_id(1) == 0) & (i >= first) & (i < first + tiles))
        def _(xp_ref=xp_ref, x_ref=x_ref, xn_ref=xn_ref):
            g = g_ref[...]
            row0 = i * tm
            hp = jnp.where(_in_set(row0, starts), 0.0, _norm_rows(xp_ref[...], g))
            hn = jnp.where(_in_set(row0 + tm, ends), 0.0, _norm_rows(xn_ref[...], g))
            h_ref[0:HALO, :] = hp.astype(BF16)
            h_ref[HALO:HALO + tm, :] = _norm_rows(x_ref[...], g).astype(BF16)
            h_ref[HALO + tm:HALO + tm + HALO, :] = hn.astype(BF16)

        first += tiles

    n = tm + 2 * HALO
    tn = w_ref.shape[1]
    sub = min(tn, 2 * LANES)

    def finish(p, c0):
        up = pltpu.roll(p, 1, axis=0)
        dn = pltpu.roll(p, n - 1, axis=0)
        cw = cw_ref[:, c0:c0 + sub]
        y = (cw[0:1, :] * up[HALO:HALO + tm, :] + cw[1:2, :] * p[HALO:HALO + tm, :]
             + cw[2:3, :] * dn[HALO:HALO + tm, :])
        if act == "silu":
            y = y * jax.nn.sigmoid(y)
        if transpose_out:
            o_ref[c0:c0 + sub, :] = y.T
        else:
            o_ref[:, c0:c0 + sub] = y

    h = h_ref[...]
    prev = None
    for c0 in range(0, tn, sub):
        p = jnp.dot(h, w_ref[:, c0:c0 + sub], preferred_element_type=F32)
        if prev is not None:
            finish(*prev)
        prev = (p, c0)
    finish(*prev)


def _proj_kernel(x_ref, g_ref, w_ref, o_ref, h_ref):
    @pl.when(pl.program_id(1) == 0)
    def _():
        h_ref[...] = _norm_rows(x_ref[...], g_ref[...]).astype(BF16)

    o_ref[...] = jnp.dot(h_ref[...], w_ref[...], preferred_element_type=F32)


def _proj(x, gain, w, conv_w=None, *, act=None, transpose_out=False, starts=(), ends=(),
          tm=1024, tn=512):
    parts = x if isinstance(x, tuple) else (x,)
    m = sum(p.shape[0] for p in parts)
    d = parts[0].shape[1]
    n = w.shape[1]
    tm = min(tm, m)
    if conv_w is not None and n % (2 * tn) == 0:
        tn = 2 * tn
    tn = min(tn, n)
    grid = (m // tm, n // tn)
    gain = gain.reshape(1, d).astype(F32)
    w = w.astype(BF16)
    if conv_w is None:
        assert len(parts) == 1
        return pl.pallas_call(
            _proj_kernel,
            grid=grid,
            in_specs=[pl.BlockSpec((tm, d), lambda i, j: (i, 0)),
                      pl.BlockSpec((1, d), lambda i, j: (0, 0)),
                      pl.BlockSpec((d, tn), lambda i, j: (0, j))],
            out_specs=pl.BlockSpec((tm, tn), lambda i, j: (i, j)),
            out_shape=jax.ShapeDtypeStruct((m, n), F32),
            scratch_shapes=[pltpu.VMEM((tm, d), BF16)],
            compiler_params=_cparams(("parallel", "arbitrary")),
            name="proj",
        )(parts[0], gain, w)
    hb = tm // HALO
    part_tiles = tuple(p.shape[0] // tm for p in parts)
    assert all(p.shape[0] % tm == 0 for p in parts)
    assert all(sum(part_tiles[:k + 1]) * tm in ends for k in range(len(parts)))
    kern = functools.partial(_proj_conv_kernel, tm=tm, part_tiles=part_tiles, starts=tuple(starts),
                             ends=tuple(ends), act=act, transpose_out=transpose_out)
    x_specs, x_args, first = [], [], 0
    for p, tiles in zip(parts, part_tiles):
        last = p.shape[0] // HALO - 1
        loc = lambda i, first=first, tiles=tiles: jnp.clip(i - first, 0, tiles - 1)
        x_specs += [pl.BlockSpec((HALO, d), lambda i, j, loc=loc: (jnp.maximum(loc(i) * hb - 1, 0), 0)),
                    pl.BlockSpec((tm, d), lambda i, j, loc=loc: (loc(i), 0)),
                    pl.BlockSpec((HALO, d), lambda i, j, loc=loc, last=last: (jnp.minimum((loc(i) + 1) * hb, last), 0))]
        x_args += [p, p, p]
        first += tiles
    if transpose_out:
        out_spec = pl.BlockSpec((tn, tm), lambda i, j: (j, i))
        out_shape = jax.ShapeDtypeStruct((n, m), F32)
    else:
        out_spec = pl.BlockSpec((tm, tn), lambda i, j: (i, j))
        out_shape = jax.ShapeDtypeStruct((m, n), F32)
    return pl.pallas_call(
        kern,
        grid=grid,
        in_specs=x_specs + [pl.BlockSpec((1, d), lambda i, j: (0, 0)),
                            pl.BlockSpec((d, tn), lambda i, j: (0, j)),
                            pl.BlockSpec((3, tn), lambda i, j: (0, j))],
        out_specs=out_spec,
        out_shape=out_shape,
        scratch_shapes=[pltpu.VMEM((tm + 2 * HALO, d), BF16)],
        compiler_params=_cparams(("parallel", "arbitrary")),
        name="proj_conv",
    )(*x_args, gain, w, conv_w.astype(F32))


def _dft_rows(n2):
    return n2 // 2 + 8


@functools.lru_cache(maxsize=None)
def _dft_consts(n2):
    n1 = DFT_N1
    n = n1 * n2
    h = n2 // 2
    kept = h + 1
    hr = _dft_rows(n2)
    k1 = np.arange(n1)
    f1 = np.exp(-2j * np.pi * np.outer(k1, k1) / n1)
    f1_fwd = np.block([[f1.real, f1.imag], [-f1.imag, f1.real]])
    f1_inv = np.block([[f1.real, -f1.imag], [f1.imag, f1.real]])
    k2 = np.arange(n2)
    f2 = np.exp(-2j * np.pi * np.outer(k2, k2) / n2)
    pad_rows = lambda a: np.concatenate([a[:kept], np.zeros((hr - kept,) + a.shape[1:])], axis=0)
    tw = pad_rows(np.exp(-2j * np.pi * np.outer(k2, k1) / n))
    lf_full = np.concatenate([pad_rows(f2.real), pad_rows(f2.imag)], axis=0)
    lf_half = lf_full[:, :h]
    weight = np.where((k2 == 0) | (k2 == h), 1.0, 2.0)[None, :] / n
    li_re = pad_rows((f2.real[:h, :] * weight).T).T
    li_im = pad_rows((f2.imag[:h, :] * weight).T).T
    li_half = np.concatenate([li_re, li_im], axis=1)
    return dict(f1_fwd=f1_fwd, f1_inv=f1_inv, twre=tw.real, twim=tw.imag,
                lf_half=lf_half, lf_full=lf_full, li_half=li_half)


def _consts_dev(n2):
    c = _dft_consts(n2)
    return dict(
        f1_fwd=jnp.asarray(c["f1_fwd"], BF16), f1_inv=jnp.asarray(c["f1_inv"], BF16),
        twre=jnp.asarray(c["twre"], F32), twim=jnp.asarray(c["twim"], F32),
        lf_half=jnp.asarray(c["lf_half"], BF16), lf_full=jnp.asarray(c["lf_full"], BF16),
        li_half=jnp.asarray(c["li_half"], BF16))


def _dft_fwd(sig, lf, twre, twim, f1f, a_scr):
    ns, two_hr, _ = a_scr.shape
    hr = two_hr // 2
    for s in range(ns):
        a_scr[s] = jnp.dot(lf, sig(s).astype(BF16), preferred_element_type=F32)
    are = a_scr[:, 0:hr, :]
    aim = a_scr[:, hr:two_hr, :]
    bre = (are * twre - aim * twim).reshape(ns * hr, DFT_N1)
    bim = (are * twim + aim * twre).reshape(ns * hr, DFT_N1)
    bcat = jnp.concatenate([bre, bim], axis=1).astype(BF16)
    cc = jnp.dot(bcat, f1f, preferred_element_type=F32)
    return cc[:, 0:DFT_N1], cc[:, DFT_N1:2 * DFT_N1]


def _dft_inv_real(yre, yim, li, twre, twim, f1i, d_scr, emit):
    ns, two_hr, _ = d_scr.shape
    hr = two_hr // 2
    ycat = jnp.concatenate([yre, yim], axis=1).astype(BF16)
    bb = jnp.dot(ycat, f1i, preferred_element_type=F32)
    bre = bb[:, 0:DFT_N1].reshape(ns, hr, DFT_N1)
    bim = bb[:, DFT_N1:2 * DFT_N1].reshape(ns, hr, DFT_N1)
    d_scr[:, 0:hr, :] = bre * twre + bim * twim
    d_scr[:, hr:two_hr, :] = bim * twre - bre * twim
    for s in range(ns):
        emit(s, jnp.dot(li, d_scr[s].astype(BF16), preferred_element_type=F32))


def _taps_kernel(bands_ref, w1t_ref, w1c_ref, w1s_ref, b1_ref, w2_ref, b2_ref, w3_ref, b3_ref,
                 fq_ref, w4_ref, dl_ref, o_ref, h_ref, tm_ref, *, seq_len, tl):
    nt = pl.program_id(0)

    @pl.when(pl.program_id(1) == 0)
    def _():
        col = nt * tl + lax.broadcasted_iota(jnp.int32, (1, tl), 1)
        pos = jnp.where(col < seq_len, col, 2 * seq_len - col)
        posf = pos.astype(F32)
        t = posf * np.float32(1.0 / (seq_len - 1))
        ang = bands_ref[...] * (posf * np.float32(2.0 * math.pi / seq_len))
        fq = fq_ref[...]
        pre = (w1t_ref[...] * t
               + jnp.dot(w1c_ref[...], jnp.cos(ang), preferred_element_type=F32, precision=HIGHEST)
               - jnp.dot(w1s_ref[...], jnp.sin(ang), preferred_element_type=F32, precision=HIGHEST))
        h = jnp.sin(fq * (pre + b1_ref[...]))
        h = jnp.sin(fq * (jnp.dot(w2_ref[...], h, preferred_element_type=F32, precision=HIGHEST) + b2_ref[...]))
        h = jnp.sin(fq * (jnp.dot(w3_ref[...], h, preferred_element_type=F32, precision=HIGHEST) + b3_ref[...]))
        h_ref[...] = h.astype(BF16)
        tm_ref[0:1, :] = t
        tm_ref[1:2, :] = jnp.where(col == seq_len, 0.0, 1.0)

    taps = jnp.dot(w4_ref[...], h_ref[...], preferred_element_type=F32)
    window = jnp.exp(-(dl_ref[...] * tm_ref[0:1, :]))
    o_ref[...] = taps * window * tm_ref[1:2, :]


def _hyena_taps(seq_len, f_w1, f_b1, f_w2, f_b2, f_w3, f_b3, f_w4, f_freq, d_model, *, tl=2048, td=512):
    fh = f_w1.shape[1]
    n_ord = f_w4.shape[1] // (2 * d_model)
    tl = min(tl, seq_len)
    ndt = d_model // td
    nlt = seq_len // tl
    col = lambda v: v.reshape(-1, 1).astype(F32)
    bands = jnp.linspace(1e-4, HY_BANDS - 1, HY_BANDS, dtype=F32).reshape(-1, 1)
    w1 = f_w1.astype(F32).T
    w4t = f_w4.astype(F32).reshape(fh, n_ord, 2, d_model).transpose(1, 2, 3, 0).reshape(n_ord * 2 * d_model, fh)
    w4t = w4t.astype(BF16)
    max_decay = math.log(HY_TARGET) / HY_FAST_PCT
    min_decay = math.log(HY_TARGET) / HY_SLOW_PCT
    deltas = jnp.abs(jnp.linspace(min_decay, max_decay, d_model, dtype=F32))
    deltas = jnp.tile(deltas, n_ord).reshape(-1, 1)
    small = lambda a: pl.BlockSpec(a.shape, lambda n, j: (0, 0))
    args = [bands, w1[:, 0:1], w1[:, 1:1 + HY_BANDS], w1[:, 1 + HY_BANDS:1 + 2 * HY_BANDS], col(f_b1),
            f_w2.astype(F32).T, col(f_b2), f_w3.astype(F32).T, col(f_b3), col(f_freq)]
    return pl.pallas_call(
        functools.partial(_taps_kernel, seq_len=seq_len, tl=tl),
        grid=(2 * nlt, n_ord * ndt),
        in_specs=[small(a) for a in args] + [
            pl.BlockSpec((td, fh), lambda n, j: ((j // ndt * 2 + (n >= nlt).astype(jnp.int32)) * ndt + j % ndt, 0)),
            pl.BlockSpec((td, 1), lambda n, j: (j, 0))],
        out_specs=pl.BlockSpec((td, tl), lambda n, j: (j, n)),
        out_shape=jax.ShapeDtypeStruct((n_ord * d_model, 2 * seq_len), F32),
        scratch_shapes=[pltpu.VMEM((fh, tl), BF16), pltpu.VMEM((8, tl), F32)],
        compiler_params=_cparams(("parallel", "arbitrary")),
        name="hyena_taps",
    )(*args, w4t, deltas)


def _spectrum_kernel(x_ref, lf_ref, twre_ref, twim_ref, f1f_ref, re_ref, im_ref, a_scr):
    ct, hr, _ = re_ref.shape
    cre, cim = _dft_fwd(lambda s: x_ref[s], lf_ref[...], twre_ref[...][None], twim_ref[...][None],
                        f1f_ref[...], a_scr)
    re_ref[...] = cre.reshape(ct, hr, DFT_N1)
    im_ref[...] = cim.reshape(ct, hr, DFT_N1)


def _spectrum(taps, *, ct=32):
    c, n = taps.shape
    n2 = n // DFT_N1
    hr = _dft_rows(n2)
    cs = _consts_dev(n2)
    x = taps.reshape(c, n2, DFT_N1)
    blk = pl.BlockSpec((ct, hr, DFT_N1), lambda i: (i, 0, 0))
    full = lambda a: pl.BlockSpec(a.shape, lambda i: (0,) * a.ndim)
    consts = [cs["lf_full"], cs["twre"], cs["twim"], cs["f1_fwd"]]
    return pl.pallas_call(
        _spectrum_kernel,
        grid=(c // ct,),
        in_specs=[pl.BlockSpec((ct, n2, DFT_N1), lambda i: (i, 0, 0))] + [full(a) for a in consts],
        out_specs=[blk, blk],
        out_shape=[jax.ShapeDtypeStruct((c, hr, DFT_N1), F32)] * 2,
        scratch_shapes=[pltpu.VMEM((ct, 2 * hr, DFT_N1), F32)],
        compiler_params=_cparams(("parallel",)),
        name="hyena_spectrum",
    )(x, *consts)


def _fftconv_kernel(x_ref, gate_ref, skip_ref,
                    kpre_ref, kpim_ref, lfp_ref, lip_ref, twpre_ref, twpim_ref,
                    ksre_ref, ksim_ref, lfs_ref, lis_ref, twsre_ref, twsim_ref,
                    f1f_ref, f1i_ref, o_ref, ap_scr, dp_scr, as_scr, ds_scr, *, hp, hs, nb):
    ct = x_ref.shape[0]
    f1f = f1f_ref[...]
    f1i = f1i_ref[...]

    def run(rows, n_sig, kre, kim, lf, li, twre, twim, a_scr, d_scr):
        def sig(s):
            c, r0, h = rows(s)
            return x_ref[c, r0:r0 + h, :]

        cre, cim = _dft_fwd(sig, lf, twre, twim, f1f, a_scr)
        yre = cre * kre - cim * kim
        yim = cre * kim + cim * kre

        def emit(s, y):
            c, r0, h = rows(s)
            xs = x_ref[c, r0:r0 + h, :]
            o_ref[c, r0:r0 + h, :] = gate_ref[c, r0:r0 + h, :] * (y + skip_ref[c] * xs)

        _dft_inv_real(yre, yim, li, twre, twim, f1i, d_scr, emit)

    hrp = kpre_ref.shape[1]
    run(lambda s: (s, 0, hp), ct,
        kpre_ref[...].reshape(ct * hrp, DFT_N1), kpim_ref[...].reshape(ct * hrp, DFT_N1),
        lfp_ref[...], lip_ref[...], twpre_ref[...][None], twpim_ref[...][None], ap_scr, dp_scr)
    hrs = ksre_ref.shape[1]
    rep = lambda k: jnp.broadcast_to(k[:, None], (ct, nb, hrs, DFT_N1)).reshape(ct * nb * hrs, DFT_N1)
    run(lambda s: (s // nb, hp + (s % nb) * hs, hs), ct * nb,
        rep(ksre_ref[...]), rep(ksim_ref[...]),
        lfs_ref[...], lis_ref[...], twsre_ref[...][None], twsim_ref[...][None], as_scr, ds_scr)


def _fftconv(x, x_off, gate, gate_off, skip, kp, kp_off, ks, ks_off, *, d_model, hp, hs, nb, ct=8):
    r = hp + nb * hs
    cp, cs_ = _consts_dev(2 * hp), _consts_dev(2 * hs)
    hrp, hrs = _dft_rows(2 * hp), _dft_rows(2 * hs)
    cb = lambda off: (lambda i: (off // ct + i, 0, 0))
    full = lambda a: pl.BlockSpec(a.shape, lambda i: (0,) * a.ndim)
    kspec = lambda rows, off: pl.BlockSpec((ct, rows, DFT_N1), cb(off))
    skip3 = jnp.broadcast_to(skip.astype(F32).reshape(d_model, 1, 1), (d_model, 1, DFT_N1))
    pc = [cp["lf_half"], cp["li_half"], cp["twre"], cp["twim"]]
    sc = [cs_["lf_half"], cs_["li_half"], cs_["twre"], cs_["twim"]]
    return pl.pallas_call(
        functools.partial(_fftconv_kernel, hp=hp, hs=hs, nb=nb),
        grid=(d_model // ct,),
        in_specs=[pl.BlockSpec((ct, r, DFT_N1), cb(x_off)), pl.BlockSpec((ct, r, DFT_N1), cb(gate_off)),
                  pl.BlockSpec((ct, 1, DFT_N1), cb(0)),
                  kspec(hrp, kp_off), kspec(hrp, kp_off)] + [full(a) for a in pc]
                 + [kspec(hrs, ks_off), kspec(hrs, ks_off)] + [full(a) for a in sc]
                 + [full(cp["f1_fwd"]), full(cp["f1_inv"])],
        out_specs=pl.BlockSpec((ct, r, DFT_N1), cb(0)),
        out_shape=jax.ShapeDtypeStruct((d_model, r, DFT_N1), F32),
        scratch_shapes=[pltpu.VMEM((ct, 2 * hrp, DFT_N1), F32), pltpu.VMEM((ct, 2 * hrp, DFT_N1), F32),
                        pltpu.VMEM((ct * nb, 2 * hrs, DFT_N1), F32), pltpu.VMEM((ct * nb, 2 * hrs, DFT_N1), F32)],
        compiler_params=_cparams(("parallel",)),
        name="hyena_fftconv",
    )(x, gate, skip3, kp[0], kp[1], *pc, ks[0], ks[1], *sc, cp["f1_fwd"], cp["f1_inv"])


def _outproj_t_kernel(zt_ref, w_ref, *refs, part_tiles):
    res_refs, o_ref = refs[:-1], refs[-1]
    i = pl.program_id(0)
    z = zt_ref[...].astype(BF16)
    y = lax.dot_general(z, w_ref[...], (((0,), (0,)), ((), ())), preferred_element_type=F32)
    first = 0
    for res_ref, tiles in zip(res_refs, part_tiles):
        @pl.when((i >= first) & (i < first + tiles))
        def _(res_ref=res_ref):
            o_ref[...] = res_ref[...] + y

        first += tiles


def _outproj_t(zt, w, res, *, tm=512):
    parts = res if isinstance(res, tuple) else (res,)
    k, m = zt.shape
    n = w.shape[1]
    tm = min(tm, m)
    part_tiles = tuple(p.shape[0] // tm for p in parts)
    assert all(p.shape[0] % tm == 0 for p in parts) and sum(part_tiles) * tm == m
    res_specs, first = [], 0
    for tiles in part_tiles:
        res_specs.append(pl.BlockSpec((tm, n), lambda i, first=first, tiles=tiles: (jnp.clip(i - first, 0, tiles - 1), 0)))
        first += tiles
    return pl.pallas_call(
        functools.partial(_outproj_t_kernel, part_tiles=part_tiles),
        grid=(m // tm,),
        in_specs=[pl.BlockSpec((k, tm), lambda i: (0, i)),
                  pl.BlockSpec((k, n), lambda i: (0, 0))] + res_specs,
        out_specs=pl.BlockSpec((tm, n), lambda i: (i, 0)),
        out_shape=jax.ShapeDtypeStruct((m, n), F32),
        compiler_params=_cparams(("parallel",)),
        name="hyena_outproj",
    )(zt, w.astype(BF16), *parts)


def _softplus(x):
    return jnp.maximum(x, 0.0) + jnp.log1p(jnp.exp(-jnp.abs(x)))


def _gdn_kernel(qf_ref, kf_ref, vf_ref, baf_ref, batf_ref, qb_ref, kb_ref, vb_ref, bab_ref, batb_ref,
                alog_r_ref, dtb_r_ref, alog_c_ref, dtb_c_ref, of_ref, ob_ref, sf_ref, sb_ref, *,
                n_steps, f_resets, b_resets, n_pairs, n_sub):
    hp = pl.program_id(0)
    step = pl.program_id(1)
    t = GDN_TILE
    c = GDN_CHUNK
    nc = t // c
    dh = GDN_DH

    @pl.when(_in_set(step, f_resets))
    def _():
        sf_ref[...] = jnp.zeros_like(sf_ref)

    @pl.when(_in_set(n_steps - 1 - step, b_resets))
    def _():
        sb_ref[...] = jnp.zeros_like(sb_ref)

    ri = lax.broadcasted_iota(jnp.int32, (t, t), 0)
    ci = lax.broadcasted_iota(jnp.int32, (t, t), 1)
    same = (ri // c) == (ci // c)
    rw = lax.broadcasted_iota(jnp.int32, (c, t), 0)
    lw = lax.broadcasted_iota(jnp.int32, (c, t), 1)
    lblk = lw // c
    lsub = lw % c
    rsub = lax.broadcasted_iota(jnp.int32, (t, LANES), 0) % c
    lane = lax.broadcasted_iota(jnp.int32, (t, LANES), 1)
    lrow = lax.broadcasted_iota(jnp.int32, (8, t), 1) % c

    def diag_to_wide(full):
        parts = [jnp.where(lblk == j, full[j * c:(j + 1) * c, :], 0.0) for j in range(nc)]
        return functools.reduce(lambda x, y: x + y, parts)

    def wide_to_diag16(wide):
        return jnp.where(same, jnp.concatenate([wide] * nc, axis=0), 0.0).astype(BF16)

    def l2n(x):
        return x * lax.rsqrt(jnp.sum(x * x, axis=-1, keepdims=True) + RMS_EPS)

    def mm(a, b):
        return jnp.dot(a, b, preferred_element_type=F32)

    dirs = [(0, qf_ref, kf_ref, vf_ref, baf_ref, batf_ref), (1, qb_ref, kb_ref, vb_ref, bab_ref, batb_ref)]

    def stage1_dir(d, r0):
        _, _, _, v_ref, ba_ref, bat_ref = dirs[d]
        back = d == 1
        ba = ba_ref[r0:r0 + t, :]
        g_all = -jnp.exp(alog_r_ref[...]) * _softplus(ba + dtb_r_ref[...])
        sh = 1
        while sh < c:
            if back:
                g_all = g_all + jnp.where(rsub < c - sh, pltpu.roll(g_all, t - sh, axis=0), 0.0)
            else:
                g_all = g_all + jnp.where(rsub >= sh, pltpu.roll(g_all, sh, axis=0), 0.0)
            sh *= 2
        return dict(back=back, incl=(lsub >= rw) if back else (lsub <= rw),
                    strict=(lsub > rw) if back else (lsub < rw),
                    beta_all=jax.nn.sigmoid(ba), g_all=g_all, bat_ref=bat_ref, v_ref=v_ref, r0=r0)

    def stage1_pair(d, pp, r0):
        _, q_ref, k_ref, _, _, _ = dirs[d]
        q = l2n(q_ref[r0:r0 + t, pp * dh:(pp + 1) * dh]) * np.float32(GDN_DH ** -0.5)
        k = l2n(k_ref[r0:r0 + t, pp * dh:(pp + 1) * dh])
        k16 = k.astype(BF16)
        qk_kk = lax.dot_general(jnp.concatenate([q.astype(BF16), k16], axis=0), k16, (((1,), (1,)), ((), ())),
                                preferred_element_type=F32)
        return dict(q=q, k=k, qk=diag_to_wide(qk_kk[0:t]), kk=diag_to_wide(qk_kk[t:2 * t]))

    def stage2(d, pd, pr, ph):
        back = pd["back"]
        head = 2 * n_pairs * hp + ph
        jb = d * GDN_NV + head
        ja = 2 * GDN_NV + d * GDN_NV + head
        beta_c = jnp.sum(jnp.where(lane == jb, pd["beta_all"], 0.0), axis=1, keepdims=True)
        gc_c = jnp.sum(jnp.where(lane == ja, pd["g_all"], 0.0), axis=1, keepdims=True)
        a_row = pd["bat_ref"][pl.ds(ja, 1), pd["r0"]:pd["r0"] + t]
        g_row = -jnp.exp(alog_c_ref[pl.ds(ja, 1), :]) * _softplus(a_row + dtb_c_ref[pl.ds(ja, 1), :])
        g_row = jnp.broadcast_to(g_row, (8, t))
        sh = 1
        while sh < c:
            if back:
                g_row = g_row + jnp.where(lrow < c - sh, pltpu.roll(g_row, t - sh, axis=1), 0.0)
            else:
                g_row = g_row + jnp.where(lrow >= sh, pltpu.roll(g_row, sh, axis=1), 0.0)
            sh *= 2
        gc_r = g_row[0:1, :]
        incl = pd["incl"]
        decay = jnp.where(incl, jnp.exp(jnp.where(incl, diag_to_wide(gc_c) - gc_r, 0.0)), 0.0)
        p = jnp.where(pd["strict"], -(diag_to_wide(beta_c) * pr["kk"] * decay), 0.0)
        return dict(pd=pd, pr=pr, ph=ph, beta_c=beta_c, gc_c=gc_c, decay=decay, p=p)

    def stage3(group):
        for v in group:
            v["nn"] = v["p"]
            v["pm"] = mm(v["p"].astype(BF16), wide_to_diag16(v["p"]))
            yield
        m = 2
        while 2 * m < c:
            for v in group:
                both = mm(jnp.concatenate([v["pm"], v["nn"]], axis=0).astype(BF16), wide_to_diag16(v["pm"]))
                v["nn"] = v["nn"] + v["pm"] + both[c:2 * c]
                v["pm"] = both[0:c]
                yield
            m *= 2
        for v in group:
            v["nn"] = v["nn"] + v["pm"] + mm(v["nn"].astype(BF16), wide_to_diag16(v["pm"]))
            yield

    def stage4(group):
        for v in group:
            pd, pr, ph, beta_c, gc_c = v["pd"], v["pr"], v["ph"], v["beta_c"], v["gc_c"]
            vv = pd["v_ref"][pd["r0"]:pd["r0"] + t, ph * dh:(ph + 1) * dh]
            rhs = jnp.concatenate([vv * beta_c, pr["k"] * (beta_c * jnp.exp(gc_c))], axis=1)
            v["uw16"] = (rhs + mm(wide_to_diag16(v["nn"]), rhs.astype(BF16))).astype(BF16)
        for v in group:
            pd, pr, gc_c = v["pd"], v["pr"], v["gc_c"]
            qq = mm(wide_to_diag16(jnp.where(pd["incl"], pr["qk"] * v["decay"], 0.0)), v["uw16"])
            gc3 = gc_c.reshape(nc, c, 1)
            gl3 = gc3[:, 0:1, :] if pd["back"] else gc3[:, c - 1:c, :]
            kg16 = (pr["k"] * jnp.exp(jnp.broadcast_to(gl3, (nc, c, 1)).reshape(t, 1) - gc_c)).astype(BF16)
            v["qp16"] = (pr["q"] * jnp.exp(gc_c) - qq[:, dh:2 * dh]).astype(BF16)
            v["qq"] = qq
            v["gl3"] = gl3
            v["rp"] = [lax.dot_general(kg16[n * c:(n + 1) * c], v["uw16"][n * c:(n + 1) * c],
                                       (((0,), (0,)), ((), ())), preferred_element_type=F32)
                       for n in range(nc)]

    units = [(sub, d) for sub in range(n_sub) for d in range(2)]
    row0 = lambda sub, d: (n_sub - 1 - sub if d == 1 else sub) * t
    var = {}
    groups = {u: [] for u in units}

    def setup(u):
        sub, d = u
        pd = stage1_dir(d, row0(sub, d))
        yield
        for pp in range(n_pairs):
            pr = stage1_pair(d, pp, row0(sub, d))
            yield
            for ph in (2 * pp, 2 * pp + 1):
                var[(sub, d, ph)] = stage2(d, pd, pr, ph)
                groups[u].append(var[(sub, d, ph)])
                yield

    def run(main, filler=None, every=1):
        for i, _ in enumerate(main):
            if filler is not None and i % every == every - 1:
                next(filler, None)
        if filler is not None:
            for _ in filler:
                pass

    zero = jnp.zeros((dh, dh), F32)
    s_refs = [sf_ref, sb_ref]
    outs = [of_ref, ob_ref]
    state = [[s_refs[d][ph] for ph in range(2 * n_pairs)] for d in range(2)]

    def stage5(sub):
        for i in range(nc):
            for d in range(2):
                n = nc - 1 - i if d == 1 else i
                sl = slice(n * c, (n + 1) * c)
                so = slice(row0(sub, d) + n * c, row0(sub, d) + (n + 1) * c)
                for pp in range(n_pairs):
                    v0, v1 = var[(sub, d, 2 * pp)], var[(sub, d, 2 * pp + 1)]
                    s0, s1 = state[d][2 * pp], state[d][2 * pp + 1]
                    s_d = jnp.concatenate([jnp.concatenate([s0, zero], axis=1),
                                           jnp.concatenate([zero, s1], axis=1)], axis=0).astype(BF16)
                    lhs = jnp.concatenate([
                        jnp.concatenate([v0["rp"][n][:, dh:2 * dh], v1["rp"][n][:, dh:2 * dh]], axis=1).astype(BF16),
                        jnp.concatenate([v0["qp16"][sl], v1["qp16"][sl]], axis=1)], axis=0)
                    z = mm(lhs, s_d)
                    c0 = 2 * pp * dh
                    outs[d][so, c0:c0 + dh] = z[dh:dh + c, 0:dh] + v0["qq"][sl, 0:dh]
                    outs[d][so, c0 + dh:c0 + 2 * dh] = z[dh:dh + c, dh:2 * dh] + v1["qq"][sl, 0:dh]
                    state[d][2 * pp] = jnp.exp(v0["gl3"][n]) * s0 - z[0:dh, 0:dh] + v0["rp"][n][:, 0:dh]
                    state[d][2 * pp + 1] = jnp.exp(v1["gl3"][n]) * s1 - z[0:dh, dh:2 * dh] + v1["rp"][n][:, 0:dh]

    n_setup = 1 + 3 * n_pairs
    n_doubling = 2 * n_pairs * (c.bit_length() - 1)
    run(setup(units[0]))
    for i, u in enumerate(units):
        filler = setup(units[i + 1]) if i + 1 < len(units) else None
        run(stage3(groups[u]), filler, every=max(1, n_doubling // n_setup))
        if u[1] == 1:
            stage4(groups[(u[0], 0)])
            stage4(groups[(u[0], 1)])
            stage5(u[0])
    for d in range(2):
        for ph in range(2 * n_pairs):
            s_refs[d][ph] = state[d][ph]


def _gdn_scan(qkv, ba, a_log, dt_bias, *, starts, ends, n_pairs=4, n_sub=2):
    m = qkv.shape[0]
    t = n_sub * GDN_TILE
    assert all(s % t == 0 for s in starts) and all(e % t == 0 for e in ends)
    n_tiles = m // t
    bat = ba.T
    pad = lambda a: jnp.pad(a.astype(F32).reshape(-1), (2 * GDN_NV, LANES - 4 * GDN_NV))
    alog_r = pad(a_log).reshape(1, LANES)
    dtb_r = pad(dt_bias).reshape(1, LANES)
    alog_c = pad(a_log).reshape(LANES, 1)
    dtb_c = pad(dt_bias).reshape(LANES, 1)
    f_resets = tuple(s // t for s in starts)
    b_resets = tuple(e // t - 1 for e in ends)
    fwd = lambda h, s: s
    bwd = lambda h, s: n_tiles - 1 - s

    kw = n_pairs * GDN_DH
    key_blocks = GDN_NK // n_pairs

    def specs(tile):
        return [pl.BlockSpec((t, kw), lambda h, s: (tile(h, s), h)),
                pl.BlockSpec((t, kw), lambda h, s: (tile(h, s), key_blocks + h)),
                pl.BlockSpec((t, 2 * kw), lambda h, s: (tile(h, s), key_blocks + h)),
                pl.BlockSpec((t, LANES), lambda h, s: (tile(h, s), 0)),
                pl.BlockSpec((LANES, t), lambda h, s: (0, tile(h, s)))]

    small = lambda a: pl.BlockSpec(a.shape, lambda h, s: (0, 0))
    out = jax.ShapeDtypeStruct((m, GDN_NV * GDN_DH), F32)
    return pl.pallas_call(
        functools.partial(_gdn_kernel, n_steps=n_tiles, f_resets=f_resets, b_resets=b_resets, n_pairs=n_pairs,
                          n_sub=n_sub),
        grid=(key_blocks, n_tiles),
        in_specs=specs(fwd) + specs(bwd) + [small(alog_r), small(dtb_r), small(alog_c), small(dtb_c)],
        out_specs=[pl.BlockSpec((t, 2 * kw), lambda h, s: (s, h)),
                   pl.BlockSpec((t, 2 * kw), lambda h, s: (n_tiles - 1 - s, h))],
        out_shape=[out, out],
        scratch_shapes=[pltpu.VMEM((2 * n_pairs, GDN_DH, GDN_DH), F32),
                        pltpu.VMEM((2 * n_pairs, GDN_DH, GDN_DH), F32)],
        compiler_params=_cparams(("parallel", "arbitrary")),
        name="gdn_scan",
    )(qkv, qkv, qkv, ba, bat, qkv, qkv, qkv, ba, bat, alog_r, dtb_r, alog_c, dtb_c)


def _gdn_out_kernel(of_ref, ob_ref, z_ref, nw_ref, w_ref, res_ref, o_ref, y_ref):
    nw = nw_ref[...]
    for h in range(GDN_NV):
        sl = slice(h * GDN_DH, (h + 1) * GDN_DH)
        o = of_ref[:, sl] + ob_ref[:, sl]
        o = o * lax.rsqrt(jnp.mean(o * o, axis=-1, keepdims=True) + RMS_EPS) * nw
        z = z_ref[:, sl]
        y_ref[:, sl] = (o * (z * jax.nn.sigmoid(z))).astype(BF16)
    o_ref[...] = res_ref[...] + jnp.dot(y_ref[...], w_ref[...], preferred_element_type=F32)


def _gdn_out(o_f, o_b, z, norm_w, w, res, *, tm=512):
    m, kv = o_f.shape
    n = w.shape[1]
    tm = min(tm, m)
    row = lambda width: pl.BlockSpec((tm, width), lambda i: (i, 0))
    return pl.pallas_call(
        _gdn_out_kernel,
        grid=(m // tm,),
        in_specs=[row(kv), row(kv), row(kv),
                  pl.BlockSpec((1, GDN_DH), lambda i: (0, 0)),
                  pl.BlockSpec((kv, n), lambda i: (0, 0)),
                  row(n)],
        out_specs=row(n),
        out_shape=jax.ShapeDtypeStruct((m, n), F32),
        scratch_shapes=[pltpu.VMEM((tm, kv), BF16)],
        compiler_params=_cparams(("parallel",)),
        name="gdn_out",
    )(o_f, o_b, z, norm_w.astype(F32).reshape(1, GDN_DH), w.astype(BF16), res)


def _swiglu_tile(h, wg_ref, wu_ref, wd_ref, lead=()):
    tf = wg_ref.shape[-1]
    sub = 2 * LANES if tf % (2 * LANES) == 0 else tf
    y = None
    prev = None

    def finish(a, u, c0):
        act = (a * jax.nn.sigmoid(a) * u).astype(BF16)
        return jnp.dot(act, wd_ref[lead + (slice(c0, c0 + sub), slice(None))], preferred_element_type=F32)

    for c0 in range(0, tf, sub):
        cols = lead + (slice(None), slice(c0, c0 + sub))
        a = jnp.dot(h, wg_ref[cols], preferred_element_type=F32)
        u = jnp.dot(h, wu_ref[cols], preferred_element_type=F32)
        if prev is not None:
            part = finish(*prev)
            y = part if y is None else y + part
        prev = (a, u, c0)
    part = finish(*prev)
    return part if y is None else y + part


def _ffn_kernel(x_ref, g_ref, wg_ref, wu_ref, wd_ref, gf_ref, o_ref, h_ref, acc_ref, *, final_norm):
    f = pl.program_id(1)

    @pl.when(f == 0)
    def _():
        h_ref[...] = _norm_rows(x_ref[...], g_ref[...]).astype(BF16)
        acc_ref[...] = jnp.zeros_like(acc_ref)

    acc_ref[...] += _swiglu_tile(h_ref[...], wg_ref, wu_ref, wd_ref)

    @pl.when(f == pl.num_programs(1) - 1)
    def _():
        out = x_ref[...] + acc_ref[...]
        if final_norm:
            out = _norm_rows(out, gf_ref[...])
        o_ref[...] = out


def _ffn(x, gain, w_gate, w_up, w_down, *, final_gain=None, tm=1024, tf=1792):
    m, d = x.shape
    ff = w_gate.shape[1]
    tm = min(tm, m)
    tf = min(tf, ff)
    final_norm = final_gain is not None
    gf = (final_gain if final_norm else gain).astype(F32).reshape(1, d)
    return pl.pallas_call(
        functools.partial(_ffn_kernel, final_norm=final_norm),
        grid=(m // tm, ff // tf),
        in_specs=[pl.BlockSpec((tm, d), lambda i, f: (i, 0)),
                  pl.BlockSpec((1, d), lambda i, f: (0, 0)),
                  pl.BlockSpec((d, tf), lambda i, f: (0, f)),
                  pl.BlockSpec((d, tf), lambda i, f: (0, f)),
                  pl.BlockSpec((tf, d), lambda i, f: (f, 0)),
                  pl.BlockSpec((1, d), lambda i, f: (0, 0))],
        out_specs=pl.BlockSpec((tm, d), lambda i, f: (i, 0)),
        out_shape=jax.ShapeDtypeStruct((m, d), F32),
        scratch_shapes=[pltpu.VMEM((tm, d), BF16), pltpu.VMEM((tm, d), F32)],
        compiler_params=_cparams(("parallel", "arbitrary")),
        name="ffn_mixer",
    )(x, gain.astype(F32).reshape(1, d), w_gate.astype(BF16), w_up.astype(BF16), w_down.astype(BF16), gf)


def _moe_kernel(x_ref, g_ref, r_ref, rb_ref, wg_ref, wu_ref, wd_ref, gf_ref, o_ref,
                h_ref, acc_ref, gates_ref, posc_ref, posr_ref, hx_ref, y_ref, nblk_ref, *,
                n_exp, final_norm, rb):
    e = pl.program_id(1)
    f = pl.program_id(2)
    tm = x_ref.shape[0]
    lane = lax.broadcasted_iota(jnp.int32, (tm, LANES), 1)

    @pl.when((e == 0) & (f == 0))
    def _():
        h = _norm_rows(x_ref[...], g_ref[...])
        h_ref[...] = h.astype(BF16)
        acc_ref[...] = jnp.zeros_like(acc_ref)
        h_hi = h_ref[...]
        h_lo = (h - h_hi.astype(F32)).astype(BF16)
        r_hi = r_ref[0]
        logits = (jnp.dot(h_hi, r_hi, preferred_element_type=F32)
                  + jnp.dot(h_hi, r_ref[1], preferred_element_type=F32)
                  + jnp.dot(h_lo, r_hi, preferred_element_type=F32)) + rb_ref[...]
        logits = jnp.where(lane < n_exp, logits, -jnp.inf)
        m1 = jnp.max(logits, axis=1, keepdims=True)
        i1 = jnp.min(jnp.where(logits == m1, lane, LANES), axis=1, keepdims=True)
        rest = jnp.where(lane == i1, -jnp.inf, logits)
        m2 = jnp.max(rest, axis=1, keepdims=True)
        i2 = jnp.min(jnp.where(rest == m2, lane, LANES), axis=1, keepdims=True)
        e2 = jnp.exp(m2 - m1)
        w1 = 1.0 / (1.0 + e2)
        gates_ref[...] = jnp.where(lane == i1, w1, 0.0) + jnp.where(lane == i2, e2 * w1, 0.0)
        chosen = jnp.where((lane == i1) | (lane == i2), 1.0, 0.0)
        ri = lax.broadcasted_iota(jnp.int32, (tm, tm), 0)
        ci = lax.broadcasted_iota(jnp.int32, (tm, tm), 1)
        before = jnp.where(ci < ri, 1.0, 0.0).astype(BF16)
        rank = jnp.dot(before, chosen.astype(BF16), preferred_element_type=F32)
        posc = jnp.where(chosen > 0.5, rank, -1.0)
        posc_ref[...] = posc
        posr_ref[...] = posc.T
        for ee in range(n_exp):
            cnt = jnp.sum(chosen[:, ee:ee + 1]).astype(jnp.int32)
            nblk_ref[ee] = (cnt + rb // 2 - 1) // (rb // 2)

    n_half = nblk_ref[e]
    nblk = (n_half + 1) // 2

    @pl.when(f == 0)
    def _():
        pos_row = posr_ref[pl.ds(e, 1), :]
        h = h_ref[...]

        def gather(b, carry):
            r0 = pl.multiple_of(b * 2 * rb, 2 * rb)
            rows = (lax.broadcasted_iota(jnp.int32, (2 * rb, 1), 0) + r0).astype(F32)
            onehot = jnp.where(pos_row == rows, 1.0, 0.0).astype(BF16)
            hx_ref[pl.ds(r0, 2 * rb), :] = jnp.dot(onehot, h, preferred_element_type=F32).astype(BF16)
            y_ref[pl.ds(r0, 2 * rb), :] = jnp.zeros((2 * rb, y_ref.shape[1]), F32)
            return carry

        lax.fori_loop(0, (nblk + 1) // 2, gather, 0)

    def expert(r0, rows):
        y_ref[pl.ds(r0, rows), :] += _swiglu_tile(hx_ref[pl.ds(r0, rows), :], wg_ref, wu_ref, wd_ref, lead=(0,))

    def expert_pair(b, carry):
        expert(pl.multiple_of(b * 2 * rb, 2 * rb), 2 * rb)
        return carry

    n_pair = (n_half + 1) // 4
    left = n_half % 4
    lax.fori_loop(0, n_pair, expert_pair, 0)

    @pl.when(left == 1)
    def _():
        expert(pl.multiple_of(n_pair * 2 * rb, 2 * rb), rb // 2)

    @pl.when(left == 2)
    def _():
        expert(pl.multiple_of(n_pair * 2 * rb, 2 * rb), rb)

    @pl.when(f == pl.num_programs(2) - 1)
    def _():
        pos_col = jnp.sum(jnp.where(lane == e, posc_ref[...], 0.0), axis=1, keepdims=True)
        gate = jnp.sum(jnp.where(lane == e, gates_ref[...], 0.0), axis=1, keepdims=True)

        def scatter(b, carry):
            r0 = pl.multiple_of(b * 2 * rb, 2 * rb)
            cols = (lax.broadcasted_iota(jnp.int32, (1, 2 * rb), 1) + r0).astype(F32)
            onehot = jnp.where(pos_col == cols, 1.0, 0.0).astype(BF16)
            back = jnp.dot(onehot, y_ref[pl.ds(r0, 2 * rb), :].astype(BF16), preferred_element_type=F32)
            acc_ref[...] += gate * back
            return carry

        lax.fori_loop(0, (nblk + 1) // 2, scatter, 0)

    @pl.when((e == n_exp - 1) & (f == pl.num_programs(2) - 1))
    def _():
        out = x_ref[...] + acc_ref[...]
        if final_norm:
            out = _norm_rows(out, gf_ref[...])
        o_ref[...] = out


def _moe(x, gain, router, router_bias, w_gate, w_up, w_down, *, final_gain=None, tm=1024, tf=1792, rb=128):
    m, d = x.shape
    n_exp, _, ff = w_gate.shape
    tm = min(tm, m)
    tf = tf if ff % tf == 0 else min(512, ff)
    assert (tm // rb) % 2 == 0
    r = jnp.pad(router.astype(F32), ((0, 0), (0, LANES - n_exp)))
    r_hi = r.astype(BF16)
    r = jnp.stack([r_hi, (r - r_hi.astype(F32)).astype(BF16)])
    rbias = jnp.pad(router_bias.astype(F32).reshape(1, n_exp), ((0, 0), (0, LANES - n_exp)))
    final_norm = final_gain is not None
    gf = (final_gain if final_norm else gain).astype(F32).reshape(1, d)
    return pl.pallas_call(
        functools.partial(_moe_kernel, n_exp=n_exp, final_norm=final_norm, rb=rb),
        grid=(m // tm, n_exp, ff // tf),
        in_specs=[pl.BlockSpec((tm, d), lambda i, e, f: (i, 0), pipeline_mode=pl.Buffered(1)),
                  pl.BlockSpec((1, d), lambda i, e, f: (0, 0)),
                  pl.BlockSpec((2, d, LANES), lambda i, e, f: (0, 0, 0)),
                  pl.BlockSpec((1, LANES), lambda i, e, f: (0, 0)),
                  pl.BlockSpec((1, d, tf), lambda i, e, f: (e, 0, f)),
                  pl.BlockSpec((1, d, tf), lambda i, e, f: (e, 0, f)),
                  pl.BlockSpec((1, tf, d), lambda i, e, f: (e, f, 0)),
                  pl.BlockSpec((1, d), lambda i, e, f: (0, 0))],
        out_specs=pl.BlockSpec((tm, d), lambda i, e, f: (i, 0)),
        out_shape=jax.ShapeDtypeStruct((m, d), F32),
        scratch_shapes=[pltpu.VMEM((tm, d), BF16), pltpu.VMEM((tm, d), F32), pltpu.VMEM((tm, LANES), F32),
                        pltpu.VMEM((tm, LANES), F32), pltpu.VMEM((LANES, tm), F32),
                        pltpu.VMEM((tm, d), BF16), pltpu.VMEM((tm, d), F32), pltpu.SMEM((n_exp,), jnp.int32)],
        compiler_params=_cparams(("parallel", "arbitrary", "arbitrary")),
        name="moe_mixer",
    )(x, gain.astype(F32).reshape(1, d), r, rbias, w_gate.astype(BF16), w_up.astype(BF16),
      w_down.astype(BF16), gf)


def _hyena_layer(x, gain, lp, ls, nb_p, nb_s, w_in, conv_w, f_w1, f_b1, f_w2, f_b2, f_w3, f_b3, f_w4,
                 f_freq, skip, w_out, *, starts, ends):
    parts = x if isinstance(x, tuple) else (x,)
    m, d = sum(p.shape[0] for p in parts), parts[0].shape[1]
    assert nb_p == 1 and lp % (2 * DFT_N1) == 0 and ls % (2 * DFT_N1) == 0
    hp, hs = lp // DFT_N1, ls // DFT_N1
    ut = _proj(x, gain, w_in, conv_w, transpose_out=True, starts=starts, ends=ends)
    u3 = ut.reshape(3 * d, m // DFT_N1, DFT_N1)
    filt = (f_w1, f_b1, f_w2, f_b2, f_w3, f_b3, f_w4, f_freq)
    kp = _spectrum(_hyena_taps(lp, *filt, d))
    ks = _spectrum(_hyena_taps(ls, *filt, d))
    conv = functools.partial(_fftconv, d_model=d, hp=hp, hs=hs, nb=nb_s)
    z1 = conv(u3, 0, u3, d, skip[0], kp, 0, ks, 0)
    z2 = conv(z1, 0, u3, 2 * d, skip[1], kp, d, ks, d)
    return _outproj_t(z2.reshape(d, m), w_out, x)


def _gdn_layer(x, gain, w_in, conv_w, a_log, dt_bias, norm_w, w_out, *, starts, ends):
    key, val = GDN_NK * GDN_DH, GDN_NV * GDN_DH
    cd = 2 * key + val
    qkv = _proj(x, gain, w_in[:, :cd], conv_w, act="silu", starts=starts, ends=ends)
    z = _proj(x, gain, w_in[:, cd:cd + val])
    w_ba = jnp.pad(w_in[:, cd + val:], ((0, 0), (0, LANES - 4 * GDN_NV)))
    ba = _proj(x, gain, w_ba)
    o_f, o_b = _gdn_scan(qkv, ba, a_log, dt_bias, starts=starts, ends=ends)
    return _gdn_out(o_f, o_b, z, norm_w, w_out, x)


def kernel(x_prompt, x_sample, norm_mix, norm_ffn, norm_final, hy_w_in, hy_conv, hy_f_w1, hy_f_b1, hy_f_w2, hy_f_b2, hy_f_w3, hy_f_b3, hy_f_w4, hy_f_freq, hy_skip, hy_w_out, gdn_w_in, gdn_conv, gdn_a_log, gdn_dt_bias, gdn_norm, gdn_w_out, ffn_w_gate, ffn_w_up, ffn_w_down, moe_router, moe_router_bias, moe_w_gate, moe_w_up, moe_w_down):
    bp, lp, d = x_prompt.shape
    bs, ls, _ = x_sample.shape
    x = (x_prompt.reshape(bp * lp, d), x_sample.reshape(bs * ls, d))
    starts = tuple(b * lp for b in range(bp)) + tuple(bp * lp + b * ls for b in range(bs))
    ends = tuple(s + lp for s in starts[:bp]) + tuple(s + ls for s in starts[bp:])
    depth = norm_mix.shape[0]
    for i in range(depth):
        j = i // 2
        last = i == depth - 1
        if i % 2 == 0:
            x = _hyena_layer(x, norm_mix[i], lp, ls, bp, bs, hy_w_in[j], hy_conv[j], hy_f_w1[j], hy_f_b1[j],
                             hy_f_w2[j], hy_f_b2[j], hy_f_w3[j], hy_f_b3[j], hy_f_w4[j], hy_f_freq[j],
                             hy_skip[j], hy_w_out[j], starts=starts, ends=ends)
            x = _ffn(x, norm_ffn[i], ffn_w_gate[j], ffn_w_up[j], ffn_w_down[j],
                     final_gain=norm_final if last else None)
        else:
            if isinstance(x, tuple):
                x = jnp.concatenate(x, axis=0)
            x = _gdn_layer(x, norm_mix[i], gdn_w_in[j], gdn_conv[j], gdn_a_log[j], gdn_dt_bias[j],
                           gdn_norm[j], gdn_w_out[j], starts=starts, ends=ends)
            x = _moe(x, norm_ffn[i], moe_router[j], moe_router_bias[j], moe_w_gate[j], moe_w_up[j],
                     moe_w_down[j], final_gain=norm_final if last else None)
    y_prompt = x[:bp * lp].reshape(bp, lp, d)
    y_sample = x[bp * lp:].reshape(bs, ls, d)
    return (y_prompt, y_sample)
```

```python
import functools
import math

import numpy as np
import jax
import jax.numpy as jnp
from jax import lax
from jax.experimental import pallas as pl
from jax.experimental.pallas import tpu as pltpu

F32 = jnp.float32
BF16 = jnp.bfloat16

RMS_EPS = 1e-6
HY_BANDS = 16
HY_FAST_PCT = 0.3
HY_SLOW_PCT = 1.5
HY_TARGET = 1e-2
GDN_NK = 8
GDN_NV = 16
GDN_DH = 128
TOP_K = 2

LANES = 128
HALO = 16
DFT_N1 = 256
GDN_CHUNK = 64
GDN_TILE = 256
VMEM_LIMIT = 56 * 1024 * 1024

HIGHEST = lax.Precision.HIGHEST


def _cparams(sem):
    return pltpu.CompilerParams(dimension_semantics=sem, vmem_limit_bytes=VMEM_LIMIT)


def _in_set(v, values):
    r = v == values[0]
    for b in values[1:]:
        r = jnp.logical_or(r, v == b)
    return r


def _norm_rows(x, g):
    ms = jnp.mean(x * x, axis=-1, keepdims=True)
    return x * lax.rsqrt(ms + RMS_EPS) * g


def _proj_conv_kernel(*refs, tm, part_tiles, starts, ends, act, transpose_out):
    n_parts = len(part_tiles)
    g_ref, w_ref, cw_ref, o_ref, h_ref = refs[3 * n_parts:]
    i = pl.program_id(0)
    first = 0
    for p, tiles in enumerate(part_tiles):
        xp_ref, x_ref, xn_ref = refs[3 * p:3 * p + 3]

        @pl.when((pl.program_id(1) == 0) & (i >= first) & (i < first + tiles))
        def _(xp_ref=xp_ref, x_ref=x_ref, xn_ref=xn_ref):
            g = g_ref[...]
            row0 = i * tm
            hp = jnp.where(_in_set(row0, starts), 0.0, _norm_rows(xp_ref[...], g))
            hn = jnp.where(_in_set(row0 + tm, ends), 0.0, _norm_rows(xn_ref[...], g))
            h_ref[0:HALO, :] = hp.astype(BF16)
            h_ref[HALO:HALO + tm, :] = _norm_rows(x_ref[...], g).astype(BF16)
            h_ref[HALO + tm:HALO + tm + HALO, :] = hn.astype(BF16)

        first += tiles

    n = tm + 2 * HALO
    tn = w_ref.shape[1]
    sub = min(tn, 2 * LANES)

    def finish(p, c0):
        up = pltpu.roll(p, 1, axis=0)
        dn = pltpu.roll(p, n - 1, axis=0)
        cw = cw_ref[:, c0:c0 + sub]
        y = (cw[0:1, :] * up[HALO:HALO + tm, :] + cw[1:2, :] * p[HALO:HALO + tm, :]
             + cw[2:3, :] * dn[HALO:HALO + tm, :])
        if act == "silu":
            y = y * jax.nn.sigmoid(y)
        if transpose_out:
            o_ref[c0:c0 + sub, :] = y.T
        else:
            o_ref[:, c0:c0 + sub] = y

    h = h_ref[...]
    prev = None
    for c0 in range(0, tn, sub):
        p = jnp.dot(h, w_ref[:, c0:c0 + sub], preferred_element_type=F32)
        if prev is not None:
            finish(*prev)
        prev = (p, c0)
    finish(*prev)


def _proj_kernel(x_ref, g_ref, w_ref, o_ref, h_ref):
    @pl.when(pl.program_id(1) == 0)
    def _():
        h_ref[...] = _norm_rows(x_ref[...], g_ref[...]).astype(BF16)

    o_ref[...] = jnp.dot(h_ref[...], w_ref[...], preferred_element_type=F32)


def _proj(x, gain, w, conv_w=None, *, act=None, transpose_out=False, starts=(), ends=(),
          tm=1024, tn=512):
    parts = x if isinstance(x, tuple) else (x,)
    m = sum(p.shape[0] for p in parts)
    d = parts[0].shape[1]
    n = w.shape[1]
    tm = min(tm, m)
    if conv_w is not None and n % (2 * tn) == 0:
        tn = 2 * tn
    tn = min(tn, n)
    grid = (m // tm, n // tn)
    gain = gain.reshape(1, d).astype(F32)
    w = w.astype(BF16)
    if conv_w is None:
        assert len(parts) == 1
        return pl.pallas_call(
            _proj_kernel,
            grid=grid,
            in_specs=[pl.BlockSpec((tm, d), lambda i, j: (i, 0)),
                      pl.BlockSpec((1, d), lambda i, j: (0, 0)),
                      pl.BlockSpec((d, tn), lambda i, j: (0, j))],
            out_specs=pl.BlockSpec((tm, tn), lambda i, j: (i, j)),
            out_shape=jax.ShapeDtypeStruct((m, n), F32),
            scratch_shapes=[pltpu.VMEM((tm, d), BF16)],
            compiler_params=_cparams(("parallel", "arbitrary")),
            name="proj",
        )(parts[0], gain, w)
    hb = tm // HALO
    part_tiles = tuple(p.shape[0] // tm for p in parts)
    assert all(p.shape[0] % tm == 0 for p in parts)
    assert all(sum(part_tiles[:k + 1]) * tm in ends for k in range(len(parts)))
    kern = functools.partial(_proj_conv_kernel, tm=tm, part_tiles=part_tiles, starts=tuple(starts),
                             ends=tuple(ends), act=act, transpose_out=transpose_out)
    x_specs, x_args, first = [], [], 0
    for p, tiles in zip(parts, part_tiles):
        last = p.shape[0] // HALO - 1
        loc = lambda i, first=first, tiles=tiles: jnp.clip(i - first, 0, tiles - 1)
        x_specs += [pl.BlockSpec((HALO, d), lambda i, j, loc=loc: (jnp.maximum(loc(i) * hb - 1, 0), 0)),
                    pl.BlockSpec((tm, d), lambda i, j, loc=loc: (loc(i), 0)),
                    pl.BlockSpec((HALO, d), lambda i, j, loc=loc, last=last: (jnp.minimum((loc(i) + 1) * hb, last), 0))]
        x_args += [p, p, p]
        first += tiles
    if transpose_out:
        out_spec = pl.BlockSpec((tn, tm), lambda i, j: (j, i))
        out_shape = jax.ShapeDtypeStruct((n, m), F32)
    else:
        out_spec = pl.BlockSpec((tm, tn), lambda i, j: (i, j))
        out_shape = jax.ShapeDtypeStruct((m, n), F32)
    return pl.pallas_call(
        kern,
        grid=grid,
        in_specs=x_specs + [pl.BlockSpec((1, d), lambda i, j: (0, 0)),
                            pl.BlockSpec((d, tn), lambda i, j: (0, j)),
                            pl.BlockSpec((3, tn), lambda i, j: (0, j))],
        out_specs=out_spec,
        out_shape=out_shape,
        scratch_shapes=[pltpu.VMEM((tm + 2 * HALO, d), BF16)],
        compiler_params=_cparams(("parallel", "arbitrary")),
        name="proj_conv",
    )(*x_args, gain, w, conv_w.astype(F32))


def _dft_rows(n2):
    return n2 // 2 + 8


@functools.lru_cache(maxsize=None)
def _dft_consts(n2):
    n1 = DFT_N1
    n = n1 * n2
    h = n2 // 2
    kept = h + 1
    hr = _dft_rows(n2)
    k1 = np.arange(n1)
    f1 = np.exp(-2j * np.pi * np.outer(k1, k1) / n1)
    f1_fwd = np.block([[f1.real, f1.imag], [-f1.imag, f1.real]])
    f1_inv = np.block([[f1.real, -f1.imag], [f1.imag, f1.real]])
    k2 = np.arange(n2)
    f2 = np.exp(-2j * np.pi * np.outer(k2, k2) / n2)
    pad_rows = lambda a: np.concatenate([a[:kept], np.zeros((hr - kept,) + a.shape[1:])], axis=0)
    tw = pad_rows(np.exp(-2j * np.pi * np.outer(k2, k1) / n))
    lf_full = np.concatenate([pad_rows(f2.real), pad_rows(f2.imag)], axis=0)
    lf_half = lf_full[:, :h]
    weight = np.where((k2 == 0) | (k2 == h), 1.0, 2.0)[None, :] / n
    li_re = pad_rows((f2.real[:h, :] * weight).T).T
    li_im = pad_rows((f2.imag[:h, :] * weight).T).T
    li_half = np.concatenate([li_re, li_im], axis=1)
    return dict(f1_fwd=f1_fwd, f1_inv=f1_inv, twre=tw.real, twim=tw.imag,
                lf_half=lf_half, lf_full=lf_full, li_half=li_half)


def _consts_dev(n2):
    c = _dft_consts(n2)
    return dict(
        f1_fwd=jnp.asarray(c["f1_fwd"], BF16), f1_inv=jnp.asarray(c["f1_inv"], BF16),
        twre=jnp.asarray(c["twre"], F32), twim=jnp.asarray(c["twim"], F32),
        lf_half=jnp.asarray(c["lf_half"], BF16), lf_full=jnp.asarray(c["lf_full"], BF16),
        li_half=jnp.asarray(c["li_half"], BF16))


def _dft_fwd(sig, lf, twre, twim, f1f, a_scr):
    ns, two_hr, _ = a_scr.shape
    hr = two_hr // 2
    for s in range(ns):
        a_scr[s] = jnp.dot(lf, sig(s).astype(BF16), preferred_element_type=F32)
    are = a_scr[:, 0:hr, :]
    aim = a_scr[:, hr:two_hr, :]
    bre = (are * twre - aim * twim).reshape(ns * hr, DFT_N1)
    bim = (are * twim + aim * twre).reshape(ns * hr, DFT_N1)
    bcat = jnp.concatenate([bre, bim], axis=1).astype(BF16)
    cc = jnp.dot(bcat, f1f, preferred_element_type=F32)
    return cc[:, 0:DFT_N1], cc[:, DFT_N1:2 * DFT_N1]


def _dft_inv_real(yre, yim, li, twre, twim, f1i, d_scr, emit):
    ns, two_hr, _ = d_scr.shape
    hr = two_hr // 2
    ycat = jnp.concatenate([yre, yim], axis=1).astype(BF16)
    bb = jnp.dot(ycat, f1i, preferred_element_type=F32)
    bre = bb[:, 0:DFT_N1].reshape(ns, hr, DFT_N1)
    bim = bb[:, DFT_N1:2 * DFT_N1].reshape(ns, hr, DFT_N1)
    d_scr[:, 0:hr, :] = bre * twre + bim * twim
    d_scr[:, hr:two_hr, :] = bim * twre - bre * twim
    for s in range(ns):
        emit(s, jnp.dot(li, d_scr[s].astype(BF16), preferred_element_type=F32))


def _taps_kernel(bands_ref, w1t_ref, w1c_ref, w1s_ref, b1_ref, w2_ref, b2_ref, w3_ref, b3_ref,
                 fq_ref, w4_ref, dl_ref, o_ref, h_ref, tm_ref, *, seq_len, tl):
    nt = pl.program_id(0)

    @pl.when(pl.program_id(1) == 0)
    def _():
        col = nt * tl + lax.broadcasted_iota(jnp.int32, (1, tl), 1)
        pos = jnp.where(col < seq_len, col, 2 * seq_len - col)
        posf = pos.astype(F32)
        t = posf * np.float32(1.0 / (seq_len - 1))
        ang = bands_ref[...] * (posf * np.float32(2.0 * math.pi / seq_len))
        fq = fq_ref[...]
        pre = (w1t_ref[...] * t
               + jnp.dot(w1c_ref[...], jnp.cos(ang), preferred_element_type=F32, precision=HIGHEST)
               - jnp.dot(w1s_ref[...], jnp.sin(ang), preferred_element_type=F32, precision=HIGHEST))
        h = jnp.sin(fq * (pre + b1_ref[...]))
        h = jnp.sin(fq * (jnp.dot(w2_ref[...], h, preferred_element_type=F32, precision=HIGHEST) + b2_ref[...]))
        h = jnp.sin(fq * (jnp.dot(w3_ref[...], h, preferred_element_type=F32, precision=HIGHEST) + b3_ref[...]))
        h_ref[...] = h.astype(BF16)
        tm_ref[0:1, :] = t
        tm_ref[1:2, :] = jnp.where(col == seq_len, 0.0, 1.0)

    taps = jnp.dot(w4_ref[...], h_ref[...], preferred_element_type=F32)
    window = jnp.exp(-(dl_ref[...] * tm_ref[0:1, :]))
    o_ref[...] = taps * window * tm_ref[1:2, :]


def _hyena_taps(seq_len, f_w1, f_b1, f_w2, f_b2, f_w3, f_b3, f_w4, f_freq, d_model, *, tl=2048, td=512):
    fh = f_w1.shape[1]
    n_ord = f_w4.shape[1] // (2 * d_model)
    tl = min(tl, seq_len)
    ndt = d_model // td
    nlt = seq_len // tl
    col = lambda v: v.reshape(-1, 1).astype(F32)
    bands = jnp.linspace(1e-4, HY_BANDS - 1, HY_BANDS, dtype=F32).reshape(-1, 1)
    w1 = f_w1.astype(F32).T
    w4t = f_w4.astype(F32).reshape(fh, n_ord, 2, d_model).transpose(1, 2, 3, 0).reshape(n_ord * 2 * d_model, fh)
    w4t = w4t.astype(BF16)
    max_decay = math.log(HY_TARGET) / HY_FAST_PCT
    min_decay = math.log(HY_TARGET) / HY_SLOW_PCT
    deltas = jnp.abs(jnp.linspace(min_decay, max_decay, d_model, dtype=F32))
    deltas = jnp.tile(deltas, n_ord).reshape(-1, 1)
    small = lambda a: pl.BlockSpec(a.shape, lambda n, j: (0, 0))
    args = [bands, w1[:, 0:1], w1[:, 1:1 + HY_BANDS], w1[:, 1 + HY_BANDS:1 + 2 * HY_BANDS], col(f_b1),
            f_w2.astype(F32).T, col(f_b2), f_w3.astype(F32).T, col(f_b3), col(f_freq)]
    return pl.pallas_call(
        functools.partial(_taps_kernel, seq_len=seq_len, tl=tl),
        grid=(2 * nlt, n_ord * ndt),
        in_specs=[small(a) for a in args] + [
            pl.BlockSpec((td, fh), lambda n, j: ((j // ndt * 2 + (n >= nlt).astype(jnp.int32)) * ndt + j % ndt, 0)),
            pl.BlockSpec((td, 1), lambda n, j: (j, 0))],
        out_specs=pl.BlockSpec((td, tl), lambda n, j: (j, n)),
        out_shape=jax.ShapeDtypeStruct((n_ord * d_model, 2 * seq_len), F32),
        scratch_shapes=[pltpu.VMEM((fh, tl), BF16), pltpu.VMEM((8, tl), F32)],
        compiler_params=_cparams(("parallel", "arbitrary")),
        name="hyena_taps",
    )(*args, w4t, deltas)


def _spectrum_kernel(x_ref, lf_ref, twre_ref, twim_ref, f1f_ref, re_ref, im_ref, a_scr):
    ct, hr, _ = re_ref.shape
    cre, cim = _dft_fwd(lambda s: x_ref[s], lf_ref[...], twre_ref[...][None], twim_ref[...][None],
                        f1f_ref[...], a_scr)
    re_ref[...] = cre.reshape(ct, hr, DFT_N1)
    im_ref[...] = cim.reshape(ct, hr, DFT_N1)


def _spectrum(taps, *, ct=32):
    c, n = taps.shape
    n2 = n // DFT_N1
    hr = _dft_rows(n2)
    cs = _consts_dev(n2)
    x = taps.reshape(c, n2, DFT_N1)
    blk = pl.BlockSpec((ct, hr, DFT_N1), lambda i: (i, 0, 0))
    full = lambda a: pl.BlockSpec(a.shape, lambda i: (0,) * a.ndim)
    consts = [cs["lf_full"], cs["twre"], cs["twim"], cs["f1_fwd"]]
    return pl.pallas_call(
        _spectrum_kernel,
        grid=(c // ct,),
        in_specs=[pl.BlockSpec((ct, n2, DFT_N1), lambda i: (i, 0, 0))] + [full(a) for a in consts],
        out_specs=[blk, blk],
        out_shape=[jax.ShapeDtypeStruct((c, hr, DFT_N1), F32)] * 2,
        scratch_shapes=[pltpu.VMEM((ct, 2 * hr, DFT_N1), F32)],
        compiler_params=_cparams(("parallel",)),
        name="hyena_spectrum",
    )(x, *consts)


def _fftconv_kernel(x_ref, gate_ref, skip_ref,
                    kpre_ref, kpim_ref, lfp_ref, lip_ref, twpre_ref, twpim_ref,
                    ksre_ref, ksim_ref, lfs_ref, lis_ref, twsre_ref, twsim_ref,
                    f1f_ref, f1i_ref, o_ref, ap_scr, dp_scr, as_scr, ds_scr, *, hp, hs, nb):
    ct = x_ref.shape[0]
    f1f = f1f_ref[...]
    f1i = f1i_ref[...]

    def run(rows, n_sig, kre, kim, lf, li, twre, twim, a_scr, d_scr):
        def sig(s):
            c, r0, h = rows(s)
            return x_ref[c, r0:r0 + h, :]

        cre, cim = _dft_fwd(sig, lf, twre, twim, f1f, a_scr)
        yre = cre * kre - cim * kim
        yim = cre * kim + cim * kre

        def emit(s, y):
            c, r0, h = rows(s)
            xs = x_ref[c, r0:r0 + h, :]
            o_ref[c, r0:r0 + h, :] = gate_ref[c, r0:r0 + h, :] * (y + skip_ref[c] * xs)

        _dft_inv_real(yre, yim, li, twre, twim, f1i, d_scr, emit)

    hrp = kpre_ref.shape[1]
    run(lambda s: (s, 0, hp), ct,
        kpre_ref[...].reshape(ct * hrp, DFT_N1), kpim_ref[...].reshape(ct * hrp, DFT_N1),
        lfp_ref[...], lip_ref[...], twpre_ref[...][None], twpim_ref[...][None], ap_scr, dp_scr)
    hrs = ksre_ref.shape[1]
    rep = lambda k: jnp.broadcast_to(k[:, None], (ct, nb, hrs, DFT_N1)).reshape(ct * nb * hrs, DFT_N1)
    run(lambda s: (s // nb, hp + (s % nb) * hs, hs), ct * nb,
        rep(ksre_ref[...]), rep(ksim_ref[...]),
        lfs_ref[...], lis_ref[...], twsre_ref[...][None], twsim_ref[...][None], as_scr, ds_scr)


def _fftconv(x, x_off, gate, gate_off, skip, kp, kp_off, ks, ks_off, *, d_model, hp, hs, nb, ct=8):
    r = hp + nb * hs
    cp, cs_ = _consts_dev(2 * hp), _consts_dev(2 * hs)
    hrp, hrs = _dft_rows(2 * hp), _dft_rows(2 * hs)
    cb = lambda off: (lambda i: (off // ct + i, 0, 0))
    full = lambda a: pl.BlockSpec(a.shape, lambda i: (0,) * a.ndim)
    kspec = lambda rows, off: pl.BlockSpec((ct, rows, DFT_N1), cb(off))
    skip3 = jnp.broadcast_to(skip.astype(F32).reshape(d_model, 1, 1), (d_model, 1, DFT_N1))
    pc = [cp["lf_half"], cp["li_half"], cp["twre"], cp["twim"]]
    sc = [cs_["lf_half"], cs_["li_half"], cs_["twre"], cs_["twim"]]
    return pl.pallas_call(
        functools.partial(_fftconv_kernel, hp=hp, hs=hs, nb=nb),
        grid=(d_model // ct,),
        in_specs=[pl.BlockSpec((ct, r, DFT_N1), cb(x_off)), pl.BlockSpec((ct, r, DFT_N1), cb(gate_off)),
                  pl.BlockSpec((ct, 1, DFT_N1), cb(0)),
                  kspec(hrp, kp_off), kspec(hrp, kp_off)] + [full(a) for a in pc]
                 + [kspec(hrs, ks_off), kspec(hrs, ks_off)] + [full(a) for a in sc]
                 + [full(cp["f1_fwd"]), full(cp["f1_inv"])],
        out_specs=pl.BlockSpec((ct, r, DFT_N1), cb(0)),
        out_shape=jax.ShapeDtypeStruct((d_model, r, DFT_N1), F32),
        scratch_shapes=[pltpu.VMEM((ct, 2 * hrp, DFT_N1), F32), pltpu.VMEM((ct, 2 * hrp, DFT_N1), F32),
                        pltpu.VMEM((ct * nb, 2 * hrs, DFT_N1), F32), pltpu.VMEM((ct * nb, 2 * hrs, DFT_N1), F32)],
        compiler_params=_cparams(("parallel",)),
        name="hyena_fftconv",
    )(x, gate, skip3, kp[0], kp[1], *pc, ks[0], ks[1], *sc, cp["f1_fwd"], cp["f1_inv"])


def _outproj_t_kernel(zt_ref, w_ref, *refs, part_tiles):
    res_refs, o_ref = refs[:-1], refs[-1]
    i = pl.program_id(0)
    z = zt_ref[...].astype(BF16)
    y = lax.dot_general(z, w_ref[...], (((0,), (0,)), ((), ())), preferred_element_type=F32)
    first = 0
    for res_ref, tiles in zip(res_refs, part_tiles):
        @pl.when((i >= first) & (i < first + tiles))
        def _(res_ref=res_ref):
            o_ref[...] = res_ref[...] + y

        first += tiles


def _outproj_t(zt, w, res, *, tm=512):
    parts = res if isinstance(res, tuple) else (res,)
    k, m = zt.shape
    n = w.shape[1]
    tm = min(tm, m)
    part_tiles = tuple(p.shape[0] // tm for p in parts)
    assert all(p.shape[0] % tm == 0 for p in parts) and sum(part_tiles) * tm == m
    res_specs, first = [], 0
    for tiles in part_tiles:
        res_specs.append(pl.BlockSpec((tm, n), lambda i, first=first, tiles=tiles: (jnp.clip(i - first, 0, tiles - 1), 0)))
        first += tiles
    return pl.pallas_call(
        functools.partial(_outproj_t_kernel, part_tiles=part_tiles),
        grid=(m // tm,),
        in_specs=[pl.BlockSpec((k, tm), lambda i: (0, i)),
                  pl.BlockSpec((k, n), lambda i: (0, 0))] + res_specs,
        out_specs=pl.BlockSpec((tm, n), lambda i: (i, 0)),
        out_shape=jax.ShapeDtypeStruct((m, n), F32),
        compiler_params=_cparams(("parallel",)),
        name="hyena_outproj",
    )(zt, w.astype(BF16), *parts)


def _softplus(x):
    return jnp.maximum(x, 0.0) + jnp.log1p(jnp.exp(-jnp.abs(x)))


def _gdn_kernel(qf_ref, kf_ref, vf_ref, baf_ref, batf_ref, qb_ref, kb_ref, vb_ref, bab_ref, batb_ref,
                alog_r_ref, dtb_r_ref, alog_c_ref, dtb_c_ref, of_ref, ob_ref, sf_ref, sb_ref, *,
                n_steps, f_resets, b_resets, n_pairs, n_sub):
    hp = pl.program_id(0)
    step = pl.program_id(1)
    t = GDN_TILE
    c = GDN_CHUNK
    nc = t // c
    dh = GDN_DH

    @pl.when(_in_set(step, f_resets))
    def _():
        sf_ref[...] = jnp.zeros_like(sf_ref)

    @pl.when(_in_set(n_steps - 1 - step, b_resets))
    def _():
        sb_ref[...] = jnp.zeros_like(sb_ref)

    ri = lax.broadcasted_iota(jnp.int32, (t, t), 0)
    ci = lax.broadcasted_iota(jnp.int32, (t, t), 1)
    same = (ri // c) == (ci // c)
    rw = lax.broadcasted_iota(jnp.int32, (c, t), 0)
    lw = lax.broadcasted_iota(jnp.int32, (c, t), 1)
    lblk = lw // c
    lsub = lw % c
    rsub = lax.broadcasted_iota(jnp.int32, (t, LANES), 0) % c
    lane = lax.broadcasted_iota(jnp.int32, (t, LANES), 1)
    lrow = lax.broadcasted_iota(jnp.int32, (8, t), 1) % c

    def diag_to_wide(full):
        parts = [jnp.where(lblk == j, full[j * c:(j + 1) * c, :], 0.0) for j in range(nc)]
        return functools.reduce(lambda x, y: x + y, parts)

    def wide_to_diag16(wide):
        return jnp.where(same, jnp.concatenate([wide] * nc, axis=0), 0.0).astype(BF16)

    def l2n(x):
        return x * lax.rsqrt(jnp.sum(x * x, axis=-1, keepdims=True) + RMS_EPS)

    def mm(a, b):
        return jnp.dot(a, b, preferred_element_type=F32)

    dirs = [(0, qf_ref, kf_ref, vf_ref, baf_ref, batf_ref), (1, qb_ref, kb_ref, vb_ref, bab_ref, batb_ref)]

    def stage1_dir(d, r0):
        _, _, _, v_ref, ba_ref, bat_ref = dirs[d]
        back = d == 1
        ba = ba_ref[r0:r0 + t, :]
        g_all = -jnp.exp(alog_r_ref[...]) * _softplus(ba + dtb_r_ref[...])
        sh = 1
        while sh < c:
            if back:
                g_all = g_all + jnp.where(rsub < c - sh, pltpu.roll(g_all, t - sh, axis=0), 0.0)
            else:
                g_all = g_all + jnp.where(rsub >= sh, pltpu.roll(g_all, sh, axis=0), 0.0)
            sh *= 2
        return dict(back=back, incl=(lsub >= rw) if back else (lsub <= rw),
                    strict=(lsub > rw) if back else (lsub < rw),
                    beta_all=jax.nn.sigmoid(ba), g_all=g_all, bat_ref=bat_ref, v_ref=v_ref, r0=r0)

    def stage1_pair(d, pp, r0):
        _, q_ref, k_ref, _, _, _ = dirs[d]
        q = l2n(q_ref[r0:r0 + t, pp * dh:(pp + 1) * dh]) * np.float32(GDN_DH ** -0.5)
        k = l2n(k_ref[r0:r0 + t, pp * dh:(pp + 1) * dh])
        k16 = k.astype(BF16)
        qk_kk = lax.dot_general(jnp.concatenate([q.astype(BF16), k16], axis=0), k16, (((1,), (1,)), ((), ())),
                                preferred_element_type=F32)
        return dict(q=q, k=k, qk=diag_to_wide(qk_kk[0:t]), kk=diag_to_wide(qk_kk[t:2 * t]))

    def stage2(d, pd, pr, ph):
        back = pd["back"]
        head = 2 * n_pairs * hp + ph
        jb = d * GDN_NV + head
        ja = 2 * GDN_NV + d * GDN_NV + head
        beta_c = jnp.sum(jnp.where(lane == jb, pd["beta_all"], 0.0), axis=1, keepdims=True)
        gc_c = jnp.sum(jnp.where(lane == ja, pd["g_all"], 0.0), axis=1, keepdims=True)
        a_row = pd["bat_ref"][pl.ds(ja, 1), pd["r0"]:pd["r0"] + t]
        g_row = -jnp.exp(alog_c_ref[pl.ds(ja, 1), :]) * _softplus(a_row + dtb_c_ref[pl.ds(ja, 1), :])
        g_row = jnp.broadcast_to(g_row, (8, t))
        sh = 1
        while sh < c:
            if back:
                g_row = g_row + jnp.where(lrow < c - sh, pltpu.roll(g_row, t - sh, axis=1), 0.0)
            else:
                g_row = g_row + jnp.where(lrow >= sh, pltpu.roll(g_row, sh, axis=1), 0.0)
            sh *= 2
        gc_r = g_row[0:1, :]
        incl = pd["incl"]
        decay = jnp.where(incl, jnp.exp(jnp.where(incl, diag_to_wide(gc_c) - gc_r, 0.0)), 0.0)
        p = jnp.where(pd["strict"], -(diag_to_wide(beta_c) * pr["kk"] * decay), 0.0)
        return dict(pd=pd, pr=pr, ph=ph, beta_c=beta_c, gc_c=gc_c, decay=decay, p=p)

    def stage3(group):
        for v in group:
            v["nn"] = v["p"]
            v["pm"] = mm(v["p"].astype(BF16), wide_to_diag16(v["p"]))
            yield
        m = 2
        while 2 * m < c:
            for v in group:
                both = mm(jnp.concatenate([v["pm"], v["nn"]], axis=0).astype(BF16), wide_to_diag16(v["pm"]))
                v["nn"] = v["nn"] + v["pm"] + both[c:2 * c]
                v["pm"] = both[0:c]
                yield
            m *= 2
        for v in group:
            v["nn"] = v["nn"] + v["pm"] + mm(v["nn"].astype(BF16), wide_to_diag16(v["pm"]))
            yield

    def stage4(group):
        for v in group:
            pd, pr, ph, beta_c, gc_c = v["pd"], v["pr"], v["ph"], v["beta_c"], v["gc_c"]
            vv = pd["v_ref"][pd["r0"]:pd["r0"] + t, ph * dh:(ph + 1) * dh]
            rhs = jnp.concatenate([vv * beta_c, pr["k"] * (beta_c * jnp.exp(gc_c))], axis=1)
            v["uw16"] = (rhs + mm(wide_to_diag16(v["nn"]), rhs.astype(BF16))).astype(BF16)
        for v in group:
            pd, pr, gc_c = v["pd"], v["pr"], v["gc_c"]
            qq = mm(wide_to_diag16(jnp.where(pd["incl"], pr["qk"] * v["decay"], 0.0)), v["uw16"])
            gc3 = gc_c.reshape(nc, c, 1)
            gl3 = gc3[:, 0:1, :] if pd["back"] else gc3[:, c - 1:c, :]
            kg16 = (pr["k"] * jnp.exp(jnp.broadcast_to(gl3, (nc, c, 1)).reshape(t, 1) - gc_c)).astype(BF16)
            v["qp16"] = (pr["q"] * jnp.exp(gc_c) - qq[:, dh:2 * dh]).astype(BF16)
            v["qq"] = qq
            v["gl3"] = gl3
            v["rp"] = [lax.dot_general(kg16[n * c:(n + 1) * c], v["uw16"][n * c:(n + 1) * c],
                                       (((0,), (0,)), ((), ())), preferred_element_type=F32)
                       for n in range(nc)]

    units = [(sub, d) for sub in range(n_sub) for d in range(2)]
    row0 = lambda sub, d: (n_sub - 1 - sub if d == 1 else sub) * t
    var = {}
    groups = {u: [] for u in units}

    def setup(u):
        sub, d = u
        pd = stage1_dir(d, row0(sub, d))
        yield
        for pp in range(n_pairs):
            pr = stage1_pair(d, pp, row0(sub, d))
            yield
            for ph in (2 * pp, 2 * pp + 1):
                var[(sub, d, ph)] = stage2(d, pd, pr, ph)
                groups[u].append(var[(sub, d, ph)])
                yield

    def run(main, filler=None, every=1):
        for i, _ in enumerate(main):
            if filler is not None and i % every == every - 1:
                next(filler, None)
        if filler is not None:
            for _ in filler:
                pass

    zero = jnp.zeros((dh, dh), F32)
    s_refs = [sf_ref, sb_ref]
    outs = [of_ref, ob_ref]
    state = [[s_refs[d][ph] for ph in range(2 * n_pairs)] for d in range(2)]

    def stage5(sub):
        for i in range(nc):
            for d in range(2):
                n = nc - 1 - i if d == 1 else i
                sl = slice(n * c, (n + 1) * c)
                so = slice(row0(sub, d) + n * c, row0(sub, d) + (n + 1) * c)
                for pp in range(n_pairs):
                    v0, v1 = var[(sub, d, 2 * pp)], var[(sub, d, 2 * pp + 1)]
                    s0, s1 = state[d][2 * pp], state[d][2 * pp + 1]
                    s_d = jnp.concatenate([jnp.concatenate([s0, zero], axis=1),
                                           jnp.concatenate([zero, s1], axis=1)], axis=0).astype(BF16)
                    lhs = jnp.concatenate([
                        jnp.concatenate([v0["rp"][n][:, dh:2 * dh], v1["rp"][n][:, dh:2 * dh]], axis=1).astype(BF16),
                        jnp.concatenate([v0["qp16"][sl], v1["qp16"][sl]], axis=1)], axis=0)
                    z = mm(lhs, s_d)
                    c0 = 2 * pp * dh
                    outs[d][so, c0:c0 + dh] = z[dh:dh + c, 0:dh] + v0["qq"][sl, 0:dh]
                    outs[d][so, c0 + dh:c0 + 2 * dh] = z[dh:dh + c, dh:2 * dh] + v1["qq"][sl, 0:dh]
                    state[d][2 * pp] = jnp.exp(v0["gl3"][n]) * s0 - z[0:dh, 0:dh] + v0["rp"][n][:, 0:dh]
                    state[d][2 * pp + 1] = jnp.exp(v1["gl3"][n]) * s1 - z[0:dh, dh:2 * dh] + v1["rp"][n][:, 0:dh]

    n_setup = 1 + 3 * n_pairs
    n_doubling = 2 * n_pairs * (c.bit_length() - 1)
    run(setup(units[0]))
    for i, u in enumerate(units):
        filler = setup(units[i + 1]) if i + 1 < len(units) else None
        run(stage3(groups[u]), filler, every=max(1, n_doubling // n_setup))
        if u[1] == 1:
            stage4(groups[(u[0], 0)])
            stage4(groups[(u[0], 1)])
            stage5(u[0])
    for d in range(2):
        for ph in range(2 * n_pairs):
            s_refs[d][ph] = state[d][ph]


def _gdn_scan(qkv, ba, a_log, dt_bias, *, starts, ends, n_pairs=4, n_sub=2):
    m = qkv.shape[0]
    t = n_sub * GDN_TILE
    assert all(s % t == 0 for s in starts) and all(e % t == 0 for e in ends)
    n_tiles = m // t
    bat = ba.T
    pad = lambda a: jnp.pad(a.astype(F32).reshape(-1), (2 * GDN_NV, LANES - 4 * GDN_NV))
    alog_r = pad(a_log).reshape(1, LANES)
    dtb_r = pad(dt_bias).reshape(1, LANES)
    alog_c = pad(a_log).reshape(LANES, 1)
    dtb_c = pad(dt_bias).reshape(LANES, 1)
    f_resets = tuple(s // t for s in starts)
    b_resets = tuple(e // t - 1 for e in ends)
    fwd = lambda h, s: s
    bwd = lambda h, s: n_tiles - 1 - s

    kw = n_pairs * GDN_DH
    key_blocks = GDN_NK // n_pairs

    def specs(tile):
        return [pl.BlockSpec((t, kw), lambda h, s: (tile(h, s), h)),
                pl.BlockSpec((t, kw), lambda h, s: (tile(h, s), key_blocks + h)),
                pl.BlockSpec((t, 2 * kw), lambda h, s: (tile(h, s), key_blocks + h)),
                pl.BlockSpec((t, LANES), lambda h, s: (tile(h, s), 0)),
                pl.BlockSpec((LANES, t), lambda h, s: (0, tile(h, s)))]

    small = lambda a: pl.BlockSpec(a.shape, lambda h, s: (0, 0))
    out = jax.ShapeDtypeStruct((m, GDN_NV * GDN_DH), F32)
    return pl.pallas_call(
        functools.partial(_gdn_kernel, n_steps=n_tiles, f_resets=f_resets, b_resets=b_resets, n_pairs=n_pairs,
                          n_sub=n_sub),
        grid=(key_blocks, n_tiles),
        in_specs=specs(fwd) + specs(bwd) + [small(alog_r), small(dtb_r), small(alog_c), small(dtb_c)],
        out_specs=[pl.BlockSpec((t, 2 * kw), lambda h, s: (s, h)),
                   pl.BlockSpec((t, 2 * kw), lambda h, s: (n_tiles - 1 - s, h))],
        out_shape=[out, out],
        scratch_shapes=[pltpu.VMEM((2 * n_pairs, GDN_DH, GDN_DH), F32),
                        pltpu.VMEM((2 * n_pairs, GDN_DH, GDN_DH), F32)],
        compiler_params=_cparams(("parallel", "arbitrary")),
        name="gdn_scan",
    )(qkv, qkv, qkv, ba, bat, qkv, qkv, qkv, ba, bat, alog_r, dtb_r, alog_c, dtb_c)


def _gdn_out_kernel(of_ref, ob_ref, x_ref, g_ref, wz_ref, nw_ref, w_ref, o_ref, y_ref):
    nw = nw_ref[...]
    x = x_ref[...]
    h16 = _norm_rows(x, g_ref[...]).astype(BF16)
    for h in range(GDN_NV):
        sl = slice(h * GDN_DH, (h + 1) * GDN_DH)
        if h % 2 == 0:
            z2 = jnp.dot(h16, wz_ref[:, h * GDN_DH:(h + 2) * GDN_DH], preferred_element_type=F32)
        z = z2[:, (h % 2) * GDN_DH:(h % 2 + 1) * GDN_DH]
        o = of_ref[:, sl] + ob_ref[:, sl]
        o = o * lax.rsqrt(jnp.mean(o * o, axis=-1, keepdims=True) + RMS_EPS) * nw
        y_ref[:, sl] = (o * (z * jax.nn.sigmoid(z))).astype(BF16)
    o_ref[...] = x + jnp.dot(y_ref[...], w_ref[...], preferred_element_type=F32)


def _gdn_out(o_f, o_b, x, gain, w_z, norm_w, w, *, tm=512):
    m, kv = o_f.shape
    d = x.shape[1]
    n = w.shape[1]
    tm = min(tm, m)
    row = lambda width: pl.BlockSpec((tm, width), lambda i: (i, 0))
    whole = lambda a: pl.BlockSpec(a.shape, lambda i: (0, 0))
    args = [gain.astype(F32).reshape(1, d), w_z.astype(BF16), norm_w.astype(F32).reshape(1, GDN_DH), w.astype(BF16)]
    return pl.pallas_call(
        _gdn_out_kernel,
        grid=(m // tm,),
        in_specs=[row(kv), row(kv), row(d)] + [whole(a) for a in args],
        out_specs=row(n),
        out_shape=jax.ShapeDtypeStruct((m, n), F32),
        scratch_shapes=[pltpu.VMEM((tm, kv), BF16)],
        compiler_params=_cparams(("parallel",)),
        name="gdn_out",
    )(o_f, o_b, x, *args)


def _swiglu_tile(h, wg_ref, wu_ref, wd_ref, lead=()):
    tf = wg_ref.shape[-1]
    sub = 2 * LANES if tf % (2 * LANES) == 0 else tf
    y = None
    prev = None

    def finish(a, u, c0):
        act = (a * jax.nn.sigmoid(a) * u).astype(BF16)
        return jnp.dot(act, wd_ref[lead + (slice(c0, c0 + sub), slice(None))], preferred_element_type=F32)

    for c0 in range(0, tf, sub):
        cols = lead + (slice(None), slice(c0, c0 + sub))
        a = jnp.dot(h, wg_ref[cols], preferred_element_type=F32)
        u = jnp.dot(h, wu_ref[cols], preferred_element_type=F32)
        if prev is not None:
            part = finish(*prev)
            y = part if y is None else y + part
        prev = (a, u, c0)
    part = finish(*prev)
    return part if y is None else y + part


def _ffn_kernel(x_ref, g_ref, wg_ref, wu_ref, wd_ref, gf_ref, o_ref, h_ref, acc_ref, *, final_norm):
    f = pl.program_id(1)

    @pl.when(f == 0)
    def _():
        h_ref[...] = _norm_rows(x_ref[...], g_ref[...]).astype(BF16)
        acc_ref[...] = jnp.zeros_like(acc_ref)

    acc_ref[...] += _swiglu_tile(h_ref[...], wg_ref, wu_ref, wd_ref)

    @pl.when(f == pl.num_programs(1) - 1)
    def _():
        out = x_ref[...] + acc_ref[...]
        if final_norm:
            out = _norm_rows(out, gf_ref[...])
        o_ref[...] = out


def _ffn(x, gain, w_gate, w_up, w_down, *, final_gain=None, tm=1024, tf=1792):
    m, d = x.shape
    ff = w_gate.shape[1]
    tm = min(tm, m)
    tf = min(tf, ff)
    final_norm = final_gain is not None
    gf = (final_gain if final_norm else gain).astype(F32).reshape(1, d)
    return pl.pallas_call(
        functools.partial(_ffn_kernel, final_norm=final_norm),
        grid=(m // tm, ff // tf),
        in_specs=[pl.BlockSpec((tm, d), lambda i, f: (i, 0)),
                  pl.BlockSpec((1, d), lambda i, f: (0, 0)),
                  pl.BlockSpec((d, tf), lambda i, f: (0, f)),
                  pl.BlockSpec((d, tf), lambda i, f: (0, f)),
                  pl.BlockSpec((tf, d), lambda i, f: (f, 0)),
                  pl.BlockSpec((1, d), lambda i, f: (0, 0))],
        out_specs=pl.BlockSpec((tm, d), lambda i, f: (i, 0)),
        out_shape=jax.ShapeDtypeStruct((m, d), F32),
        scratch_shapes=[pltpu.VMEM((tm, d), BF16), pltpu.VMEM((tm, d), F32)],
        compiler_params=_cparams(("parallel", "arbitrary")),
        name="ffn_mixer",
    )(x, gain.astype(F32).reshape(1, d), w_gate.astype(BF16), w_up.astype(BF16), w_down.astype(BF16), gf)


def _moe_kernel(x_ref, g_ref, r_ref, rb_ref, wg_ref, wu_ref, wd_ref, gf_ref, o_ref,
                h_ref, acc_ref, gates_ref, posc_ref, posr_ref, hx_ref, y_ref, nblk_ref, *,
                n_exp, final_norm, rb):
    e = pl.program_id(1)
    f = pl.program_id(2)
    tm = x_ref.shape[0]
    lane = lax.broadcasted_iota(jnp.int32, (tm, LANES), 1)

    @pl.when((e == 0) & (f == 0))
    def _():
        h = _norm_rows(x_ref[...], g_ref[...])
        h_ref[...] = h.astype(BF16)
        acc_ref[...] = jnp.zeros_like(acc_ref)
        h_hi = h_ref[...]
        h_lo = (h - h_hi.astype(F32)).astype(BF16)
        r_hi = r_ref[0]
        logits = (jnp.dot(h_hi, r_hi, preferred_element_type=F32)
                  + jnp.dot(h_hi, r_ref[1], preferred_element_type=F32)
                  + jnp.dot(h_lo, r_hi, preferred_element_type=F32)) + rb_ref[...]
        logits = jnp.where(lane < n_exp, logits, -jnp.inf)
        m1 = jnp.max(logits, axis=1, keepdims=True)
        i1 = jnp.min(jnp.where(logits == m1, lane, LANES), axis=1, keepdims=True)
        rest = jnp.where(lane == i1, -jnp.inf, logits)
        m2 = jnp.max(rest, axis=1, keepdims=True)
        i2 = jnp.min(jnp.where(rest == m2, lane, LANES), axis=1, keepdims=True)
        e2 = jnp.exp(m2 - m1)
        w1 = 1.0 / (1.0 + e2)
        gates_ref[...] = jnp.where(lane == i1, w1, 0.0) + jnp.where(lane == i2, e2 * w1, 0.0)
        chosen = jnp.where((lane == i1) | (lane == i2), 1.0, 0.0)
        ri = lax.broadcasted_iota(jnp.int32, (tm, tm), 0)
        ci = lax.broadcasted_iota(jnp.int32, (tm, tm), 1)
        before = jnp.where(ci < ri, 1.0, 0.0).astype(BF16)
        rank = jnp.dot(before, chosen.astype(BF16), preferred_element_type=F32)
        posc = jnp.where(chosen > 0.5, rank, -1.0)
        posc_ref[...] = posc
        posr_ref[...] = posc.T
        for ee in range(n_exp):
            cnt = jnp.sum(chosen[:, ee:ee + 1]).astype(jnp.int32)
            nblk_ref[ee] = (cnt + rb - 1) // rb

    nblk = nblk_ref[e]

    @pl.when(f == 0)
    def _():
        pos_row = posr_ref[pl.ds(e, 1), :]
        h = h_ref[...]

        def gather(b, carry):
            r0 = pl.multiple_of(b * 2 * rb, 2 * rb)
            rows = (lax.broadcasted_iota(jnp.int32, (2 * rb, 1), 0) + r0).astype(F32)
            onehot = jnp.where(pos_row == rows, 1.0, 0.0).astype(BF16)
            hx_ref[pl.ds(r0, 2 * rb), :] = jnp.dot(onehot, h, preferred_element_type=F32).astype(BF16)
            y_ref[pl.ds(r0, 2 * rb), :] = jnp.zeros((2 * rb, y_ref.shape[1]), F32)
            return carry

        lax.fori_loop(0, (nblk + 1) // 2, gather, 0)

    def expert(r0, rows):
        y_ref[pl.ds(r0, rows), :] += _swiglu_tile(hx_ref[pl.ds(r0, rows), :], wg_ref, wu_ref, wd_ref, lead=(0,))

    def expert_pair(b, carry):
        expert(pl.multiple_of(b * 2 * rb, 2 * rb), 2 * rb)
        return carry

    lax.fori_loop(0, nblk // 2, expert_pair, 0)

    @pl.when(nblk % 2 == 1)
    def _():
        expert(pl.multiple_of((nblk - 1) * rb, rb), rb)

    @pl.when(f == pl.num_programs(2) - 1)
    def _():
        pos_col = jnp.sum(jnp.where(lane == e, posc_ref[...], 0.0), axis=1, keepdims=True)
        gate = jnp.sum(jnp.where(lane == e, gates_ref[...], 0.0), axis=1, keepdims=True)

        def scatter(b, carry):
            r0 = pl.multiple_of(b * 2 * rb, 2 * rb)
            cols = (lax.broadcasted_iota(jnp.int32, (1, 2 * rb), 1) + r0).astype(F32)
            onehot = jnp.where(pos_col == cols, 1.0, 0.0).astype(BF16)
            back = jnp.dot(onehot, y_ref[pl.ds(r0, 2 * rb), :].astype(BF16), preferred_element_type=F32)
            acc_ref[...] += gate * back
            return carry

        lax.fori_loop(0, (nblk + 1) // 2, scatter, 0)

    @pl.when((e == n_exp - 1) & (f == pl.num_programs(2) - 1))
    def _():
        out = x_ref[...] + acc_ref[...]
        if final_norm:
            out = _norm_rows(out, gf_ref[...])
        o_ref[...] = out


def _moe(x, gain, router, router_bias, w_gate, w_up, w_down, *, final_gain=None, tm=1024, tf=1792, rb=128):
    m, d = x.shape
    n_exp, _, ff = w_gate.shape
    tm = min(tm, m)
    tf = tf if ff % tf == 0 else min(512, ff)
    assert (tm // rb) % 2 == 0
    r = jnp.pad(router.astype(F32), ((0, 0), (0, LANES - n_exp)))
    r_hi = r.astype(BF16)
    r = jnp.stack([r_hi, (r - r_hi.astype(F32)).astype(BF16)])
    rbias = jnp.pad(router_bias.astype(F32).reshape(1, n_exp), ((0, 0), (0, LANES - n_exp)))
    final_norm = final_gain is not None
    gf = (final_gain if final_norm else gain).astype(F32).reshape(1, d)
    return pl.pallas_call(
        functools.partial(_moe_kernel, n_exp=n_exp, final_norm=final_norm, rb=rb),
        grid=(m // tm, n_exp, ff // tf),
        in_specs=[pl.BlockSpec((tm, d), lambda i, e, f: (i, 0), pipeline_mode=pl.Buffered(1)),
                  pl.BlockSpec((1, d), lambda i, e, f: (0, 0)),
                  pl.BlockSpec((2, d, LANES), lambda i, e, f: (0, 0, 0)),
                  pl.BlockSpec((1, LANES), lambda i, e, f: (0, 0)),
                  pl.BlockSpec((1, d, tf), lambda i, e, f: (e, 0, f)),
                  pl.BlockSpec((1, d, tf), lambda i, e, f: (e, 0, f)),
                  pl.BlockSpec((1, tf, d), lambda i, e, f: (e, f, 0)),
                  pl.BlockSpec((1, d), lambda i, e, f: (0, 0))],
        out_specs=pl.BlockSpec((tm, d), lambda i, e, f: (i, 0)),
        out_shape=jax.ShapeDtypeStruct((m, d), F32),
        scratch_shapes=[pltpu.VMEM((tm, d), BF16), pltpu.VMEM((tm, d), F32), pltpu.VMEM((tm, LANES), F32),
                        pltpu.VMEM((tm, LANES), F32), pltpu.VMEM((LANES, tm), F32),
                        pltpu.VMEM((tm, d), BF16), pltpu.VMEM((tm, d), F32), pltpu.SMEM((n_exp,), jnp.int32)],
        compiler_params=_cparams(("parallel", "arbitrary", "arbitrary")),
        name="moe_mixer",
    )(x, gain.astype(F32).reshape(1, d), r, rbias, w_gate.astype(BF16), w_up.astype(BF16),
      w_down.astype(BF16), gf)


def _hyena_layer(x, gain, lp, ls, nb_p, nb_s, w_in, conv_w, f_w1, f_b1, f_w2, f_b2, f_w3, f_b3, f_w4,
                 f_freq, skip, w_out, *, starts, ends):
    parts = x if isinstance(x, tuple) else (x,)
    m, d = sum(p.shape[0] for p in parts), parts[0].shape[1]
    assert nb_p == 1 and lp % (2 * DFT_N1) == 0 and ls % (2 * DFT_N1) == 0
    hp, hs = lp // DFT_N1, ls // DFT_N1
    ut = _proj(x, gain, w_in, conv_w, transpose_out=True, starts=starts, ends=ends)
    u3 = ut.reshape(3 * d, m // DFT_N1, DFT_N1)
    filt = (f_w1, f_b1, f_w2, f_b2, f_w3, f_b3, f_w4, f_freq)
    kp = _spectrum(_hyena_taps(lp, *filt, d))
    ks = _spectrum(_hyena_taps(ls, *filt, d))
    conv = functools.partial(_fftconv, d_model=d, hp=hp, hs=hs, nb=nb_s)
    z1 = conv(u3, 0, u3, d, skip[0], kp, 0, ks, 0)
    z2 = conv(z1, 0, u3, 2 * d, skip[1], kp, d, ks, d)
    return _outproj_t(z2.reshape(d, m), w_out, x)


def _gdn_layer(x, gain, w_in, conv_w, a_log, dt_bias, norm_w, w_out, *, starts, ends):
    key, val = GDN_NK * GDN_DH, GDN_NV * GDN_DH
    cd = 2 * key + val
    qkv = _proj(x, gain, w_in[:, :cd], conv_w, act="silu", starts=starts, ends=ends)
    w_ba = jnp.pad(w_in[:, cd + val:], ((0, 0), (0, LANES - 4 * GDN_NV)))
    ba = _proj(x, gain, w_ba)
    o_f, o_b = _gdn_scan(qkv, ba, a_log, dt_bias, starts=starts, ends=ends)
    return _gdn_out(o_f, o_b, x, gain, w_in[:, cd:cd + val], norm_w, w_out)


def kernel(x_prompt, x_sample, norm_mix, norm_ffn, norm_final, hy_w_in, hy_conv, hy_f_w1, hy_f_b1, hy_f_w2, hy_f_b2, hy_f_w3, hy_f_b3, hy_f_w4, hy_f_freq, hy_skip, hy_w_out, gdn_w_in, gdn_conv, gdn_a_log, gdn_dt_bias, gdn_norm, gdn_w_out, ffn_w_gate, ffn_w_up, ffn_w_down, moe_router, moe_router_bias, moe_w_gate, moe_w_up, moe_w_down):
    bp, lp, d = x_prompt.shape
    bs, ls, _ = x_sample.shape
    x = (x_prompt.reshape(bp * lp, d), x_sample.reshape(bs * ls, d))
    starts = tuple(b * lp for b in range(bp)) + tuple(bp * lp + b * ls for b in range(bs))
    ends = tuple(s + lp for s in starts[:bp]) + tuple(s + ls for s in starts[bp:])
    depth = norm_mix.shape[0]
    for i in range(depth):
        j = i // 2
        last = i == depth - 1
        if i % 2 == 0:
            x = _hyena_layer(x, norm_mix[i], lp, ls, bp, bs, hy_w_in[j], hy_conv[j], hy_f_w1[j], hy_f_b1[j],
                             hy_f_w2[j], hy_f_b2[j], hy_f_w3[j], hy_f_b3[j], hy_f_w4[j], hy_f_freq[j],
                             hy_skip[j], hy_w_out[j], starts=starts, ends=ends)
            x = _ffn(x, norm_ffn[i], ffn_w_gate[j], ffn_w_up[j], ffn_w_down[j],
                     final_gain=norm_final if last else None)
        else:
            if isinstance(x, tuple):
                x = jnp.concatenate(x, axis=0)
            x = _gdn_layer(x, norm_mix[i], gdn_w_in[j], gdn_conv[j], gdn_a_log[j], gdn_dt_bias[j],
                           gdn_norm[j], gdn_w_out[j], starts=starts, ends=ends)
            x = _moe(x, norm_ffn[i], moe_router[j], moe_router_bias[j], moe_w_gate[j], moe_w_up[j],
                     moe_w_down[j], final_gain=norm_final if last else None)
    y_prompt = x[:bp * lp].reshape(bp, lp, d)
    y_sample = x[bp * lp:].reshape(bs, ls, d)
    return (y_prompt, y_sample)
```

```python
import functools
import math

import numpy as np
import jax
import jax.numpy as jnp
from jax import lax
from jax.experimental import pallas as pl
from jax.experimental.pallas import tpu as pltpu

F32 = jnp.float32
BF16 = jnp.bfloat16

RMS_EPS = 1e-6
HY_BANDS = 16
HY_FAST_PCT = 0.3
HY_SLOW_PCT = 1.5
HY_TARGET = 1e-2
GDN_NK = 8
GDN_NV = 16
GDN_DH = 128
TOP_K = 2

LANES = 128
HALO = 16
DFT_N1 = 256
GDN_CHUNK = 64
GDN_TILE = 256
VMEM_LIMIT = 56 * 1024 * 1024

HIGHEST = lax.Precision.HIGHEST


def _cparams(sem):
    return pltpu.CompilerParams(dimension_semantics=sem, vmem_limit_bytes=VMEM_LIMIT)


def _in_set(v, values):
    r = v == values[0]
    for b in values[1:]:
        r = jnp.logical_or(r, v == b)
    return r


def _norm_rows(x, g):
    ms = jnp.mean(x * x, axis=-1, keepdims=True)
    return x * lax.rsqrt(ms + RMS_EPS) * g


def _proj_conv_kernel(*refs, tm, part_tiles, starts, ends, act, transpose_out):
    n_parts = len(part_tiles)
    g_ref, w_ref, cw_ref, o_ref, h_ref = refs[3 * n_parts:]
    i = pl.program_id(0)
    first = 0
    for p, tiles in enumerate(part_tiles):
        xp_ref, x_ref, xn_ref = refs[3 * p:3 * p + 3]

        @pl.when((pl.program_id(1) == 0) & (i >= first) & (i < first + tiles))
        def _(xp_ref=xp_ref, x_ref=x_ref, xn_ref=xn_ref):
            g = g_ref[...]
            row0 = i * tm
            hp = jnp.where(_in_set(row0, starts), 0.0, _norm_rows(xp_ref[...], g))
            hn = jnp.where(_in_set(row0 + tm, ends), 0.0, _norm_rows(xn_ref[...], g))
            h_ref[0:HALO, :] = hp.astype(BF16)
            h_ref[HALO:HALO + tm, :] = _norm_rows(x_ref[...], g).astype(BF16)
            h_ref[HALO + tm:HALO + tm + HALO, :] = hn.astype(BF16)

        first += tiles

    n = tm + 2 * HALO
    tn = w_ref.shape[1]
    sub = min(tn, 2 * LANES)

    def finish(p, c0):
        up = pltpu.roll(p, 1, axis=0)
        dn = pltpu.roll(p, n - 1, axis=0)
        cw = cw_ref[:, c0:c0 + sub]
        y = (cw[0:1, :] * up[HALO:HALO + tm, :] + cw[1:2, :] * p[HALO:HALO + tm, :]
             + cw[2:3, :] * dn[HALO:HALO + tm, :])
        if act == "silu":
            y = y * jax.nn.sigmoid(y)
        if transpose_out:
            o_ref[c0:c0 + sub, :] = y.T
        else:
            o_ref[:, c0:c0 + sub] = y

    h = h_ref[...]
    prev = None
    for c0 in range(0, tn, sub):
        p = jnp.dot(h, w_ref[:, c0:c0 + sub], preferred_element_type=F32)
        if prev is not None:
            finish(*prev)
        prev = (p, c0)
    finish(*prev)


def _proj_kernel(x_ref, g_ref, w_ref, o_ref, h_ref):
    @pl.when(pl.program_id(1) == 0)
    def _():
        h_ref[...] = _norm_rows(x_ref[...], g_ref[...]).astype(BF16)

    o_ref[...] = jnp.dot(h_ref[...], w_ref[...], preferred_element_type=F32)


def _proj(x, gain, w, conv_w=None, *, act=None, transpose_out=False, starts=(), ends=(),
          tm=1024, tn=512):
    parts = x if isinstance(x, tuple) else (x,)
    m = sum(p.shape[0] for p in parts)
    d = parts[0].shape[1]
    n = w.shape[1]
    tm = min(tm, m)
    if conv_w is not None:
        wide = [c for c in range(2 * LANES, 4 * tn + 1, 2 * LANES) if n % c == 0]
        tn = wide[-1] if wide else tn
    tn = min(tn, n)
    grid = (m // tm, n // tn)
    gain = gain.reshape(1, d).astype(F32)
    w = w.astype(BF16)
    if conv_w is None:
        assert len(parts) == 1
        return pl.pallas_call(
            _proj_kernel,
            grid=grid,
            in_specs=[pl.BlockSpec((tm, d), lambda i, j: (i, 0)),
                      pl.BlockSpec((1, d), lambda i, j: (0, 0)),
                      pl.BlockSpec((d, tn), lambda i, j: (0, j))],
            out_specs=pl.BlockSpec((tm, tn), lambda i, j: (i, j)),
            out_shape=jax.ShapeDtypeStruct((m, n), F32),
            scratch_shapes=[pltpu.VMEM((tm, d), BF16)],
            compiler_params=_cparams(("parallel", "arbitrary")),
            name="proj",
        )(parts[0], gain, w)
    hb = tm // HALO
    part_tiles = tuple(p.shape[0] // tm for p in parts)
    assert all(p.shape[0] % tm == 0 for p in parts)
    assert all(sum(part_tiles[:k + 1]) * tm in ends for k in range(len(parts)))
    kern = functools.partial(_proj_conv_kernel, tm=tm, part_tiles=part_tiles, starts=tuple(starts),
                             ends=tuple(ends), act=act, transpose_out=transpose_out)
    x_specs, x_args, first = [], [], 0
    for p, tiles in zip(parts, part_tiles):
        last = p.shape[0] // HALO - 1
        loc = lambda i, first=first, tiles=tiles: jnp.clip(i - first, 0, tiles - 1)
        x_specs += [pl.BlockSpec((HALO, d), lambda i, j, loc=loc: (jnp.maximum(loc(i) * hb - 1, 0), 0)),
                    pl.BlockSpec((tm, d), lambda i, j, loc=loc: (loc(i), 0)),
                    pl.BlockSpec((HALO, d), lambda i, j, loc=loc, last=last: (jnp.minimum((loc(i) + 1) * hb, last), 0))]
        x_args += [p, p, p]
        first += tiles
    if transpose_out:
        out_spec = pl.BlockSpec((tn, tm), lambda i, j: (j, i))
        out_shape = jax.ShapeDtypeStruct((n, m), F32)
    else:
        out_spec = pl.BlockSpec((tm, tn), lambda i, j: (i, j))
        out_shape = jax.ShapeDtypeStruct((m, n), F32)
    return pl.pallas_call(
        kern,
        grid=grid,
        in_specs=x_specs + [pl.BlockSpec((1, d), lambda i, j: (0, 0)),
                            pl.BlockSpec((d, tn), lambda i, j: (0, j)),
                            pl.BlockSpec((3, tn), lambda i, j: (0, j))],
        out_specs=out_spec,
        out_shape=out_shape,
        scratch_shapes=[pltpu.VMEM((tm + 2 * HALO, d), BF16)],
        compiler_params=_cparams(("parallel", "arbitrary")),
        name="proj_conv",
    )(*x_args, gain, w, conv_w.astype(F32))


def _dft_rows(n2):
    return n2 // 2 + 8


@functools.lru_cache(maxsize=None)
def _dft_consts(n2):
    n1 = DFT_N1
    n = n1 * n2
    h = n2 // 2
    kept = h + 1
    hr = _dft_rows(n2)
    k1 = np.arange(n1)
    f1 = np.exp(-2j * np.pi * np.outer(k1, k1) / n1)
    f1_fwd = np.block([[f1.real, f1.imag], [-f1.imag, f1.real]])
    f1_inv = np.block([[f1.real, -f1.imag], [f1.imag, f1.real]])
    k2 = np.arange(n2)
    f2 = np.exp(-2j * np.pi * np.outer(k2, k2) / n2)
    pad_rows = lambda a: np.concatenate([a[:kept], np.zeros((hr - kept,) + a.shape[1:])], axis=0)
    tw = pad_rows(np.exp(-2j * np.pi * np.outer(k2, k1) / n))
    lf_full = np.concatenate([pad_rows(f2.real), pad_rows(f2.imag)], axis=0)
    lf_half = lf_full[:, :h]
    weight = np.where((k2 == 0) | (k2 == h), 1.0, 2.0)[None, :] / n
    li_re = pad_rows((f2.real[:h, :] * weight).T).T
    li_im = pad_rows((f2.imag[:h, :] * weight).T).T
    li_half = np.concatenate([li_re, li_im], axis=1)
    return dict(f1_fwd=f1_fwd, f1_inv=f1_inv, twre=tw.real, twim=tw.imag,
                lf_half=lf_half, lf_full=lf_full, li_half=li_half)


def _consts_dev(n2):
    c = _dft_consts(n2)
    return dict(
        f1_fwd=jnp.asarray(c["f1_fwd"], BF16), f1_inv=jnp.asarray(c["f1_inv"], BF16),
        twre=jnp.asarray(c["twre"], F32), twim=jnp.asarray(c["twim"], F32),
        lf_half=jnp.asarray(c["lf_half"], BF16), lf_full=jnp.asarray(c["lf_full"], BF16),
        li_half=jnp.asarray(c["li_half"], BF16))


def _dft_fwd(sig, lf, twre, twim, f1f, a_scr):
    ns, two_hr, _ = a_scr.shape
    hr = two_hr // 2
    for s in range(ns):
        a_scr[s] = jnp.dot(lf, sig(s).astype(BF16), preferred_element_type=F32)
    are = a_scr[:, 0:hr, :]
    aim = a_scr[:, hr:two_hr, :]
    bre = (are * twre - aim * twim).reshape(ns * hr, DFT_N1)
    bim = (are * twim + aim * twre).reshape(ns * hr, DFT_N1)
    bcat = jnp.concatenate([bre, bim], axis=1).astype(BF16)
    cc = jnp.dot(bcat, f1f, preferred_element_type=F32)
    return cc[:, 0:DFT_N1], cc[:, DFT_N1:2 * DFT_N1]


def _dft_inv_real(yre, yim, li, twre, twim, f1i, d_scr, emit):
    ns, two_hr, _ = d_scr.shape
    hr = two_hr // 2
    ycat = jnp.concatenate([yre, yim], axis=1).astype(BF16)
    bb = jnp.dot(ycat, f1i, preferred_element_type=F32)
    bre = bb[:, 0:DFT_N1].reshape(ns, hr, DFT_N1)
    bim = bb[:, DFT_N1:2 * DFT_N1].reshape(ns, hr, DFT_N1)
    d_scr[:, 0:hr, :] = bre * twre + bim * twim
    d_scr[:, hr:two_hr, :] = bim * twre - bre * twim
    for s in range(ns):
        emit(s, jnp.dot(li, d_scr[s].astype(BF16), preferred_element_type=F32))


def _taps_kernel(bands_ref, w1t_ref, w1c_ref, w1s_ref, b1_ref, w2_ref, b2_ref, w3_ref, b3_ref,
                 fq_ref, w4_ref, dl_ref, o_ref, h_ref, tm_ref, *, seq_len, tl):
    nt = pl.program_id(0)

    @pl.when(pl.program_id(1) == 0)
    def _():
        col = nt * tl + lax.broadcasted_iota(jnp.int32, (1, tl), 1)
        pos = jnp.where(col < seq_len, col, 2 * seq_len - col)
        posf = pos.astype(F32)
        t = posf * np.float32(1.0 / (seq_len - 1))
        ang = bands_ref[...] * (posf * np.float32(2.0 * math.pi / seq_len))
        fq = fq_ref[...]
        pre = (w1t_ref[...] * t
               + jnp.dot(w1c_ref[...], jnp.cos(ang), preferred_element_type=F32, precision=HIGHEST)
               - jnp.dot(w1s_ref[...], jnp.sin(ang), preferred_element_type=F32, precision=HIGHEST))
        h = jnp.sin(fq * (pre + b1_ref[...]))
        h = jnp.sin(fq * (jnp.dot(w2_ref[...], h, preferred_element_type=F32, precision=HIGHEST) + b2_ref[...]))
        h = jnp.sin(fq * (jnp.dot(w3_ref[...], h, preferred_element_type=F32, precision=HIGHEST) + b3_ref[...]))
        h_ref[...] = h.astype(BF16)
        tm_ref[0:1, :] = t
        tm_ref[1:2, :] = jnp.where(col == seq_len, 0.0, 1.0)

    taps = jnp.dot(w4_ref[...], h_ref[...], preferred_element_type=F32)
    window = jnp.exp(-(dl_ref[...] * tm_ref[0:1, :]))
    o_ref[...] = taps * window * tm_ref[1:2, :]


def _hyena_taps(seq_len, f_w1, f_b1, f_w2, f_b2, f_w3, f_b3, f_w4, f_freq, d_model, *, tl=2048, td=512):
    fh = f_w1.shape[1]
    n_ord = f_w4.shape[1] // (2 * d_model)
    tl = min(tl, seq_len)
    ndt = d_model // td
    nlt = seq_len // tl
    col = lambda v: v.reshape(-1, 1).astype(F32)
    bands = jnp.linspace(1e-4, HY_BANDS - 1, HY_BANDS, dtype=F32).reshape(-1, 1)
    w1 = f_w1.astype(F32).T
    w4t = f_w4.astype(F32).reshape(fh, n_ord, 2, d_model).transpose(1, 2, 3, 0).reshape(n_ord * 2 * d_model, fh)
    w4t = w4t.astype(BF16)
    max_decay = math.log(HY_TARGET) / HY_FAST_PCT
    min_decay = math.log(HY_TARGET) / HY_SLOW_PCT
    deltas = jnp.abs(jnp.linspace(min_decay, max_decay, d_model, dtype=F32))
    deltas = jnp.tile(deltas, n_ord).reshape(-1, 1)
    small = lambda a: pl.BlockSpec(a.shape, lambda n, j: (0, 0))
    args = [bands, w1[:, 0:1], w1[:, 1:1 + HY_BANDS], w1[:, 1 + HY_BANDS:1 + 2 * HY_BANDS], col(f_b1),
            f_w2.astype(F32).T, col(f_b2), f_w3.astype(F32).T, col(f_b3), col(f_freq)]
    return pl.pallas_call(
        functools.partial(_taps_kernel, seq_len=seq_len, tl=tl),
        grid=(2 * nlt, n_ord * ndt),
        in_specs=[small(a) for a in args] + [
            pl.BlockSpec((td, fh), lambda n, j: ((j // ndt * 2 + (n >= nlt).astype(jnp.int32)) * ndt + j % ndt, 0)),
            pl.BlockSpec((td, 1), lambda n, j: (j, 0))],
        out_specs=pl.BlockSpec((td, tl), lambda n, j: (j, n)),
        out_shape=jax.ShapeDtypeStruct((n_ord * d_model, 2 * seq_len), F32),
        scratch_shapes=[pltpu.VMEM((fh, tl), BF16), pltpu.VMEM((8, tl), F32)],
        compiler_params=_cparams(("parallel", "arbitrary")),
        name="hyena_taps",
    )(*args, w4t, deltas)


def _spectrum_kernel(x_ref, lf_ref, twre_ref, twim_ref, f1f_ref, re_ref, im_ref, a_scr):
    ct, hr, _ = re_ref.shape
    cre, cim = _dft_fwd(lambda s: x_ref[s], lf_ref[...], twre_ref[...][None], twim_ref[...][None],
                        f1f_ref[...], a_scr)
    re_ref[...] = cre.reshape(ct, hr, DFT_N1)
    im_ref[...] = cim.reshape(ct, hr, DFT_N1)


def _spectrum(taps, *, ct=32):
    c, n = taps.shape
    n2 = n // DFT_N1
    hr = _dft_rows(n2)
    cs = _consts_dev(n2)
    x = taps.reshape(c, n2, DFT_N1)
    blk = pl.BlockSpec((ct, hr, DFT_N1), lambda i: (i, 0, 0))
    full = lambda a: pl.BlockSpec(a.shape, lambda i: (0,) * a.ndim)
    consts = [cs["lf_full"], cs["twre"], cs["twim"], cs["f1_fwd"]]
    return pl.pallas_call(
        _spectrum_kernel,
        grid=(c // ct,),
        in_specs=[pl.BlockSpec((ct, n2, DFT_N1), lambda i: (i, 0, 0))] + [full(a) for a in consts],
        out_specs=[blk, blk],
        out_shape=[jax.ShapeDtypeStruct((c, hr, DFT_N1), F32)] * 2,
        scratch_shapes=[pltpu.VMEM((ct, 2 * hr, DFT_N1), F32)],
        compiler_params=_cparams(("parallel",)),
        name="hyena_spectrum",
    )(x, *consts)


def _fftconv_kernel(x_ref, gate_ref, skip_ref,
                    kpre_ref, kpim_ref, lfp_ref, lip_ref, twpre_ref, twpim_ref,
                    ksre_ref, ksim_ref, lfs_ref, lis_ref, twsre_ref, twsim_ref,
                    f1f_ref, f1i_ref, o_ref, ap_scr, dp_scr, as_scr, ds_scr, *, hp, hs, nb):
    ct = x_ref.shape[0]
    f1f = f1f_ref[...]
    f1i = f1i_ref[...]

    def run(rows, n_sig, kre, kim, lf, li, twre, twim, a_scr, d_scr):
        def sig(s):
            c, r0, h = rows(s)
            return x_ref[c, r0:r0 + h, :]

        cre, cim = _dft_fwd(sig, lf, twre, twim, f1f, a_scr)
        yre = cre * kre - cim * kim
        yim = cre * kim + cim * kre

        def emit(s, y):
            c, r0, h = rows(s)
            xs = x_ref[c, r0:r0 + h, :]
            o_ref[c, r0:r0 + h, :] = gate_ref[c, r0:r0 + h, :] * (y + skip_ref[c] * xs)

        _dft_inv_real(yre, yim, li, twre, twim, f1i, d_scr, emit)

    hrp = kpre_ref.shape[1]
    run(lambda s: (s, 0, hp), ct,
        kpre_ref[...].reshape(ct * hrp, DFT_N1), kpim_ref[...].reshape(ct * hrp, DFT_N1),
        lfp_ref[...], lip_ref[...], twpre_ref[...][None], twpim_ref[...][None], ap_scr, dp_scr)
    hrs = ksre_ref.shape[1]
    rep = lambda k: jnp.broadcast_to(k[:, None], (ct, nb, hrs, DFT_N1)).reshape(ct * nb * hrs, DFT_N1)
    run(lambda s: (s // nb, hp + (s % nb) * hs, hs), ct * nb,
        rep(ksre_ref[...]), rep(ksim_ref[...]),
        lfs_ref[...], lis_ref[...], twsre_ref[...][None], twsim_ref[...][None], as_scr, ds_scr)


def _fftconv(x, x_off, gate, gate_off, skip, kp, kp_off, ks, ks_off, *, d_model, hp, hs, nb, ct=8):
    r = hp + nb * hs
    cp, cs_ = _consts_dev(2 * hp), _consts_dev(2 * hs)
    hrp, hrs = _dft_rows(2 * hp), _dft_rows(2 * hs)
    cb = lambda off: (lambda i: (off // ct + i, 0, 0))
    full = lambda a: pl.BlockSpec(a.shape, lambda i: (0,) * a.ndim)
    kspec = lambda rows, off: pl.BlockSpec((ct, rows, DFT_N1), cb(off))
    skip3 = jnp.broadcast_to(skip.astype(F32).reshape(d_model, 1, 1), (d_model, 1, DFT_N1))
    pc = [cp["lf_half"], cp["li_half"], cp["twre"], cp["twim"]]
    sc = [cs_["lf_half"], cs_["li_half"], cs_["twre"], cs_["twim"]]
    return pl.pallas_call(
        functools.partial(_fftconv_kernel, hp=hp, hs=hs, nb=nb),
        grid=(d_model // ct,),
        in_specs=[pl.BlockSpec((ct, r, DFT_N1), cb(x_off)), pl.BlockSpec((ct, r, DFT_N1), cb(gate_off)),
                  pl.BlockSpec((ct, 1, DFT_N1), cb(0)),
                  kspec(hrp, kp_off), kspec(hrp, kp_off)] + [full(a) for a in pc]
                 + [kspec(hrs, ks_off), kspec(hrs, ks_off)] + [full(a) for a in sc]
                 + [full(cp["f1_fwd"]), full(cp["f1_inv"])],
        out_specs=pl.BlockSpec((ct, r, DFT_N1), cb(0)),
        out_shape=jax.ShapeDtypeStruct((d_model, r, DFT_N1), F32),
        scratch_shapes=[pltpu.VMEM((ct, 2 * hrp, DFT_N1), F32), pltpu.VMEM((ct, 2 * hrp, DFT_N1), F32),
                        pltpu.VMEM((ct * nb, 2 * hrs, DFT_N1), F32), pltpu.VMEM((ct * nb, 2 * hrs, DFT_N1), F32)],
        compiler_params=_cparams(("parallel",)),
        name="hyena_fftconv",
    )(x, gate, skip3, kp[0], kp[1], *pc, ks[0], ks[1], *sc, cp["f1_fwd"], cp["f1_inv"])


def _outproj_t_kernel(zt_ref, w_ref, *refs, part_tiles):
    res_refs, o_ref = refs[:-1], refs[-1]
    i = pl.program_id(0)
    z = zt_ref[...].astype(BF16)
    y = lax.dot_general(z, w_ref[...], (((0,), (0,)), ((), ())), preferred_element_type=F32)
    first = 0
    for res_ref, tiles in zip(res_refs, part_tiles):
        @pl.when((i >= first) & (i < first + tiles))
        def _(res_ref=res_ref):
            o_ref[...] = res_ref[...] + y

        first += tiles


def _outproj_t(zt, w, res, *, tm=512):
    parts = res if isinstance(res, tuple) else (res,)
    k, m = zt.shape
    n = w.shape[1]
    tm = min(tm, m)
    part_tiles = tuple(p.shape[0] // tm for p in parts)
    assert all(p.shape[0] % tm == 0 for p in parts) and sum(part_tiles) * tm == m
    res_specs, first = [], 0
    for tiles in part_tiles:
        res_specs.append(pl.BlockSpec((tm, n), lambda i, first=first, tiles=tiles: (jnp.clip(i - first, 0, tiles - 1), 0)))
        first += tiles
    return pl.pallas_call(
        functools.partial(_outproj_t_kernel, part_tiles=part_tiles),
        grid=(m // tm,),
        in_specs=[pl.BlockSpec((k, tm), lambda i: (0, i)),
                  pl.BlockSpec((k, n), lambda i: (0, 0))] + res_specs,
        out_specs=pl.BlockSpec((tm, n), lambda i: (i, 0)),
        out_shape=jax.ShapeDtypeStruct((m, n), F32),
        compiler_params=_cparams(("parallel",)),
        name="hyena_outproj",
    )(zt, w.astype(BF16), *parts)


def _softplus(x):
    return jnp.maximum(x, 0.0) + jnp.log1p(jnp.exp(-jnp.abs(x)))


def _gdn_kernel(qf_ref, kf_ref, vf_ref, baf_ref, batf_ref, qb_ref, kb_ref, vb_ref, bab_ref, batb_ref,
                alog_r_ref, dtb_r_ref, alog_c_ref, dtb_c_ref, of_ref, ob_ref, sf_ref, sb_ref, *,
                n_steps, f_resets, b_resets, n_pairs, n_sub):
    hp = pl.program_id(0)
    step = pl.program_id(1)
    t = GDN_TILE
    c = GDN_CHUNK
    nc = t // c
    dh = GDN_DH

    @pl.when(_in_set(step, f_resets))
    def _():
        sf_ref[...] = jnp.zeros_like(sf_ref)

    @pl.when(_in_set(n_steps - 1 - step, b_resets))
    def _():
        sb_ref[...] = jnp.zeros_like(sb_ref)

    ri = lax.broadcasted_iota(jnp.int32, (t, t), 0)
    ci = lax.broadcasted_iota(jnp.int32, (t, t), 1)
    same = (ri // c) == (ci // c)
    rw = lax.broadcasted_iota(jnp.int32, (c, t), 0)
    lw = lax.broadcasted_iota(jnp.int32, (c, t), 1)
    lblk = lw // c
    lsub = lw % c
    rsub = lax.broadcasted_iota(jnp.int32, (t, LANES), 0) % c
    lane = lax.broadcasted_iota(jnp.int32, (t, LANES), 1)
    lrow = lax.broadcasted_iota(jnp.int32, (8, t), 1) % c

    def diag_to_wide(full):
        parts = [jnp.where(lblk == j, full[j * c:(j + 1) * c, :], 0.0) for j in range(nc)]
        return functools.reduce(lambda x, y: x + y, parts)

    def wide_to_diag16(wide):
        return jnp.where(same, jnp.concatenate([wide] * nc, axis=0), 0.0).astype(BF16)

    def l2n(x):
        return x * lax.rsqrt(jnp.sum(x * x, axis=-1, keepdims=True) + RMS_EPS)

    def mm(a, b):
        return jnp.dot(a, b, preferred_element_type=F32)

    dirs = [(0, qf_ref, kf_ref, vf_ref, baf_ref, batf_ref), (1, qb_ref, kb_ref, vb_ref, bab_ref, batb_ref)]

    def stage1_dir(d, r0):
        _, _, _, v_ref, ba_ref, bat_ref = dirs[d]
        back = d == 1
        ba = ba_ref[r0:r0 + t, :]
        g_all = -jnp.exp(alog_r_ref[...]) * _softplus(ba + dtb_r_ref[...])
        sh = 1
        while sh < c:
            if back:
                g_all = g_all + jnp.where(rsub < c - sh, pltpu.roll(g_all, t - sh, axis=0), 0.0)
            else:
                g_all = g_all + jnp.where(rsub >= sh, pltpu.roll(g_all, sh, axis=0), 0.0)
            sh *= 2
        return dict(back=back, incl=(lsub >= rw) if back else (lsub <= rw),
                    strict=(lsub > rw) if back else (lsub < rw),
                    beta_all=jax.nn.sigmoid(ba), g_all=g_all, bat_ref=bat_ref, v_ref=v_ref, r0=r0)

    def stage1_pair(d, pp, r0):
        _, q_ref, k_ref, _, _, _ = dirs[d]
        q = l2n(q_ref[r0:r0 + t, pp * dh:(pp + 1) * dh]) * np.float32(GDN_DH ** -0.5)
        k = l2n(k_ref[r0:r0 + t, pp * dh:(pp + 1) * dh])
        k16 = k.astype(BF16)
        qk_kk = lax.dot_general(jnp.concatenate([q.astype(BF16), k16], axis=0), k16, (((1,), (1,)), ((), ())),
                                preferred_element_type=F32)
        return dict(q=q, k=k, qk=diag_to_wide(qk_kk[0:t]), kk=diag_to_wide(qk_kk[t:2 * t]))

    def stage2(d, pd, pr, ph):
        back = pd["back"]
        head = 2 * n_pairs * hp + ph
        jb = d * GDN_NV + head
        ja = 2 * GDN_NV + d * GDN_NV + head
        beta_c = jnp.sum(jnp.where(lane == jb, pd["beta_all"], 0.0), axis=1, keepdims=True)
        gc_c = jnp.sum(jnp.where(lane == ja, pd["g_all"], 0.0), axis=1, keepdims=True)
        a_row = pd["bat_ref"][pl.ds(ja, 1), pd["r0"]:pd["r0"] + t]
        g_row = -jnp.exp(alog_c_ref[pl.ds(ja, 1), :]) * _softplus(a_row + dtb_c_ref[pl.ds(ja, 1), :])
        g_row = jnp.broadcast_to(g_row, (8, t))
        sh = 1
        while sh < c:
            if back:
                g_row = g_row + jnp.where(lrow < c - sh, pltpu.roll(g_row, t - sh, axis=1), 0.0)
            else:
                g_row = g_row + jnp.where(lrow >= sh, pltpu.roll(g_row, sh, axis=1), 0.0)
            sh *= 2
        gc_r = g_row[0:1, :]
        incl = pd["incl"]
        decay = jnp.where(incl, jnp.exp(jnp.where(incl, diag_to_wide(gc_c) - gc_r, 0.0)), 0.0)
        p = jnp.where(pd["strict"], -(diag_to_wide(beta_c) * pr["kk"] * decay), 0.0)
        return dict(pd=pd, pr=pr, ph=ph, beta_c=beta_c, gc_c=gc_c, decay=decay, p=p)

    def stage3(group):
        for v in group:
            v["nn"] = v["p"]
            v["pm"] = mm(v["p"].astype(BF16), wide_to_diag16(v["p"]))
            yield
        m = 2
        while 2 * m < c:
            for v in group:
                both = mm(jnp.concatenate([v["pm"], v["nn"]], axis=0).astype(BF16), wide_to_diag16(v["pm"]))
                v["nn"] = v["nn"] + v["pm"] + both[c:2 * c]
                v["pm"] = both[0:c]
                yield
            m *= 2
        for v in group:
            v["nn"] = v["nn"] + v["pm"] + mm(v["nn"].astype(BF16), wide_to_diag16(v["pm"]))
            yield

    def stage4(group):
        for v in group:
            pd, pr, ph, beta_c, gc_c = v["pd"], v["pr"], v["ph"], v["beta_c"], v["gc_c"]
            vv = pd["v_ref"][pd["r0"]:pd["r0"] + t, ph * dh:(ph + 1) * dh]
            rhs = jnp.concatenate([vv * beta_c, pr["k"] * (beta_c * jnp.exp(gc_c))], axis=1)
            v["uw16"] = (rhs + mm(wide_to_diag16(v["nn"]), rhs.astype(BF16))).astype(BF16)
        for v in group:
            pd, pr, gc_c = v["pd"], v["pr"], v["gc_c"]
            qq = mm(wide_to_diag16(jnp.where(pd["incl"], pr["qk"] * v["decay"], 0.0)), v["uw16"])
            gc3 = gc_c.reshape(nc, c, 1)
            gl3 = gc3[:, 0:1, :] if pd["back"] else gc3[:, c - 1:c, :]
            kg16 = (pr["k"] * jnp.exp(jnp.broadcast_to(gl3, (nc, c, 1)).reshape(t, 1) - gc_c)).astype(BF16)
            v["qp16"] = (pr["q"] * jnp.exp(gc_c) - qq[:, dh:2 * dh]).astype(BF16)
            v["qq"] = qq
            v["gl3"] = gl3
            v["rp"] = [lax.dot_general(kg16[n * c:(n + 1) * c], v["uw16"][n * c:(n + 1) * c],
                                       (((0,), (0,)), ((), ())), preferred_element_type=F32)
                       for n in range(nc)]

    units = [(sub, d) for sub in range(n_sub) for d in range(2)]
    row0 = lambda sub, d: (n_sub - 1 - sub if d == 1 else sub) * t
    var = {}
    groups = {u: [] for u in units}

    def setup(u):
        sub, d = u
        pd = stage1_dir(d, row0(sub, d))
        yield
        for pp in range(n_pairs):
            pr = stage1_pair(d, pp, row0(sub, d))
            yield
            for ph in (2 * pp, 2 * pp + 1):
                var[(sub, d, ph)] = stage2(d, pd, pr, ph)
                groups[u].append(var[(sub, d, ph)])
                yield

    def run(main, filler=None, every=1):
        for i, _ in enumerate(main):
            if filler is not None and i % every == every - 1:
                next(filler, None)
        if filler is not None:
            for _ in filler:
                pass

    zero = jnp.zeros((dh, dh), F32)
    s_refs = [sf_ref, sb_ref]
    outs = [of_ref, ob_ref]
    state = [[s_refs[d][ph] for ph in range(2 * n_pairs)] for d in range(2)]

    def stage5(sub):
        for i in range(nc):
            for d in range(2):
                n = nc - 1 - i if d == 1 else i
                sl = slice(n * c, (n + 1) * c)
                so = slice(row0(sub, d) + n * c, row0(sub, d) + (n + 1) * c)
                for pp in range(n_pairs):
                    v0, v1 = var[(sub, d, 2 * pp)], var[(sub, d, 2 * pp + 1)]
                    s0, s1 = state[d][2 * pp], state[d][2 * pp + 1]
                    s_d = jnp.concatenate([jnp.concatenate([s0, zero], axis=1),
                                           jnp.concatenate([zero, s1], axis=1)], axis=0).astype(BF16)
                    lhs = jnp.concatenate([
                        jnp.concatenate([v0["rp"][n][:, dh:2 * dh], v1["rp"][n][:, dh:2 * dh]], axis=1).astype(BF16),
                        jnp.concatenate([v0["qp16"][sl], v1["qp16"][sl]], axis=1)], axis=0)
                    z = mm(lhs, s_d)
                    c0 = 2 * pp * dh
                    outs[d][so, c0:c0 + dh] = z[dh:dh + c, 0:dh] + v0["qq"][sl, 0:dh]
                    outs[d][so, c0 + dh:c0 + 2 * dh] = z[dh:dh + c, dh:2 * dh] + v1["qq"][sl, 0:dh]
                    state[d][2 * pp] = jnp.exp(v0["gl3"][n]) * s0 - z[0:dh, 0:dh] + v0["rp"][n][:, 0:dh]
                    state[d][2 * pp + 1] = jnp.exp(v1["gl3"][n]) * s1 - z[0:dh, dh:2 * dh] + v1["rp"][n][:, 0:dh]

    n_setup = 1 + 3 * n_pairs
    n_doubling = 2 * n_pairs * (c.bit_length() - 1)
    run(setup(units[0]))
    for i, u in enumerate(units):
        filler = setup(units[i + 1]) if i + 1 < len(units) else None
        run(stage3(groups[u]), filler, every=max(1, n_doubling // n_setup))
        if u[1] == 1:
            stage4(groups[(u[0], 0)])
            stage4(groups[(u[0], 1)])
            stage5(u[0])
    for d in range(2):
        for ph in range(2 * n_pairs):
            s_refs[d][ph] = state[d][ph]


def _gdn_scan(qkv, ba, a_log, dt_bias, *, starts, ends, n_pairs=4, n_sub=2):
    m = qkv.shape[0]
    t = n_sub * GDN_TILE
    assert all(s % t == 0 for s in starts) and all(e % t == 0 for e in ends)
    n_tiles = m // t
    bat = ba.T
    pad = lambda a: jnp.pad(a.astype(F32).reshape(-1), (2 * GDN_NV, LANES - 4 * GDN_NV))
    alog_r = pad(a_log).reshape(1, LANES)
    dtb_r = pad(dt_bias).reshape(1, LANES)
    alog_c = pad(a_log).reshape(LANES, 1)
    dtb_c = pad(dt_bias).reshape(LANES, 1)
    f_resets = tuple(s // t for s in starts)
    b_resets = tuple(e // t - 1 for e in ends)
    fwd = lambda h, s: s
    bwd = lambda h, s: n_tiles - 1 - s

    kw = n_pairs * GDN_DH
    key_blocks = GDN_NK // n_pairs

    def specs(tile):
        return [pl.BlockSpec((t, kw), lambda h, s: (tile(h, s), h)),
                pl.BlockSpec((t, kw), lambda h, s: (tile(h, s), key_blocks + h)),
                pl.BlockSpec((t, 2 * kw), lambda h, s: (tile(h, s), key_blocks + h)),
                pl.BlockSpec((t, LANES), lambda h, s: (tile(h, s), 0)),
                pl.BlockSpec((LANES, t), lambda h, s: (0, tile(h, s)))]

    small = lambda a: pl.BlockSpec(a.shape, lambda h, s: (0, 0))
    out = jax.ShapeDtypeStruct((m, GDN_NV * GDN_DH), F32)
    return pl.pallas_call(
        functools.partial(_gdn_kernel, n_steps=n_tiles, f_resets=f_resets, b_resets=b_resets, n_pairs=n_pairs,
                          n_sub=n_sub),
        grid=(key_blocks, n_tiles),
        in_specs=specs(fwd) + specs(bwd) + [small(alog_r), small(dtb_r), small(alog_c), small(dtb_c)],
        out_specs=[pl.BlockSpec((t, 2 * kw), lambda h, s: (s, h)),
                   pl.BlockSpec((t, 2 * kw), lambda h, s: (n_tiles - 1 - s, h))],
        out_shape=[out, out],
        scratch_shapes=[pltpu.VMEM((2 * n_pairs, GDN_DH, GDN_DH), F32),
                        pltpu.VMEM((2 * n_pairs, GDN_DH, GDN_DH), F32)],
        compiler_params=_cparams(("parallel", "arbitrary")),
        name="gdn_scan",
    )(qkv, qkv, qkv, ba, bat, qkv, qkv, qkv, ba, bat, alog_r, dtb_r, alog_c, dtb_c)


def _gdn_out_kernel(of_ref, ob_ref, x_ref, g_ref, wz_ref, nw_ref, w_ref, o_ref, y_ref):
    nw = nw_ref[...]
    x = x_ref[...]
    h16 = _norm_rows(x, g_ref[...]).astype(BF16)
    for h in range(GDN_NV):
        sl = slice(h * GDN_DH, (h + 1) * GDN_DH)
        if h % 2 == 0:
            z2 = jnp.dot(h16, wz_ref[:, h * GDN_DH:(h + 2) * GDN_DH], preferred_element_type=F32)
        z = z2[:, (h % 2) * GDN_DH:(h % 2 + 1) * GDN_DH]
        o = of_ref[:, sl] + ob_ref[:, sl]
        o = o * lax.rsqrt(jnp.mean(o * o, axis=-1, keepdims=True) + RMS_EPS) * nw
        y_ref[:, sl] = (o * (z * jax.nn.sigmoid(z))).astype(BF16)
    o_ref[...] = x + jnp.dot(y_ref[...], w_ref[...], preferred_element_type=F32)


def _gdn_out(o_f, o_b, x, gain, w_z, norm_w, w, *, tm=512):
    m, kv = o_f.shape
    d = x.shape[1]
    n = w.shape[1]
    tm = min(tm, m)
    row = lambda width: pl.BlockSpec((tm, width), lambda i: (i, 0))
    whole = lambda a: pl.BlockSpec(a.shape, lambda i: (0, 0))
    args = [gain.astype(F32).reshape(1, d), w_z.astype(BF16), norm_w.astype(F32).reshape(1, GDN_DH), w.astype(BF16)]
    return pl.pallas_call(
        _gdn_out_kernel,
        grid=(m // tm,),
        in_specs=[row(kv), row(kv), row(d)] + [whole(a) for a in args],
        out_specs=row(n),
        out_shape=jax.ShapeDtypeStruct((m, n), F32),
        scratch_shapes=[pltpu.VMEM((tm, kv), BF16)],
        compiler_params=_cparams(("parallel",)),
        name="gdn_out",
    )(o_f, o_b, x, *args)


def _swiglu_tile(h, wg_ref, wu_ref, wd_ref, lead=()):
    tf = wg_ref.shape[-1]
    sub = 2 * LANES if tf % (2 * LANES) == 0 else tf
    y = None
    prev = None

    def finish(a, u, c0):
        act = (a * jax.nn.sigmoid(a) * u).astype(BF16)
        return jnp.dot(act, wd_ref[lead + (slice(c0, c0 + sub), slice(None))], preferred_element_type=F32)

    for c0 in range(0, tf, sub):
        cols = lead + (slice(None), slice(c0, c0 + sub))
        a = jnp.dot(h, wg_ref[cols], preferred_element_type=F32)
        u = jnp.dot(h, wu_ref[cols], preferred_element_type=F32)
        if prev is not None:
            part = finish(*prev)
            y = part if y is None else y + part
        prev = (a, u, c0)
    part = finish(*prev)
    return part if y is None else y + part


def _ffn_kernel(x_ref, g_ref, wg_ref, wu_ref, wd_ref, gf_ref, o_ref, h_ref, acc_ref, *, final_norm):
    f = pl.program_id(1)

    @pl.when(f == 0)
    def _():
        h_ref[...] = _norm_rows(x_ref[...], g_ref[...]).astype(BF16)
        acc_ref[...] = jnp.zeros_like(acc_ref)

    acc_ref[...] += _swiglu_tile(h_ref[...], wg_ref, wu_ref, wd_ref)

    @pl.when(f == pl.num_programs(1) - 1)
    def _():
        out = x_ref[...] + acc_ref[...]
        if final_norm:
            out = _norm_rows(out, gf_ref[...])
        o_ref[...] = out


def _ffn(x, gain, w_gate, w_up, w_down, *, final_gain=None, tm=1024, tf=1792):
    m, d = x.shape
    ff = w_gate.shape[1]
    tm = min(tm, m)
    tf = min(tf, ff)
    final_norm = final_gain is not None
    gf = (final_gain if final_norm else gain).astype(F32).reshape(1, d)
    return pl.pallas_call(
        functools.partial(_ffn_kernel, final_norm=final_norm),
        grid=(m // tm, ff // tf),
        in_specs=[pl.BlockSpec((tm, d), lambda i, f: (i, 0)),
                  pl.BlockSpec((1, d), lambda i, f: (0, 0)),
                  pl.BlockSpec((d, tf), lambda i, f: (0, f)),
                  pl.BlockSpec((d, tf), lambda i, f: (0, f)),
                  pl.BlockSpec((tf, d), lambda i, f: (f, 0)),
                  pl.BlockSpec((1, d), lambda i, f: (0, 0))],
        out_specs=pl.BlockSpec((tm, d), lambda i, f: (i, 0)),
        out_shape=jax.ShapeDtypeStruct((m, d), F32),
        scratch_shapes=[pltpu.VMEM((tm, d), BF16), pltpu.VMEM((tm, d), F32)],
        compiler_params=_cparams(("parallel", "arbitrary")),
        name="ffn_mixer",
    )(x, gain.astype(F32).reshape(1, d), w_gate.astype(BF16), w_up.astype(BF16), w_down.astype(BF16), gf)


def _moe_kernel(x_ref, g_ref, r_ref, rb_ref, wg_ref, wu_ref, wd_ref, gf_ref, o_ref,
                h_ref, acc_ref, gates_ref, posc_ref, posr_ref, hx_ref, y_ref, nblk_ref, *,
                n_exp, final_norm, rb):
    e = pl.program_id(1)
    f = pl.program_id(2)
    tm = x_ref.shape[0]
    lane = lax.broadcasted_iota(jnp.int32, (tm, LANES), 1)

    @pl.when((e == 0) & (f == 0))
    def _():
        h = _norm_rows(x_ref[...], g_ref[...])
        h_ref[...] = h.astype(BF16)
        acc_ref[...] = jnp.zeros_like(acc_ref)
        h_hi = h_ref[...]
        h_lo = (h - h_hi.astype(F32)).astype(BF16)
        r_hi = r_ref[0]
        logits = (jnp.dot(h_hi, r_hi, preferred_element_type=F32)
                  + jnp.dot(h_hi, r_ref[1], preferred_element_type=F32)
                  + jnp.dot(h_lo, r_hi, preferred_element_type=F32)) + rb_ref[...]
        logits = jnp.where(lane < n_exp, logits, -jnp.inf)
        m1 = jnp.max(logits, axis=1, keepdims=True)
        i1 = jnp.min(jnp.where(logits == m1, lane, LANES), axis=1, keepdims=True)
        rest = jnp.where(lane == i1, -jnp.inf, logits)
        m2 = jnp.max(rest, axis=1, keepdims=True)
        i2 = jnp.min(jnp.where(rest == m2, lane, LANES), axis=1, keepdims=True)
        e2 = jnp.exp(m2 - m1)
        w1 = 1.0 / (1.0 + e2)
        gates_ref[...] = jnp.where(lane == i1, w1, 0.0) + jnp.where(lane == i2, e2 * w1, 0.0)
        chosen = jnp.where((lane == i1) | (lane == i2), 1.0, 0.0)
        ri = lax.broadcasted_iota(jnp.int32, (tm, tm), 0)
        ci = lax.broadcasted_iota(jnp.int32, (tm, tm), 1)
        before = jnp.where(ci < ri, 1.0, 0.0).astype(BF16)
        rank = jnp.dot(before, chosen.astype(BF16), preferred_element_type=F32)
        posc = jnp.where(chosen > 0.5, rank, -1.0)
        posc_ref[...] = posc
        posr_ref[...] = posc.T
        for ee in range(n_exp):
            cnt = jnp.sum(chosen[:, ee:ee + 1]).astype(jnp.int32)
            nblk_ref[ee] = (cnt + rb - 1) // rb

    nblk = nblk_ref[e]

    @pl.when(f == 0)
    def _():
        pos_row = posr_ref[pl.ds(e, 1), :]
        h = h_ref[...]

        def gather(b, carry):
            r0 = pl.multiple_of(b * 2 * rb, 2 * rb)
            rows = (lax.broadcasted_iota(jnp.int32, (2 * rb, 1), 0) + r0).astype(F32)
            onehot = jnp.where(pos_row == rows, 1.0, 0.0).astype(BF16)
            hx_ref[pl.ds(r0, 2 * rb), :] = jnp.dot(onehot, h, preferred_element_type=F32).astype(BF16)
            y_ref[pl.ds(r0, 2 * rb), :] = jnp.zeros((2 * rb, y_ref.shape[1]), F32)
            return carry

        lax.fori_loop(0, (nblk + 1) // 2, gather, 0)

    def expert(r0, rows):
        y_ref[pl.ds(r0, rows), :] += _swiglu_tile(hx_ref[pl.ds(r0, rows), :], wg_ref, wu_ref, wd_ref, lead=(0,))

    def expert_pair(b, carry):
        expert(pl.multiple_of(b * 2 * rb, 2 * rb), 2 * rb)
        return carry

    lax.fori_loop(0, nblk // 2, expert_pair, 0)

    @pl.when(nblk % 2 == 1)
    def _():
        expert(pl.multiple_of((nblk - 1) * rb, rb), rb)

    @pl.when(f == pl.num_programs(2) - 1)
    def _():
        pos_col = jnp.sum(jnp.where(lane == e, posc_ref[...], 0.0), axis=1, keepdims=True)
        gate = jnp.sum(jnp.where(lane == e, gates_ref[...], 0.0), axis=1, keepdims=True)

        def scatter(b, carry):
            r0 = pl.multiple_of(b * 2 * rb, 2 * rb)
            cols = (lax.broadcasted_iota(jnp.int32, (1, 2 * rb), 1) + r0).astype(F32)
            onehot = jnp.where(pos_col == cols, 1.0, 0.0).astype(BF16)
            back = jnp.dot(onehot, y_ref[pl.ds(r0, 2 * rb), :].astype(BF16), preferred_element_type=F32)
            acc_ref[...] += gate * back
            return carry

        lax.fori_loop(0, (nblk + 1) // 2, scatter, 0)

    @pl.when((e == n_exp - 1) & (f == pl.num_programs(2) - 1))
    def _():
        out = x_ref[...] + acc_ref[...]
        if final_norm:
            out = _norm_rows(out, gf_ref[...])
        o_ref[...] = out


def _moe(x, gain, router, router_bias, w_gate, w_up, w_down, *, final_gain=None, tm=1024, tf=1792, rb=128):
    m, d = x.shape
    n_exp, _, ff = w_gate.shape
    tm = min(tm, m)
    tf = tf if ff % tf == 0 else min(512, ff)
    assert (tm // rb) % 2 == 0
    r = jnp.pad(router.astype(F32), ((0, 0), (0, LANES - n_exp)))
    r_hi = r.astype(BF16)
    r = jnp.stack([r_hi, (r - r_hi.astype(F32)).astype(BF16)])
    rbias = jnp.pad(router_bias.astype(F32).reshape(1, n_exp), ((0, 0), (0, LANES - n_exp)))
    final_norm = final_gain is not None
    gf = (final_gain if final_norm else gain).astype(F32).reshape(1, d)
    return pl.pallas_call(
        functools.partial(_moe_kernel, n_exp=n_exp, final_norm=final_norm, rb=rb),
        grid=(m // tm, n_exp, ff // tf),
        in_specs=[pl.BlockSpec((tm, d), lambda i, e, f: (i, 0), pipeline_mode=pl.Buffered(1)),
                  pl.BlockSpec((1, d), lambda i, e, f: (0, 0)),
                  pl.BlockSpec((2, d, LANES), lambda i, e, f: (0, 0, 0)),
                  pl.BlockSpec((1, LANES), lambda i, e, f: (0, 0)),
                  pl.BlockSpec((1, d, tf), lambda i, e, f: (e, 0, f)),
                  pl.BlockSpec((1, d, tf), lambda i, e, f: (e, 0, f)),
                  pl.BlockSpec((1, tf, d), lambda i, e, f: (e, f, 0)),
                  pl.BlockSpec((1, d), lambda i, e, f: (0, 0))],
        out_specs=pl.BlockSpec((tm, d), lambda i, e, f: (i, 0)),
        out_shape=jax.ShapeDtypeStruct((m, d), F32),
        scratch_shapes=[pltpu.VMEM((tm, d), BF16), pltpu.VMEM((tm, d), F32), pltpu.VMEM((tm, LANES), F32),
                        pltpu.VMEM((tm, LANES), F32), pltpu.VMEM((LANES, tm), F32),
                        pltpu.VMEM((tm, d), BF16), pltpu.VMEM((tm, d), F32), pltpu.SMEM((n_exp,), jnp.int32)],
        compiler_params=_cparams(("parallel", "arbitrary", "arbitrary")),
        name="moe_mixer",
    )(x, gain.astype(F32).reshape(1, d), r, rbias, w_gate.astype(BF16), w_up.astype(BF16),
      w_down.astype(BF16), gf)


def _hyena_layer(x, gain, lp, ls, nb_p, nb_s, w_in, conv_w, f_w1, f_b1, f_w2, f_b2, f_w3, f_b3, f_w4,
                 f_freq, skip, w_out, *, starts, ends):
    parts = x if isinstance(x, tuple) else (x,)
    m, d = sum(p.shape[0] for p in parts), parts[0].shape[1]
    assert nb_p == 1 and lp % (2 * DFT_N1) == 0 and ls % (2 * DFT_N1) == 0
    hp, hs = lp // DFT_N1, ls // DFT_N1
    ut = _proj(x, gain, w_in, conv_w, transpose_out=True, starts=starts, ends=ends)
    u3 = ut.reshape(3 * d, m // DFT_N1, DFT_N1)
    filt = (f_w1, f_b1, f_w2, f_b2, f_w3, f_b3, f_w4, f_freq)
    kp = _spectrum(_hyena_taps(lp, *filt, d))
    ks = _spectrum(_hyena_taps(ls, *filt, d))
    conv = functools.partial(_fftconv, d_model=d, hp=hp, hs=hs, nb=nb_s)
    z1 = conv(u3, 0, u3, d, skip[0], kp, 0, ks, 0)
    z2 = conv(z1, 0, u3, 2 * d, skip[1], kp, d, ks, d)
    return _outproj_t(z2.reshape(d, m), w_out, x)


def _gdn_layer(x, gain, w_in, conv_w, a_log, dt_bias, norm_w, w_out, *, starts, ends):
    key, val = GDN_NK * GDN_DH, GDN_NV * GDN_DH
    cd = 2 * key + val
    qkv = _proj(x, gain, w_in[:, :cd], conv_w, act="silu", starts=starts, ends=ends)
    w_ba = jnp.pad(w_in[:, cd + val:], ((0, 0), (0, LANES - 4 * GDN_NV)))
    ba = _proj(x, gain, w_ba)
    o_f, o_b = _gdn_scan(qkv, ba, a_log, dt_bias, starts=starts, ends=ends)
    return _gdn_out(o_f, o_b, x, gain, w_in[:, cd:cd + val], norm_w, w_out)


def kernel(x_prompt, x_sample, norm_mix, norm_ffn, norm_final, hy_w_in, hy_conv, hy_f_w1, hy_f_b1, hy_f_w2, hy_f_b2, hy_f_w3, hy_f_b3, hy_f_w4, hy_f_freq, hy_skip, hy_w_out, gdn_w_in, gdn_conv, gdn_a_log, gdn_dt_bias, gdn_norm, gdn_w_out, ffn_w_gate, ffn_w_up, ffn_w_down, moe_router, moe_router_bias, moe_w_gate, moe_w_up, moe_w_down):
    bp, lp, d = x_prompt.shape
    bs, ls, _ = x_sample.shape
    x = (x_prompt.reshape(bp * lp, d), x_sample.reshape(bs * ls, d))
    starts = tuple(b * lp for b in range(bp)) + tuple(bp * lp + b * ls for b in range(bs))
    ends = tuple(s + lp for s in starts[:bp]) + tuple(s + ls for s in starts[bp:])
    depth = norm_mix.shape[0]
    for i in range(depth):
        j = i // 2
        last = i == depth - 1
        if i % 2 == 0:
            x = _hyena_layer(x, norm_mix[i], lp, ls, bp, bs, hy_w_in[j], hy_conv[j], hy_f_w1[j], hy_f_b1[j],
                             hy_f_w2[j], hy_f_b2[j], hy_f_w3[j], hy_f_b3[j], hy_f_w4[j], hy_f_freq[j],
                             hy_skip[j], hy_w_out[j], starts=starts, ends=ends)
            x = _ffn(x, norm_ffn[i], ffn_w_gate[j], ffn_w_up[j], ffn_w_down[j],
                     final_gain=norm_final if last else None)
        else:
            if isinstance(x, tuple):
                x = jnp.concatenate(x, axis=0)
            x = _gdn_layer(x, norm_mix[i], gdn_w_in[j], gdn_conv[j], gdn_a_log[j], gdn_dt_bias[j],
                           gdn_norm[j], gdn_w_out[j], starts=starts, ends=ends)
            x = _moe(x, norm_ffn[i], moe_router[j], moe_router_bias[j], moe_w_gate[j], moe_w_up[j],
                     moe_w_down[j], final_gain=norm_final if last else None)
    y_prompt = x[:bp * lp].reshape(bp, lp, d)
    y_sample = x[bp * lp:].reshape(bs, ls, d)
    return (y_prompt, y_sample)
```

```python
import functools
import math

import numpy as np
import jax
import jax.numpy as jnp
from jax import lax
from jax.experimental import pallas as pl
from jax.experimental.pallas import tpu as pltpu

F32 = jnp.float32
BF16 = jnp.bfloat16

RMS_EPS = 1e-6
HY_BANDS = 16
HY_FAST_PCT = 0.3
HY_SLOW_PCT = 1.5
HY_TARGET = 1e-2
GDN_NK = 8
GDN_NV = 16
GDN_DH = 128
TOP_K = 2

LANES = 128
HALO = 16
DFT_N1 = 256
GDN_CHUNK = 64
GDN_TILE = 256
VMEM_LIMIT = 56 * 1024 * 1024

HIGHEST = lax.Precision.HIGHEST


def _cparams(sem):
    return pltpu.CompilerParams(dimension_semantics=sem, vmem_limit_bytes=VMEM_LIMIT)


def _in_set(v, values):
    r = v == values[0]
    for b in values[1:]:
        r = jnp.logical_or(r, v == b)
    return r


def _norm_rows(x, g):
    ms = jnp.mean(x * x, axis=-1, keepdims=True)
    return x * lax.rsqrt(ms + RMS_EPS) * g


def _proj_conv_kernel(*refs, tm, part_tiles, starts, ends, act, transpose_out):
    n_parts = len(part_tiles)
    g_ref, w_ref, cw_ref, o_ref, h_ref = refs[3 * n_parts:]
    i = pl.program_id(0)
    first = 0
    for p, tiles in enumerate(part_tiles):
        xp_ref, x_ref, xn_ref = refs[3 * p:3 * p + 3]

        @pl.when((pl.program_id(1) == 0) & (i >= first) & (i < first + tiles))
        def _(xp_ref=xp_ref, x_ref=x_ref, xn_ref=xn_ref):
            g = g_ref[...]
            row0 = i * tm
            hp = jnp.where(_in_set(row0, starts), 0.0, _norm_rows(xp_ref[...], g))
            hn = jnp.where(_in_set(row0 + tm, ends), 0.0, _norm_rows(xn_ref[...], g))
            h_ref[0:HALO, :] = hp.astype(BF16)
            h_ref[HALO:HALO + tm, :] = _norm_rows(x_ref[...], g).astype(BF16)
            h_ref[HALO + tm:HALO + tm + HALO, :] = hn.astype(BF16)

        first += tiles

    n = tm + 2 * HALO
    tn = w_ref.shape[1]
    sub = min(tn, 2 * LANES)

    def finish(p, c0):
        up = pltpu.roll(p, 1, axis=0)
        dn = pltpu.roll(p, n - 1, axis=0)
        cw = cw_ref[:, c0:c0 + sub]
        y = (cw[0:1, :] * up[HALO:HALO + tm, :] + cw[1:2, :] * p[HALO:HALO + tm, :]
             + cw[2:3, :] * dn[HALO:HALO + tm, :])
        if act == "silu":
            y = y * jax.nn.sigmoid(y)
        if transpose_out:
            o_ref[c0:c0 + sub, :] = y.T
        else:
            o_ref[:, c0:c0 + sub] = y

    h = h_ref[...]
    prev = None
    for c0 in range(0, tn, sub):
        p = jnp.dot(h, w_ref[:, c0:c0 + sub], preferred_element_type=F32)
        if prev is not None:
            finish(*prev)
        prev = (p, c0)
    finish(*prev)


def _proj_kernel(x_ref, g_ref, w_ref, o_ref, h_ref):
    @pl.when(pl.program_id(1) == 0)
    def _():
        h_ref[...] = _norm_rows(x_ref[...], g_ref[...]).astype(BF16)

    o_ref[...] = jnp.dot(h_ref[...], w_ref[...], preferred_element_type=F32)


def _proj(x, gain, w, conv_w=None, *, act=None, transpose_out=False, starts=(), ends=(),
          tm=1024, tn=512):
    parts = x if isinstance(x, tuple) else (x,)
    m = sum(p.shape[0] for p in parts)
    d = parts[0].shape[1]
    n = w.shape[1]
    tm = min(tm, m)
    if conv_w is not None:
        wide = [c for c in range(2 * LANES, 4 * tn + 1, 2 * LANES) if n % c == 0]
        tn = wide[-1] if wide else tn
    tn = min(tn, n)
    grid = (m // tm, n // tn)
    gain = gain.reshape(1, d).astype(F32)
    w = w.astype(BF16)
    if conv_w is None:
        assert len(parts) == 1
        return pl.pallas_call(
            _proj_kernel,
            grid=grid,
            in_specs=[pl.BlockSpec((tm, d), lambda i, j: (i, 0)),
                      pl.BlockSpec((1, d), lambda i, j: (0, 0)),
                      pl.BlockSpec((d, tn), lambda i, j: (0, j))],
            out_specs=pl.BlockSpec((tm, tn), lambda i, j: (i, j)),
            out_shape=jax.ShapeDtypeStruct((m, n), F32),
            scratch_shapes=[pltpu.VMEM((tm, d), BF16)],
            compiler_params=_cparams(("parallel", "arbitrary")),
            name="proj",
        )(parts[0], gain, w)
    hb = tm // HALO
    part_tiles = tuple(p.shape[0] // tm for p in parts)
    assert all(p.shape[0] % tm == 0 for p in parts)
    assert all(sum(part_tiles[:k + 1]) * tm in ends for k in range(len(parts)))
    kern = functools.partial(_proj_conv_kernel, tm=tm, part_tiles=part_tiles, starts=tuple(starts),
                             ends=tuple(ends), act=act, transpose_out=transpose_out)
    x_specs, x_args, first = [], [], 0
    for p, tiles in zip(parts, part_tiles):
        last = p.shape[0] // HALO - 1
        loc = lambda i, first=first, tiles=tiles: jnp.clip(i - first, 0, tiles - 1)
        x_specs += [pl.BlockSpec((HALO, d), lambda i, j, loc=loc: (jnp.maximum(loc(i) * hb - 1, 0), 0)),
                    pl.BlockSpec((tm, d), lambda i, j, loc=loc: (loc(i), 0)),
                    pl.BlockSpec((HALO, d), lambda i, j, loc=loc, last=last: (jnp.minimum((loc(i) + 1) * hb, last), 0))]
        x_args += [p, p, p]
        first += tiles
    if transpose_out:
        out_spec = pl.BlockSpec((tn, tm), lambda i, j: (j, i))
        out_shape = jax.ShapeDtypeStruct((n, m), F32)
    else:
        out_spec = pl.BlockSpec((tm, tn), lambda i, j: (i, j))
        out_shape = jax.ShapeDtypeStruct((m, n), F32)
    return pl.pallas_call(
        kern,
        grid=grid,
        in_specs=x_specs + [pl.BlockSpec((1, d), lambda i, j: (0, 0)),
                            pl.BlockSpec((d, tn), lambda i, j: (0, j)),
                            pl.BlockSpec((3, tn), lambda i, j: (0, j))],
        out_specs=out_spec,
        out_shape=out_shape,
        scratch_shapes=[pltpu.VMEM((tm + 2 * HALO, d), BF16)],
        compiler_params=_cparams(("parallel", "arbitrary")),
        name="proj_conv",
    )(*x_args, gain, w, conv_w.astype(F32))


def _dft_rows(n2):
    return n2 // 2 + 8


@functools.lru_cache(maxsize=None)
def _dft_consts(n2):
    n1 = DFT_N1
    n = n1 * n2
    h = n2 // 2
    kept = h + 1
    hr = _dft_rows(n2)
    k1 = np.arange(n1)
    f1 = np.exp(-2j * np.pi * np.outer(k1, k1) / n1)
    f1_fwd = np.block([[f1.real, f1.imag], [-f1.imag, f1.real]])
    f1_inv = np.block([[f1.real, -f1.imag], [f1.imag, f1.real]])
    k2 = np.arange(n2)
    f2 = np.exp(-2j * np.pi * np.outer(k2, k2) / n2)
    pad_rows = lambda a: np.concatenate([a[:kept], np.zeros((hr - kept,) + a.shape[1:])], axis=0)
    tw = pad_rows(np.exp(-2j * np.pi * np.outer(k2, k1) / n))
    lf_full = np.concatenate([pad_rows(f2.real), pad_rows(f2.imag)], axis=0)
    lf_half = lf_full[:, :h]
    weight = np.where((k2 == 0) | (k2 == h), 1.0, 2.0)[None, :] / n
    li_re = pad_rows((f2.real[:h, :] * weight).T).T
    li_im = pad_rows((f2.imag[:h, :] * weight).T).T
    li_half = np.concatenate([li_re, li_im], axis=1)
    return dict(f1_fwd=f1_fwd, f1_inv=f1_inv, twre=tw.real, twim=tw.imag,
                lf_half=lf_half, lf_full=lf_full, li_half=li_half)


def _consts_dev(n2):
    c = _dft_consts(n2)
    return dict(
        f1_fwd=jnp.asarray(c["f1_fwd"], BF16), f1_inv=jnp.asarray(c["f1_inv"], BF16),
        twre=jnp.asarray(c["twre"], F32), twim=jnp.asarray(c["twim"], F32),
        lf_half=jnp.asarray(c["lf_half"], BF16), lf_full=jnp.asarray(c["lf_full"], BF16),
        li_half=jnp.asarray(c["li_half"], BF16))


def _dft_fwd(sig, lf, twre, twim, f1f, a_scr):
    ns, two_hr, _ = a_scr.shape
    hr = two_hr // 2
    for s in range(ns):
        a_scr[s] = jnp.dot(lf, sig(s).astype(BF16), preferred_element_type=F32)
    are = a_scr[:, 0:hr, :]
    aim = a_scr[:, hr:two_hr, :]
    bre = (are * twre - aim * twim).reshape(ns * hr, DFT_N1)
    bim = (are * twim + aim * twre).reshape(ns * hr, DFT_N1)
    bcat = jnp.concatenate([bre, bim], axis=1).astype(BF16)
    cc = jnp.dot(bcat, f1f, preferred_element_type=F32)
    return cc[:, 0:DFT_N1], cc[:, DFT_N1:2 * DFT_N1]


def _dft_inv_real(yre, yim, li, twre, twim, f1i, d_scr, emit):
    ns, two_hr, _ = d_scr.shape
    hr = two_hr // 2
    ycat = jnp.concatenate([yre, yim], axis=1).astype(BF16)
    bb = jnp.dot(ycat, f1i, preferred_element_type=F32)
    bre = bb[:, 0:DFT_N1].reshape(ns, hr, DFT_N1)
    bim = bb[:, DFT_N1:2 * DFT_N1].reshape(ns, hr, DFT_N1)
    d_scr[:, 0:hr, :] = bre * twre + bim * twim
    d_scr[:, hr:two_hr, :] = bim * twre - bre * twim
    for s in range(ns):
        emit(s, jnp.dot(li, d_scr[s].astype(BF16), preferred_element_type=F32))


def _taps_kernel(bands_ref, w1t_ref, w1c_ref, w1s_ref, b1_ref, w2_ref, b2_ref, w3_ref, b3_ref,
                 fq_ref, w4_ref, dl_ref, o_ref, h_ref, tm_ref, *, seq_len, tl):
    nt = pl.program_id(0)

    @pl.when(pl.program_id(1) == 0)
    def _():
        col = nt * tl + lax.broadcasted_iota(jnp.int32, (1, tl), 1)
        pos = jnp.where(col < seq_len, col, 2 * seq_len - col)
        posf = pos.astype(F32)
        t = posf * np.float32(1.0 / (seq_len - 1))
        ang = bands_ref[...] * (posf * np.float32(2.0 * math.pi / seq_len))
        fq = fq_ref[...]
        pre = (w1t_ref[...] * t
               + jnp.dot(w1c_ref[...], jnp.cos(ang), preferred_element_type=F32, precision=HIGHEST)
               - jnp.dot(w1s_ref[...], jnp.sin(ang), preferred_element_type=F32, precision=HIGHEST))
        h = jnp.sin(fq * (pre + b1_ref[...]))
        h = jnp.sin(fq * (jnp.dot(w2_ref[...], h, preferred_element_type=F32, precision=HIGHEST) + b2_ref[...]))
        h = jnp.sin(fq * (jnp.dot(w3_ref[...], h, preferred_element_type=F32, precision=HIGHEST) + b3_ref[...]))
        h_ref[...] = h.astype(BF16)
        tm_ref[0:1, :] = t
        tm_ref[1:2, :] = jnp.where(col == seq_len, 0.0, 1.0)

    taps = jnp.dot(w4_ref[...], h_ref[...], preferred_element_type=F32)
    window = jnp.exp(-(dl_ref[...] * tm_ref[0:1, :]))
    o_ref[...] = (taps * window * tm_ref[1:2, :]).astype(BF16)


def _hyena_taps(seq_len, f_w1, f_b1, f_w2, f_b2, f_w3, f_b3, f_w4, f_freq, d_model, *, tl=2048, td=512):
    fh = f_w1.shape[1]
    n_ord = f_w4.shape[1] // (2 * d_model)
    tl = min(tl, seq_len)
    ndt = d_model // td
    nlt = seq_len // tl
    col = lambda v: v.reshape(-1, 1).astype(F32)
    bands = jnp.linspace(1e-4, HY_BANDS - 1, HY_BANDS, dtype=F32).reshape(-1, 1)
    w1 = f_w1.astype(F32).T
    w4t = f_w4.astype(F32).reshape(fh, n_ord, 2, d_model).transpose(1, 2, 3, 0).reshape(n_ord * 2 * d_model, fh)
    w4t = w4t.astype(BF16)
    max_decay = math.log(HY_TARGET) / HY_FAST_PCT
    min_decay = math.log(HY_TARGET) / HY_SLOW_PCT
    deltas = jnp.abs(jnp.linspace(min_decay, max_decay, d_model, dtype=F32))
    deltas = jnp.tile(deltas, n_ord).reshape(-1, 1)
    small = lambda a: pl.BlockSpec(a.shape, lambda n, j: (0, 0))
    args = [bands, w1[:, 0:1], w1[:, 1:1 + HY_BANDS], w1[:, 1 + HY_BANDS:1 + 2 * HY_BANDS], col(f_b1),
            f_w2.astype(F32).T, col(f_b2), f_w3.astype(F32).T, col(f_b3), col(f_freq)]
    return pl.pallas_call(
        functools.partial(_taps_kernel, seq_len=seq_len, tl=tl),
        grid=(2 * nlt, n_ord * ndt),
        in_specs=[small(a) for a in args] + [
            pl.BlockSpec((td, fh), lambda n, j: ((j // ndt * 2 + (n >= nlt).astype(jnp.int32)) * ndt + j % ndt, 0)),
            pl.BlockSpec((td, 1), lambda n, j: (j, 0))],
        out_specs=pl.BlockSpec((td, tl), lambda n, j: (j, n)),
        out_shape=jax.ShapeDtypeStruct((n_ord * d_model, 2 * seq_len), BF16),
        scratch_shapes=[pltpu.VMEM((fh, tl), BF16), pltpu.VMEM((8, tl), F32)],
        compiler_params=_cparams(("parallel", "arbitrary")),
        name="hyena_taps",
    )(*args, w4t, deltas)


def _spectrum_kernel(x_ref, lf_ref, twre_ref, twim_ref, f1f_ref, re_ref, im_ref, a_scr):
    ct, hr, _ = re_ref.shape
    cre, cim = _dft_fwd(lambda s: x_ref[s], lf_ref[...], twre_ref[...][None], twim_ref[...][None],
                        f1f_ref[...], a_scr)
    re_ref[...] = cre.reshape(ct, hr, DFT_N1)
    im_ref[...] = cim.reshape(ct, hr, DFT_N1)


def _spectrum(taps, *, ct=32):
    c, n = taps.shape
    n2 = n // DFT_N1
    hr = _dft_rows(n2)
    cs = _consts_dev(n2)
    x = taps.reshape(c, n2, DFT_N1)
    blk = pl.BlockSpec((ct, hr, DFT_N1), lambda i: (i, 0, 0))
    full = lambda a: pl.BlockSpec(a.shape, lambda i: (0,) * a.ndim)
    consts = [cs["lf_full"], cs["twre"], cs["twim"], cs["f1_fwd"]]
    return pl.pallas_call(
        _spectrum_kernel,
        grid=(c // ct,),
        in_specs=[pl.BlockSpec((ct, n2, DFT_N1), lambda i: (i, 0, 0))] + [full(a) for a in consts],
        out_specs=[blk, blk],
        out_shape=[jax.ShapeDtypeStruct((c, hr, DFT_N1), F32)] * 2,
        scratch_shapes=[pltpu.VMEM((ct, 2 * hr, DFT_N1), F32)],
        compiler_params=_cparams(("parallel",)),
        name="hyena_spectrum",
    )(x, *consts)


def _fftconv_kernel(x_ref, gate_ref, skip_ref,
                    kpre_ref, kpim_ref, lfp_ref, lip_ref, twpre_ref, twpim_ref,
                    ksre_ref, ksim_ref, lfs_ref, lis_ref, twsre_ref, twsim_ref,
                    f1f_ref, f1i_ref, o_ref, ap_scr, dp_scr, as_scr, ds_scr, *, hp, hs, nb):
    ct = x_ref.shape[0]
    f1f = f1f_ref[...]
    f1i = f1i_ref[...]

    def run(rows, n_sig, kre, kim, lf, li, twre, twim, a_scr, d_scr):
        def sig(s):
            c, r0, h = rows(s)
            return x_ref[c, r0:r0 + h, :]

        cre, cim = _dft_fwd(sig, lf, twre, twim, f1f, a_scr)
        yre = cre * kre - cim * kim
        yim = cre * kim + cim * kre

        def emit(s, y):
            c, r0, h = rows(s)
            xs = x_ref[c, r0:r0 + h, :]
            o_ref[c, r0:r0 + h, :] = gate_ref[c, r0:r0 + h, :] * (y + skip_ref[c] * xs)

        _dft_inv_real(yre, yim, li, twre, twim, f1i, d_scr, emit)

    hrp = kpre_ref.shape[1]
    run(lambda s: (s, 0, hp), ct,
        kpre_ref[...].reshape(ct * hrp, DFT_N1), kpim_ref[...].reshape(ct * hrp, DFT_N1),
        lfp_ref[...], lip_ref[...], twpre_ref[...][None], twpim_ref[...][None], ap_scr, dp_scr)
    hrs = ksre_ref.shape[1]
    rep = lambda k: jnp.broadcast_to(k[:, None], (ct, nb, hrs, DFT_N1)).reshape(ct * nb * hrs, DFT_N1)
    run(lambda s: (s // nb, hp + (s % nb) * hs, hs), ct * nb,
        rep(ksre_ref[...]), rep(ksim_ref[...]),
        lfs_ref[...], lis_ref[...], twsre_ref[...][None], twsim_ref[...][None], as_scr, ds_scr)


def _fftconv(x, x_off, gate, gate_off, skip, kp, kp_off, ks, ks_off, *, d_model, hp, hs, nb, ct=8):
    r = hp + nb * hs
    cp, cs_ = _consts_dev(2 * hp), _consts_dev(2 * hs)
    hrp, hrs = _dft_rows(2 * hp), _dft_rows(2 * hs)
    cb = lambda off: (lambda i: (off // ct + i, 0, 0))
    full = lambda a: pl.BlockSpec(a.shape, lambda i: (0,) * a.ndim)
    kspec = lambda rows, off: pl.BlockSpec((ct, rows, DFT_N1), cb(off))
    skip3 = jnp.broadcast_to(skip.astype(F32).reshape(d_model, 1, 1), (d_model, 1, DFT_N1))
    pc = [cp["lf_half"], cp["li_half"], cp["twre"], cp["twim"]]
    sc = [cs_["lf_half"], cs_["li_half"], cs_["twre"], cs_["twim"]]
    return pl.pallas_call(
        functools.partial(_fftconv_kernel, hp=hp, hs=hs, nb=nb),
        grid=(d_model // ct,),
        in_specs=[pl.BlockSpec((ct, r, DFT_N1), cb(x_off)), pl.BlockSpec((ct, r, DFT_N1), cb(gate_off)),
                  pl.BlockSpec((ct, 1, DFT_N1), cb(0)),
                  kspec(hrp, kp_off), kspec(hrp, kp_off)] + [full(a) for a in pc]
                 + [kspec(hrs, ks_off), kspec(hrs, ks_off)] + [full(a) for a in sc]
                 + [full(cp["f1_fwd"]), full(cp["f1_inv"])],
        out_specs=pl.BlockSpec((ct, r, DFT_N1), cb(0)),
        out_shape=jax.ShapeDtypeStruct((d_model, r, DFT_N1), F32),
        scratch_shapes=[pltpu.VMEM((ct, 2 * hrp, DFT_N1), F32), pltpu.VMEM((ct, 2 * hrp, DFT_N1), F32),
                        pltpu.VMEM((ct * nb, 2 * hrs, DFT_N1), F32), pltpu.VMEM((ct * nb, 2 * hrs, DFT_N1), F32)],
        compiler_params=_cparams(("parallel",)),
        name="hyena_fftconv",
    )(x, gate, skip3, kp[0], kp[1], *pc, ks[0], ks[1], *sc, cp["f1_fwd"], cp["f1_inv"])


def _outproj_t_kernel(zt_ref, w_ref, *refs, part_tiles):
    res_refs, o_ref = refs[:-1], refs[-1]
    i = pl.program_id(0)
    z = zt_ref[...].astype(BF16)
    y = lax.dot_general(z, w_ref[...], (((0,), (0,)), ((), ())), preferred_element_type=F32)
    first = 0
    for res_ref, tiles in zip(res_refs, part_tiles):
        @pl.when((i >= first) & (i < first + tiles))
        def _(res_ref=res_ref):
            o_ref[...] = res_ref[...] + y

        first += tiles


def _outproj_t(zt, w, res, *, tm=512):
    parts = res if isinstance(res, tuple) else (res,)
    k, m = zt.shape
    n = w.shape[1]
    tm = min(tm, m)
    part_tiles = tuple(p.shape[0] // tm for p in parts)
    assert all(p.shape[0] % tm == 0 for p in parts) and sum(part_tiles) * tm == m
    res_specs, first = [], 0
    for tiles in part_tiles:
        res_specs.append(pl.BlockSpec((tm, n), lambda i, first=first, tiles=tiles: (jnp.clip(i - first, 0, tiles - 1), 0)))
        first += tiles
    return pl.pallas_call(
        functools.partial(_outproj_t_kernel, part_tiles=part_tiles),
        grid=(m // tm,),
        in_specs=[pl.BlockSpec((k, tm), lambda i: (0, i)),
                  pl.BlockSpec((k, n), lambda i: (0, 0))] + res_specs,
        out_specs=pl.BlockSpec((tm, n), lambda i: (i, 0)),
        out_shape=jax.ShapeDtypeStruct((m, n), F32),
        compiler_params=_cparams(("parallel",)),
        name="hyena_outproj",
    )(zt, w.astype(BF16), *parts)


def _softplus(x):
    return jnp.maximum(x, 0.0) + jnp.log1p(jnp.exp(-jnp.abs(x)))


def _gdn_kernel(qf_ref, kf_ref, vf_ref, baf_ref, batf_ref, qb_ref, kb_ref, vb_ref, bab_ref, batb_ref,
                alog_r_ref, dtb_r_ref, alog_c_ref, dtb_c_ref, of_ref, ob_ref, sf_ref, sb_ref, *,
                n_steps, f_resets, b_resets, n_pairs, n_sub):
    hp = pl.program_id(0)
    step = pl.program_id(1)
    t = GDN_TILE
    c = GDN_CHUNK
    nc = t // c
    dh = GDN_DH

    @pl.when(_in_set(step, f_resets))
    def _():
        sf_ref[...] = jnp.zeros_like(sf_ref)

    @pl.when(_in_set(n_steps - 1 - step, b_resets))
    def _():
        sb_ref[...] = jnp.zeros_like(sb_ref)

    ri = lax.broadcasted_iota(jnp.int32, (t, t), 0)
    ci = lax.broadcasted_iota(jnp.int32, (t, t), 1)
    same = (ri // c) == (ci // c)
    rw = lax.broadcasted_iota(jnp.int32, (c, t), 0)
    lw = lax.broadcasted_iota(jnp.int32, (c, t), 1)
    lblk = lw // c
    lsub = lw % c
    rsub = lax.broadcasted_iota(jnp.int32, (t, LANES), 0) % c
    lane = lax.broadcasted_iota(jnp.int32, (t, LANES), 1)
    lrow = lax.broadcasted_iota(jnp.int32, (8, t), 1) % c

    def diag_to_wide(full):
        parts = [jnp.where(lblk == j, full[j * c:(j + 1) * c, :], 0.0) for j in range(nc)]
        return functools.reduce(lambda x, y: x + y, parts)

    def wide_to_diag16(wide):
        return jnp.where(same, jnp.concatenate([wide] * nc, axis=0), 0.0).astype(BF16)

    def l2n(x):
        return x * lax.rsqrt(jnp.sum(x * x, axis=-1, keepdims=True) + RMS_EPS)

    def mm(a, b):
        return jnp.dot(a, b, preferred_element_type=F32)

    dirs = [(0, qf_ref, kf_ref, vf_ref, baf_ref, batf_ref), (1, qb_ref, kb_ref, vb_ref, bab_ref, batb_ref)]

    def stage1_dir(d, r0):
        _, _, _, v_ref, ba_ref, bat_ref = dirs[d]
        back = d == 1
        ba = ba_ref[r0:r0 + t, :]
        g_all = -jnp.exp(alog_r_ref[...]) * _softplus(ba + dtb_r_ref[...])
        sh = 1
        while sh < c:
            if back:
                g_all = g_all + jnp.where(rsub < c - sh, pltpu.roll(g_all, t - sh, axis=0), 0.0)
            else:
                g_all = g_all + jnp.where(rsub >= sh, pltpu.roll(g_all, sh, axis=0), 0.0)
            sh *= 2
        return dict(back=back, incl=(lsub >= rw) if back else (lsub <= rw),
                    strict=(lsub > rw) if back else (lsub < rw),
                    beta_all=jax.nn.sigmoid(ba), g_all=g_all, bat_ref=bat_ref, v_ref=v_ref, r0=r0)

    def stage1_pair(d, pp, r0):
        _, q_ref, k_ref, _, _, _ = dirs[d]
        q = l2n(q_ref[r0:r0 + t, pp * dh:(pp + 1) * dh]) * np.float32(GDN_DH ** -0.5)
        k = l2n(k_ref[r0:r0 + t, pp * dh:(pp + 1) * dh])
        k16 = k.astype(BF16)
        qk_kk = lax.dot_general(jnp.concatenate([q.astype(BF16), k16], axis=0), k16, (((1,), (1,)), ((), ())),
                                preferred_element_type=F32)
        return dict(q=q, k=k, qk=diag_to_wide(qk_kk[0:t]), kk=diag_to_wide(qk_kk[t:2 * t]))

    def stage2(d, pd, pr, ph):
        back = pd["back"]
        head = 2 * n_pairs * hp + ph
        jb = d * GDN_NV + head
        ja = 2 * GDN_NV + d * GDN_NV + head
        beta_c = jnp.sum(jnp.where(lane == jb, pd["beta_all"], 0.0), axis=1, keepdims=True)
        gc_c = jnp.sum(jnp.where(lane == ja, pd["g_all"], 0.0), axis=1, keepdims=True)
        a_row = pd["bat_ref"][pl.ds(ja, 1), pd["r0"]:pd["r0"] + t]
        g_row = -jnp.exp(alog_c_ref[pl.ds(ja, 1), :]) * _softplus(a_row + dtb_c_ref[pl.ds(ja, 1), :])
        g_row = jnp.broadcast_to(g_row, (8, t))
        sh = 1
        while sh < c:
            if back:
                g_row = g_row + jnp.where(lrow < c - sh, pltpu.roll(g_row, t - sh, axis=1), 0.0)
            else:
                g_row = g_row + jnp.where(lrow >= sh, pltpu.roll(g_row, sh, axis=1), 0.0)
            sh *= 2
        gc_r = g_row[0:1, :]
        incl = pd["incl"]
        decay = jnp.where(incl, jnp.exp(jnp.where(incl, diag_to_wide(gc_c) - gc_r, 0.0)), 0.0)
        p = jnp.where(pd["strict"], -(diag_to_wide(beta_c) * pr["kk"] * decay), 0.0)
        return dict(pd=pd, pr=pr, ph=ph, beta_c=beta_c, gc_c=gc_c, decay=decay, p=p)

    def stage3(group):
        for v in group:
            v["nn"] = v["p"]
            v["pm"] = mm(v["p"].astype(BF16), wide_to_diag16(v["p"]))
            yield
        m = 2
        while 2 * m < c:
            for v in group:
                both = mm(jnp.concatenate([v["pm"], v["nn"]], axis=0).astype(BF16), wide_to_diag16(v["pm"]))
                v["nn"] = v["nn"] + v["pm"] + both[c:2 * c]
                v["pm"] = both[0:c]
                yield
            m *= 2
        for v in group:
            v["nn"] = v["nn"] + v["pm"] + mm(v["nn"].astype(BF16), wide_to_diag16(v["pm"]))
            yield

    def stage4(group):
        for v in group:
            pd, pr, ph, beta_c, gc_c = v["pd"], v["pr"], v["ph"], v["beta_c"], v["gc_c"]
            vv = pd["v_ref"][pd["r0"]:pd["r0"] + t, ph * dh:(ph + 1) * dh]
            rhs = jnp.concatenate([vv * beta_c, pr["k"] * (beta_c * jnp.exp(gc_c))], axis=1)
            v["uw16"] = (rhs + mm(wide_to_diag16(v["nn"]), rhs.astype(BF16))).astype(BF16)
        for v in group:
            pd, pr, gc_c = v["pd"], v["pr"], v["gc_c"]
            qq = mm(wide_to_diag16(jnp.where(pd["incl"], pr["qk"] * v["decay"], 0.0)), v["uw16"])
            gc3 = gc_c.reshape(nc, c, 1)
            gl3 = gc3[:, 0:1, :] if pd["back"] else gc3[:, c - 1:c, :]
            kg16 = (pr["k"] * jnp.exp(jnp.broadcast_to(gl3, (nc, c, 1)).reshape(t, 1) - gc_c)).astype(BF16)
            v["qp16"] = (pr["q"] * jnp.exp(gc_c) - qq[:, dh:2 * dh]).astype(BF16)
            v["qq"] = qq
            v["gl3"] = gl3
            v["rp"] = [lax.dot_general(kg16[n * c:(n + 1) * c], v["uw16"][n * c:(n + 1) * c],
                                       (((0,), (0,)), ((), ())), preferred_element_type=F32)
                       for n in range(nc)]

    units = [(sub, d) for sub in range(n_sub) for d in range(2)]
    row0 = lambda sub, d: (n_sub - 1 - sub if d == 1 else sub) * t
    var = {}
    groups = {u: [] for u in units}

    def setup(u):
        sub, d = u
        pd = stage1_dir(d, row0(sub, d))
        yield
        for pp in range(n_pairs):
            pr = stage1_pair(d, pp, row0(sub, d))
            yield
            for ph in (2 * pp, 2 * pp + 1):
                var[(sub, d, ph)] = stage2(d, pd, pr, ph)
                groups[u].append(var[(sub, d, ph)])
                yield

    def run(main, filler=None, every=1):
        for i, _ in enumerate(main):
            if filler is not None and i % every == every - 1:
                next(filler, None)
        if filler is not None:
            for _ in filler:
                pass

    zero = jnp.zeros((dh, dh), F32)
    s_refs = [sf_ref, sb_ref]
    outs = [of_ref, ob_ref]
    state = [[s_refs[d][ph] for ph in range(2 * n_pairs)] for d in range(2)]

    def stage5(sub):
        for i in range(nc):
            for d in range(2):
                n = nc - 1 - i if d == 1 else i
                sl = slice(n * c, (n + 1) * c)
                so = slice(row0(sub, d) + n * c, row0(sub, d) + (n + 1) * c)
                for pp in range(n_pairs):
                    v0, v1 = var[(sub, d, 2 * pp)], var[(sub, d, 2 * pp + 1)]
                    s0, s1 = state[d][2 * pp], state[d][2 * pp + 1]
                    s_d = jnp.concatenate([jnp.concatenate([s0, zero], axis=1),
                                           jnp.concatenate([zero, s1], axis=1)], axis=0).astype(BF16)
                    lhs = jnp.concatenate([
                        jnp.concatenate([v0["rp"][n][:, dh:2 * dh], v1["rp"][n][:, dh:2 * dh]], axis=1).astype(BF16),
                        jnp.concatenate([v0["qp16"][sl], v1["qp16"][sl]], axis=1)], axis=0)
                    z = mm(lhs, s_d)
                    c0 = 2 * pp * dh
                    outs[d][so, c0:c0 + dh] = z[dh:dh + c, 0:dh] + v0["qq"][sl, 0:dh]
                    outs[d][so, c0 + dh:c0 + 2 * dh] = z[dh:dh + c, dh:2 * dh] + v1["qq"][sl, 0:dh]
                    state[d][2 * pp] = jnp.exp(v0["gl3"][n]) * s0 - z[0:dh, 0:dh] + v0["rp"][n][:, 0:dh]
                    state[d][2 * pp + 1] = jnp.exp(v1["gl3"][n]) * s1 - z[0:dh, dh:2 * dh] + v1["rp"][n][:, 0:dh]

    n_setup = 1 + 3 * n_pairs
    n_doubling = 2 * n_pairs * (c.bit_length() - 1)
    run(setup(units[0]))
    for i, u in enumerate(units):
        filler = setup(units[i + 1]) if i + 1 < len(units) else None
        run(stage3(groups[u]), filler, every=max(1, n_doubling // n_setup))
        if u[1] == 1:
            stage4(groups[(u[0], 0)])
            stage4(groups[(u[0], 1)])
            stage5(u[0])
    for d in range(2):
        for ph in range(2 * n_pairs):
            s_refs[d][ph] = state[d][ph]


def _gdn_scan(qkv, ba, a_log, dt_bias, *, starts, ends, n_pairs=4, n_sub=2):
    m = qkv.shape[0]
    t = n_sub * GDN_TILE
    assert all(s % t == 0 for s in starts) and all(e % t == 0 for e in ends)
    n_tiles = m // t
    bat = ba.T
    pad = lambda a: jnp.pad(a.astype(F32).reshape(-1), (2 * GDN_NV, LANES - 4 * GDN_NV))
    alog_r = pad(a_log).reshape(1, LANES)
    dtb_r = pad(dt_bias).reshape(1, LANES)
    alog_c = pad(a_log).reshape(LANES, 1)
    dtb_c = pad(dt_bias).reshape(LANES, 1)
    f_resets = tuple(s // t for s in starts)
    b_resets = tuple(e // t - 1 for e in ends)
    fwd = lambda h, s: s
    bwd = lambda h, s: n_tiles - 1 - s

    kw = n_pairs * GDN_DH
    key_blocks = GDN_NK // n_pairs

    def specs(tile):
        return [pl.BlockSpec((t, kw), lambda h, s: (tile(h, s), h)),
                pl.BlockSpec((t, kw), lambda h, s: (tile(h, s), key_blocks + h)),
                pl.BlockSpec((t, 2 * kw), lambda h, s: (tile(h, s), key_blocks + h)),
                pl.BlockSpec((t, LANES), lambda h, s: (tile(h, s), 0)),
                pl.BlockSpec((LANES, t), lambda h, s: (0, tile(h, s)))]

    small = lambda a: pl.BlockSpec(a.shape, lambda h, s: (0, 0))
    out = jax.ShapeDtypeStruct((m, GDN_NV * GDN_DH), F32)
    return pl.pallas_call(
        functools.partial(_gdn_kernel, n_steps=n_tiles, f_resets=f_resets, b_resets=b_resets, n_pairs=n_pairs,
                          n_sub=n_sub),
        grid=(key_blocks, n_tiles),
        in_specs=specs(fwd) + specs(bwd) + [small(alog_r), small(dtb_r), small(alog_c), small(dtb_c)],
        out_specs=[pl.BlockSpec((t, 2 * kw), lambda h, s: (s, h)),
                   pl.BlockSpec((t, 2 * kw), lambda h, s: (n_tiles - 1 - s, h))],
        out_shape=[out, out],
        scratch_shapes=[pltpu.VMEM((2 * n_pairs, GDN_DH, GDN_DH), F32),
                        pltpu.VMEM((2 * n_pairs, GDN_DH, GDN_DH), F32)],
        compiler_params=_cparams(("parallel", "arbitrary")),
        name="gdn_scan",
    )(qkv, qkv, qkv, ba, bat, qkv, qkv, qkv, ba, bat, alog_r, dtb_r, alog_c, dtb_c)


def _gdn_out_kernel(of_ref, ob_ref, x_ref, g_ref, wz_ref, nw_ref, w_ref, o_ref, y_ref):
    nw = nw_ref[...]
    x = x_ref[...]
    h16 = _norm_rows(x, g_ref[...]).astype(BF16)
    for h in range(GDN_NV):
        sl = slice(h * GDN_DH, (h + 1) * GDN_DH)
        if h % 2 == 0:
            z2 = jnp.dot(h16, wz_ref[:, h * GDN_DH:(h + 2) * GDN_DH], preferred_element_type=F32)
        z = z2[:, (h % 2) * GDN_DH:(h % 2 + 1) * GDN_DH]
        o = of_ref[:, sl] + ob_ref[:, sl]
        o = o * lax.rsqrt(jnp.mean(o * o, axis=-1, keepdims=True) + RMS_EPS) * nw
        y_ref[:, sl] = (o * (z * jax.nn.sigmoid(z))).astype(BF16)
    o_ref[...] = x + jnp.dot(y_ref[...], w_ref[...], preferred_element_type=F32)


def _gdn_out(o_f, o_b, x, gain, w_z, norm_w, w, *, tm=512):
    m, kv = o_f.shape
    d = x.shape[1]
    n = w.shape[1]
    tm = min(tm, m)
    row = lambda width: pl.BlockSpec((tm, width), lambda i: (i, 0))
    whole = lambda a: pl.BlockSpec(a.shape, lambda i: (0, 0))
    args = [gain.astype(F32).reshape(1, d), w_z.astype(BF16), norm_w.astype(F32).reshape(1, GDN_DH), w.astype(BF16)]
    return pl.pallas_call(
        _gdn_out_kernel,
        grid=(m // tm,),
        in_specs=[row(kv), row(kv), row(d)] + [whole(a) for a in args],
        out_specs=row(n),
        out_shape=jax.ShapeDtypeStruct((m, n), F32),
        scratch_shapes=[pltpu.VMEM((tm, kv), BF16)],
        compiler_params=_cparams(("parallel",)),
        name="gdn_out",
    )(o_f, o_b, x, *args)


def _swiglu_tile(h, wg_ref, wu_ref, wd_ref, lead=()):
    tf = wg_ref.shape[-1]
    sub = 2 * LANES if tf % (2 * LANES) == 0 else tf
    y = None
    prev = None

    def finish(a, u, c0):
        act = (a * jax.nn.sigmoid(a) * u).astype(BF16)
        return jnp.dot(act, wd_ref[lead + (slice(c0, c0 + sub), slice(None))], preferred_element_type=F32)

    for c0 in range(0, tf, sub):
        cols = lead + (slice(None), slice(c0, c0 + sub))
        a = jnp.dot(h, wg_ref[cols], preferred_element_type=F32)
        u = jnp.dot(h, wu_ref[cols], preferred_element_type=F32)
        if prev is not None:
            part = finish(*prev)
            y = part if y is None else y + part
        prev = (a, u, c0)
    part = finish(*prev)
    return part if y is None else y + part


def _ffn_kernel(x_ref, g_ref, wg_ref, wu_ref, wd_ref, gf_ref, o_ref, h_ref, acc_ref, *, final_norm):
    f = pl.program_id(1)

    @pl.when(f == 0)
    def _():
        h_ref[...] = _norm_rows(x_ref[...], g_ref[...]).astype(BF16)
        acc_ref[...] = jnp.zeros_like(acc_ref)

    acc_ref[...] += _swiglu_tile(h_ref[...], wg_ref, wu_ref, wd_ref)

    @pl.when(f == pl.num_programs(1) - 1)
    def _():
        out = x_ref[...] + acc_ref[...]
        if final_norm:
            out = _norm_rows(out, gf_ref[...])
        o_ref[...] = out


def _ffn(x, gain, w_gate, w_up, w_down, *, final_gain=None, tm=1024, tf=1792):
    m, d = x.shape
    ff = w_gate.shape[1]
    tm = min(tm, m)
    tf = min(tf, ff)
    final_norm = final_gain is not None
    gf = (final_gain if final_norm else gain).astype(F32).reshape(1, d)
    return pl.pallas_call(
        functools.partial(_ffn_kernel, final_norm=final_norm),
        grid=(m // tm, ff // tf),
        in_specs=[pl.BlockSpec((tm, d), lambda i, f: (i, 0)),
                  pl.BlockSpec((1, d), lambda i, f: (0, 0)),
                  pl.BlockSpec((d, tf), lambda i, f: (0, f)),
                  pl.BlockSpec((d, tf), lambda i, f: (0, f)),
                  pl.BlockSpec((tf, d), lambda i, f: (f, 0)),
                  pl.BlockSpec((1, d), lambda i, f: (0, 0))],
        out_specs=pl.BlockSpec((tm, d), lambda i, f: (i, 0)),
        out_shape=jax.ShapeDtypeStruct((m, d), F32),
        scratch_shapes=[pltpu.VMEM((tm, d), BF16), pltpu.VMEM((tm, d), F32)],
        compiler_params=_cparams(("parallel", "arbitrary")),
        name="ffn_mixer",
    )(x, gain.astype(F32).reshape(1, d), w_gate.astype(BF16), w_up.astype(BF16), w_down.astype(BF16), gf)


def _moe_kernel(x_ref, g_ref, r_ref, rb_ref, wg_ref, wu_ref, wd_ref, gf_ref, o_ref,
                h_ref, acc_ref, gates_ref, posc_ref, posr_ref, hx_ref, y_ref, nblk_ref, *,
                n_exp, final_norm, rb):
    e = pl.program_id(1)
    f = pl.program_id(2)
    tm = x_ref.shape[0]
    lane = lax.broadcasted_iota(jnp.int32, (tm, LANES), 1)

    @pl.when((e == 0) & (f == 0))
    def _():
        h = _norm_rows(x_ref[...], g_ref[...])
        h_ref[...] = h.astype(BF16)
        acc_ref[...] = jnp.zeros_like(acc_ref)
        h_hi = h_ref[...]
        h_lo = (h - h_hi.astype(F32)).astype(BF16)
        r_hi = r_ref[0]
        logits = (jnp.dot(h_hi, r_hi, preferred_element_type=F32)
                  + jnp.dot(h_hi, r_ref[1], preferred_element_type=F32)
                  + jnp.dot(h_lo, r_hi, preferred_element_type=F32)) + rb_ref[...]
        logits = jnp.where(lane < n_exp, logits, -jnp.inf)
        m1 = jnp.max(logits, axis=1, keepdims=True)
        i1 = jnp.min(jnp.where(logits == m1, lane, LANES), axis=1, keepdims=True)
        rest = jnp.where(lane == i1, -jnp.inf, logits)
        m2 = jnp.max(rest, axis=1, keepdims=True)
        i2 = jnp.min(jnp.where(rest == m2, lane, LANES), axis=1, keepdims=True)
        e2 = jnp.exp(m2 - m1)
        w1 = 1.0 / (1.0 + e2)
        gates_ref[...] = jnp.where(lane == i1, w1, 0.0) + jnp.where(lane == i2, e2 * w1, 0.0)
        chosen = jnp.where((lane == i1) | (lane == i2), 1.0, 0.0)
        ri = lax.broadcasted_iota(jnp.int32, (tm, tm), 0)
        ci = lax.broadcasted_iota(jnp.int32, (tm, tm), 1)
        before = jnp.where(ci < ri, 1.0, 0.0).astype(BF16)
        rank = jnp.dot(before, chosen.astype(BF16), preferred_element_type=F32)
        posc = jnp.where(chosen > 0.5, rank, -1.0)
        posc_ref[...] = posc
        posr_ref[...] = posc.T
        for ee in range(n_exp):
            cnt = jnp.sum(chosen[:, ee:ee + 1]).astype(jnp.int32)
            nblk_ref[ee] = (cnt + rb - 1) // rb

    nblk = nblk_ref[e]

    @pl.when(f == 0)
    def _():
        pos_row = posr_ref[pl.ds(e, 1), :]
        h = h_ref[...]

        def gather(b, carry):
            r0 = pl.multiple_of(b * 2 * rb, 2 * rb)
            rows = (lax.broadcasted_iota(jnp.int32, (2 * rb, 1), 0) + r0).astype(F32)
            onehot = jnp.where(pos_row == rows, 1.0, 0.0).astype(BF16)
            hx_ref[pl.ds(r0, 2 * rb), :] = jnp.dot(onehot, h, preferred_element_type=F32).astype(BF16)
            y_ref[pl.ds(r0, 2 * rb), :] = jnp.zeros((2 * rb, y_ref.shape[1]), F32)
            return carry

        lax.fori_loop(0, (nblk + 1) // 2, gather, 0)

    def expert(r0, rows):
        y_ref[pl.ds(r0, rows), :] += _swiglu_tile(hx_ref[pl.ds(r0, rows), :], wg_ref, wu_ref, wd_ref, lead=(0,))

    def expert_pair(b, carry):
        expert(pl.multiple_of(b * 2 * rb, 2 * rb), 2 * rb)
        return carry

    lax.fori_loop(0, nblk // 2, expert_pair, 0)

    @pl.when(nblk % 2 == 1)
    def _():
        expert(pl.multiple_of((nblk - 1) * rb, rb), rb)

    @pl.when(f == pl.num_programs(2) - 1)
    def _():
        pos_col = jnp.sum(jnp.where(lane == e, posc_ref[...], 0.0), axis=1, keepdims=True)
        gate = jnp.sum(jnp.where(lane == e, gates_ref[...], 0.0), axis=1, keepdims=True)

        def scatter(b, carry):
            r0 = pl.multiple_of(b * 2 * rb, 2 * rb)
            cols = (lax.broadcasted_iota(jnp.int32, (1, 2 * rb), 1) + r0).astype(F32)
            onehot = jnp.where(pos_col == cols, 1.0, 0.0).astype(BF16)
            back = jnp.dot(onehot, y_ref[pl.ds(r0, 2 * rb), :].astype(BF16), preferred_element_type=F32)
            acc_ref[...] += gate * back
            return carry

        lax.fori_loop(0, (nblk + 1) // 2, scatter, 0)

    @pl.when((e == n_exp - 1) & (f == pl.num_programs(2) - 1))
    def _():
        out = x_ref[...] + acc_ref[...]
        if final_norm:
            out = _norm_rows(out, gf_ref[...])
        o_ref[...] = out


def _moe(x, gain, router, router_bias, w_gate, w_up, w_down, *, final_gain=None, tm=1024, tf=1792, rb=128):
    m, d = x.shape
    n_exp, _, ff = w_gate.shape
    tm = min(tm, m)
    tf = tf if ff % tf == 0 else min(512, ff)
    assert (tm // rb) % 2 == 0
    r = jnp.pad(router.astype(F32), ((0, 0), (0, LANES - n_exp)))
    r_hi = r.astype(BF16)
    r = jnp.stack([r_hi, (r - r_hi.astype(F32)).astype(BF16)])
    rbias = jnp.pad(router_bias.astype(F32).reshape(1, n_exp), ((0, 0), (0, LANES - n_exp)))
    final_norm = final_gain is not None
    gf = (final_gain if final_norm else gain).astype(F32).reshape(1, d)
    return pl.pallas_call(
        functools.partial(_moe_kernel, n_exp=n_exp, final_norm=final_norm, rb=rb),
        grid=(m // tm, n_exp, ff // tf),
        in_specs=[pl.BlockSpec((tm, d), lambda i, e, f: (i, 0), pipeline_mode=pl.Buffered(1)),
                  pl.BlockSpec((1, d), lambda i, e, f: (0, 0)),
                  pl.BlockSpec((2, d, LANES), lambda i, e, f: (0, 0, 0)),
                  pl.BlockSpec((1, LANES), lambda i, e, f: (0, 0)),
                  pl.BlockSpec((1, d, tf), lambda i, e, f: (e, 0, f)),
                  pl.BlockSpec((1, d, tf), lambda i, e, f: (e, 0, f)),
                  pl.BlockSpec((1, tf, d), lambda i, e, f: (e, f, 0)),
                  pl.BlockSpec((1, d), lambda i, e, f: (0, 0))],
        out_specs=pl.BlockSpec((tm, d), lambda i, e, f: (i, 0)),
        out_shape=jax.ShapeDtypeStruct((m, d), F32),
        scratch_shapes=[pltpu.VMEM((tm, d), BF16), pltpu.VMEM((tm, d), F32), pltpu.VMEM((tm, LANES), F32),
                        pltpu.VMEM((tm, LANES), F32), pltpu.VMEM((LANES, tm), F32),
                        pltpu.VMEM((tm, d), BF16), pltpu.VMEM((tm, d), F32), pltpu.SMEM((n_exp,), jnp.int32)],
        compiler_params=_cparams(("parallel", "arbitrary", "arbitrary")),
        name="moe_mixer",
    )(x, gain.astype(F32).reshape(1, d), r, rbias, w_gate.astype(BF16), w_up.astype(BF16),
      w_down.astype(BF16), gf)


def _hyena_layer(x, gain, lp, ls, nb_p, nb_s, w_in, conv_w, f_w1, f_b1, f_w2, f_b2, f_w3, f_b3, f_w4,
                 f_freq, skip, w_out, *, starts, ends):
    parts = x if isinstance(x, tuple) else (x,)
    m, d = sum(p.shape[0] for p in parts), parts[0].shape[1]
    assert nb_p == 1 and lp % (2 * DFT_N1) == 0 and ls % (2 * DFT_N1) == 0
    hp, hs = lp // DFT_N1, ls // DFT_N1
    ut = _proj(x, gain, w_in, conv_w, transpose_out=True, starts=starts, ends=ends)
    u3 = ut.reshape(3 * d, m // DFT_N1, DFT_N1)
    filt = (f_w1, f_b1, f_w2, f_b2, f_w3, f_b3, f_w4, f_freq)
    kp = _spectrum(_hyena_taps(lp, *filt, d))
    ks = _spectrum(_hyena_taps(ls, *filt, d))
    conv = functools.partial(_fftconv, d_model=d, hp=hp, hs=hs, nb=nb_s)
    z1 = conv(u3, 0, u3, d, skip[0], kp, 0, ks, 0)
    z2 = conv(z1, 0, u3, 2 * d, skip[1], kp, d, ks, d)
    return _outproj_t(z2.reshape(d, m), w_out, x)


def _gdn_layer(x, gain, w_in, conv_w, a_log, dt_bias, norm_w, w_out, *, starts, ends):
    key, val = GDN_NK * GDN_DH, GDN_NV * GDN_DH
    cd = 2 * key + val
    qkv = _proj(x, gain, w_in[:, :cd], conv_w, act="silu", starts=starts, ends=ends)
    w_ba = jnp.pad(w_in[:, cd + val:], ((0, 0), (0, LANES - 4 * GDN_NV)))
    ba = _proj(x, gain, w_ba)
    o_f, o_b = _gdn_scan(qkv, ba, a_log, dt_bias, starts=starts, ends=ends)
    return _gdn_out(o_f, o_b, x, gain, w_in[:, cd:cd + val], norm_w, w_out)


def kernel(x_prompt, x_sample, norm_mix, norm_ffn, norm_final, hy_w_in, hy_conv, hy_f_w1, hy_f_b1, hy_f_w2, hy_f_b2, hy_f_w3, hy_f_b3, hy_f_w4, hy_f_freq, hy_skip, hy_w_out, gdn_w_in, gdn_conv, gdn_a_log, gdn_dt_bias, gdn_norm, gdn_w_out, ffn_w_gate, ffn_w_up, ffn_w_down, moe_router, moe_router_bias, moe_w_gate, moe_w_up, moe_w_down):
    bp, lp, d = x_prompt.shape
    bs, ls, _ = x_sample.shape
    x = (x_prompt.reshape(bp * lp, d), x_sample.reshape(bs * ls, d))
    starts = tuple(b * lp for b in range(bp)) + tuple(bp * lp + b * ls for b in range(bs))
    ends = tuple(s + lp for s in starts[:bp]) + tuple(s + ls for s in starts[bp:])
    depth = norm_mix.shape[0]
    for i in range(depth):
        j = i // 2
        last = i == depth - 1
        if i % 2 == 0:
            x = _hyena_layer(x, norm_mix[i], lp, ls, bp, bs, hy_w_in[j], hy_conv[j], hy_f_w1[j], hy_f_b1[j],
                             hy_f_w2[j], hy_f_b2[j], hy_f_w3[j], hy_f_b3[j], hy_f_w4[j], hy_f_freq[j],
                             hy_skip[j], hy_w_out[j], starts=starts, ends=ends)
            x = _ffn(x, norm_ffn[i], ffn_w_gate[j], ffn_w_up[j], ffn_w_down[j],
                     final_gain=norm_final if last else None)
        else:
            if isinstance(x, tuple):
                x = jnp.concatenate(x, axis=0)
            x = _gdn_layer(x, norm_mix[i], gdn_w_in[j], gdn_conv[j], gdn_a_log[j], gdn_dt_bias[j],
                           gdn_norm[j], gdn_w_out[j], starts=starts, ends=ends)
            x = _moe(x, norm_ffn[i], moe_router[j], moe_router_bias[j], moe_w_gate[j], moe_w_up[j],
                     moe_w_down[j], final_gain=norm_final if last else None)
    y_prompt = x[:bp * lp].reshape(bp, lp, d)
    y_sample = x[bp * lp:].reshape(bs, ls, d)
    return (y_prompt, y_sample)
```

```python
import functools
import math

import numpy as np
import jax
import jax.numpy as jnp
from jax import lax
from jax.experimental import pallas as pl
from jax.experimental.pallas import tpu as pltpu

F32 = jnp.float32
BF16 = jnp.bfloat16

RMS_EPS = 1e-6
HY_BANDS = 16
HY_FAST_PCT = 0.3
HY_SLOW_PCT = 1.5
HY_TARGET = 1e-2
GDN_NK = 8
GDN_NV = 16
GDN_DH = 128
TOP_K = 2

LANES = 128
HALO = 16
DFT_N1 = 256
GDN_CHUNK = 64
GDN_TILE = 256
VMEM_LIMIT = 56 * 1024 * 1024

HIGHEST = lax.Precision.HIGHEST


def _cparams(sem):
    return pltpu.CompilerParams(dimension_semantics=sem, vmem_limit_bytes=VMEM_LIMIT)


def _in_set(v, values):
    r = v == values[0]
    for b in values[1:]:
        r = jnp.logical_or(r, v == b)
    return r


def _norm_rows(x, g):
    ms = jnp.mean(x * x, axis=-1, keepdims=True)
    return x * lax.rsqrt(ms + RMS_EPS) * g


def _proj_conv_kernel(*refs, tm, part_tiles, starts, ends, act, transpose_out):
    n_parts = len(part_tiles)
    g_ref, w_ref, cw_ref, o_ref, h_ref = refs[3 * n_parts:]
    i = pl.program_id(0)
    first = 0
    for p, tiles in enumerate(part_tiles):
        xp_ref, x_ref, xn_ref = refs[3 * p:3 * p + 3]

        @pl.when((pl.program_id(1) == 0) & (i >= first) & (i < first + tiles))
        def _(xp_ref=xp_ref, x_ref=x_ref, xn_ref=xn_ref):
            g = g_ref[...]
            row0 = i * tm
            hp = jnp.where(_in_set(row0, starts), 0.0, _norm_rows(xp_ref[...], g))
            hn = jnp.where(_in_set(row0 + tm, ends), 0.0, _norm_rows(xn_ref[...], g))
            h_ref[0:HALO, :] = hp.astype(BF16)
            h_ref[HALO:HALO + tm, :] = _norm_rows(x_ref[...], g).astype(BF16)
            h_ref[HALO + tm:HALO + tm + HALO, :] = hn.astype(BF16)

        first += tiles

    n = tm + 2 * HALO
    tn = w_ref.shape[1]
    sub = min(tn, 2 * LANES)

    def finish(p, c0):
        up = pltpu.roll(p, 1, axis=0)
        dn = pltpu.roll(p, n - 1, axis=0)
        cw = cw_ref[:, c0:c0 + sub]
        y = (cw[0:1, :] * up[HALO:HALO + tm, :] + cw[1:2, :] * p[HALO:HALO + tm, :]
             + cw[2:3, :] * dn[HALO:HALO + tm, :])
        if act == "silu":
            y = y * jax.nn.sigmoid(y)
        if transpose_out:
            o_ref[c0:c0 + sub, :] = y.T
        else:
            o_ref[:, c0:c0 + sub] = y

    h = h_ref[...]
    prev = None
    for c0 in range(0, tn, sub):
        p = jnp.dot(h, w_ref[:, c0:c0 + sub], preferred_element_type=F32)
        if prev is not None:
            finish(*prev)
        prev = (p, c0)
    finish(*prev)


def _proj_kernel(x_ref, g_ref, w_ref, o_ref, h_ref):
    @pl.when(pl.program_id(1) == 0)
    def _():
        h_ref[...] = _norm_rows(x_ref[...], g_ref[...]).astype(BF16)

    o_ref[...] = jnp.dot(h_ref[...], w_ref[...], preferred_element_type=F32)


def _proj(x, gain, w, conv_w=None, *, act=None, transpose_out=False, starts=(), ends=(),
          tm=1024, tn=512):
    parts = x if isinstance(x, tuple) else (x,)
    m = sum(p.shape[0] for p in parts)
    d = parts[0].shape[1]
    n = w.shape[1]
    tm = min(tm, m)
    if conv_w is not None:
        wide = [c for c in range(2 * LANES, 4 * tn + 1, 2 * LANES) if n % c == 0]
        tn = wide[-1] if wide else tn
    tn = min(tn, n)
    grid = (m // tm, n // tn)
    gain = gain.reshape(1, d).astype(F32)
    w = w.astype(BF16)
    if conv_w is None:
        assert len(parts) == 1
        return pl.pallas_call(
            _proj_kernel,
            grid=grid,
            in_specs=[pl.BlockSpec((tm, d), lambda i, j: (i, 0)),
                      pl.BlockSpec((1, d), lambda i, j: (0, 0)),
                      pl.BlockSpec((d, tn), lambda i, j: (0, j))],
            out_specs=pl.BlockSpec((tm, tn), lambda i, j: (i, j)),
            out_shape=jax.ShapeDtypeStruct((m, n), F32),
            scratch_shapes=[pltpu.VMEM((tm, d), BF16)],
            compiler_params=_cparams(("parallel", "arbitrary")),
            name="proj",
        )(parts[0], gain, w)
    hb = tm // HALO
    part_tiles = tuple(p.shape[0] // tm for p in parts)
    assert all(p.shape[0] % tm == 0 for p in parts)
    assert all(sum(part_tiles[:k + 1]) * tm in ends for k in range(len(parts)))
    kern = functools.partial(_proj_conv_kernel, tm=tm, part_tiles=part_tiles, starts=tuple(starts),
                             ends=tuple(ends), act=act, transpose_out=transpose_out)
    x_specs, x_args, first = [], [], 0
    for p, tiles in zip(parts, part_tiles):
        last = p.shape[0] // HALO - 1
        loc = lambda i, first=first, tiles=tiles: jnp.clip(i - first, 0, tiles - 1)
        x_specs += [pl.BlockSpec((HALO, d), lambda i, j, loc=loc: (jnp.maximum(loc(i) * hb - 1, 0), 0)),
                    pl.BlockSpec((tm, d), lambda i, j, loc=loc: (loc(i), 0)),
                    pl.BlockSpec((HALO, d), lambda i, j, loc=loc, last=last: (jnp.minimum((loc(i) + 1) * hb, last), 0))]
        x_args += [p, p, p]
        first += tiles
    if transpose_out:
        out_spec = pl.BlockSpec((tn, tm), lambda i, j: (j, i))
        out_shape = jax.ShapeDtypeStruct((n, m), F32)
    else:
        out_spec = pl.BlockSpec((tm, tn), lambda i, j: (i, j))
        out_shape = jax.ShapeDtypeStruct((m, n), F32)
    return pl.pallas_call(
        kern,
        grid=grid,
        in_specs=x_specs + [pl.BlockSpec((1, d), lambda i, j: (0, 0)),
                            pl.BlockSpec((d, tn), lambda i, j: (0, j)),
                            pl.BlockSpec((3, tn), lambda i, j: (0, j))],
        out_specs=out_spec,
        out_shape=out_shape,
        scratch_shapes=[pltpu.VMEM((tm + 2 * HALO, d), BF16)],
        compiler_params=_cparams(("parallel", "arbitrary")),
        name="proj_conv",
    )(*x_args, gain, w, conv_w.astype(F32))


def _dft_rows(n2):
    return n2 // 2 + 8


@functools.lru_cache(maxsize=None)
def _dft_consts(n2):
    n1 = DFT_N1
    n = n1 * n2
    h = n2 // 2
    kept = h + 1
    hr = _dft_rows(n2)
    k1 = np.arange(n1)
    f1 = np.exp(-2j * np.pi * np.outer(k1, k1) / n1)
    f1_fwd = np.block([[f1.real, f1.imag], [-f1.imag, f1.real]])
    f1_inv = np.block([[f1.real, -f1.imag], [f1.imag, f1.real]])
    k2 = np.arange(n2)
    f2 = np.exp(-2j * np.pi * np.outer(k2, k2) / n2)
    pad_rows = lambda a: np.concatenate([a[:kept], np.zeros((hr - kept,) + a.shape[1:])], axis=0)
    tw = pad_rows(np.exp(-2j * np.pi * np.outer(k2, k1) / n))
    lf_full = np.concatenate([pad_rows(f2.real), pad_rows(f2.imag)], axis=0)
    lf_half = lf_full[:, :h]
    weight = np.where((k2 == 0) | (k2 == h), 1.0, 2.0)[None, :] / n
    li_re = pad_rows((f2.real[:h, :] * weight).T).T
    li_im = pad_rows((f2.imag[:h, :] * weight).T).T
    li_half = np.concatenate([li_re, li_im], axis=1)
    return dict(f1_fwd=f1_fwd, f1_inv=f1_inv, twre=tw.real, twim=tw.imag,
                lf_half=lf_half, lf_full=lf_full, li_half=li_half)


def _consts_dev(n2):
    c = _dft_consts(n2)
    return dict(
        f1_fwd=jnp.asarray(c["f1_fwd"], BF16), f1_inv=jnp.asarray(c["f1_inv"], BF16),
        twre=jnp.asarray(c["twre"], F32), twim=jnp.asarray(c["twim"], F32),
        lf_half=jnp.asarray(c["lf_half"], BF16), lf_full=jnp.asarray(c["lf_full"], BF16),
        li_half=jnp.asarray(c["li_half"], BF16))


def _dft_fwd(sig, lf, twre, twim, f1f, a_scr):
    ns, two_hr, _ = a_scr.shape
    hr = two_hr // 2
    for s in range(ns):
        a_scr[s] = jnp.dot(lf, sig(s).astype(BF16), preferred_element_type=F32)
    are = a_scr[:, 0:hr, :]
    aim = a_scr[:, hr:two_hr, :]
    bre = (are * twre - aim * twim).reshape(ns * hr, DFT_N1)
    bim = (are * twim + aim * twre).reshape(ns * hr, DFT_N1)
    bcat = jnp.concatenate([bre, bim], axis=1).astype(BF16)
    cc = jnp.dot(bcat, f1f, preferred_element_type=F32)
    return cc[:, 0:DFT_N1], cc[:, DFT_N1:2 * DFT_N1]


def _dft_inv_real(yre, yim, li, twre, twim, f1i, d_scr, emit):
    ns, two_hr, _ = d_scr.shape
    hr = two_hr // 2
    ycat = jnp.concatenate([yre, yim], axis=1).astype(BF16)
    bb = jnp.dot(ycat, f1i, preferred_element_type=F32)
    bre = bb[:, 0:DFT_N1].reshape(ns, hr, DFT_N1)
    bim = bb[:, DFT_N1:2 * DFT_N1].reshape(ns, hr, DFT_N1)
    d_scr[:, 0:hr, :] = bre * twre + bim * twim
    d_scr[:, hr:two_hr, :] = bim * twre - bre * twim
    for s in range(ns):
        emit(s, jnp.dot(li, d_scr[s].astype(BF16), preferred_element_type=F32))


def _taps_kernel(bands_ref, w1t_ref, w1c_ref, w1s_ref, b1_ref, w2_ref, b2_ref, w3_ref, b3_ref,
                 fq_ref, w4_ref, dl_ref, o_ref, h_ref, tm_ref, *, seq_len, tl):
    nt = pl.program_id(0)

    @pl.when(pl.program_id(1) == 0)
    def _():
        col = nt * tl + lax.broadcasted_iota(jnp.int32, (1, tl), 1)
        pos = jnp.where(col < seq_len, col, 2 * seq_len - col)
        posf = pos.astype(F32)
        t = posf * np.float32(1.0 / (seq_len - 1))
        ang = bands_ref[...] * (posf * np.float32(2.0 * math.pi / seq_len))
        fq = fq_ref[...]
        pre = (w1t_ref[...] * t
               + jnp.dot(w1c_ref[...], jnp.cos(ang), preferred_element_type=F32, precision=HIGHEST)
               - jnp.dot(w1s_ref[...], jnp.sin(ang), preferred_element_type=F32, precision=HIGHEST))
        h = jnp.sin(fq * (pre + b1_ref[...]))
        h = jnp.sin(fq * (jnp.dot(w2_ref[...], h, preferred_element_type=F32, precision=HIGHEST) + b2_ref[...]))
        h = jnp.sin(fq * (jnp.dot(w3_ref[...], h, preferred_element_type=F32, precision=HIGHEST) + b3_ref[...]))
        h_ref[...] = h.astype(BF16)
        tm_ref[0:1, :] = t
        tm_ref[1:2, :] = jnp.where(col == seq_len, 0.0, 1.0)

    taps = jnp.dot(w4_ref[...], h_ref[...], preferred_element_type=F32)
    window = jnp.exp(-(dl_ref[...] * tm_ref[0:1, :]))
    o_ref[...] = (taps * window * tm_ref[1:2, :]).astype(BF16)


def _hyena_taps(seq_len, f_w1, f_b1, f_w2, f_b2, f_w3, f_b3, f_w4, f_freq, d_model, *, tl=2048, td=512):
    fh = f_w1.shape[1]
    n_ord = f_w4.shape[1] // (2 * d_model)
    tl = min(tl, seq_len)
    ndt = d_model // td
    nlt = seq_len // tl
    col = lambda v: v.reshape(-1, 1).astype(F32)
    bands = jnp.linspace(1e-4, HY_BANDS - 1, HY_BANDS, dtype=F32).reshape(-1, 1)
    w1 = f_w1.astype(F32).T
    w4t = f_w4.astype(F32).reshape(fh, n_ord, 2, d_model).transpose(1, 2, 3, 0).reshape(n_ord * 2 * d_model, fh)
    w4t = w4t.astype(BF16)
    max_decay = math.log(HY_TARGET) / HY_FAST_PCT
    min_decay = math.log(HY_TARGET) / HY_SLOW_PCT
    deltas = jnp.abs(jnp.linspace(min_decay, max_decay, d_model, dtype=F32))
    deltas = jnp.tile(deltas, n_ord).reshape(-1, 1)
    small = lambda a: pl.BlockSpec(a.shape, lambda n, j: (0, 0))
    args = [bands, w1[:, 0:1], w1[:, 1:1 + HY_BANDS], w1[:, 1 + HY_BANDS:1 + 2 * HY_BANDS], col(f_b1),
            f_w2.astype(F32).T, col(f_b2), f_w3.astype(F32).T, col(f_b3), col(f_freq)]
    return pl.pallas_call(
        functools.partial(_taps_kernel, seq_len=seq_len, tl=tl),
        grid=(2 * nlt, n_ord * ndt),
        in_specs=[small(a) for a in args] + [
            pl.BlockSpec((td, fh), lambda n, j: ((j // ndt * 2 + (n >= nlt).astype(jnp.int32)) * ndt + j % ndt, 0)),
            pl.BlockSpec((td, 1), lambda n, j: (j, 0))],
        out_specs=pl.BlockSpec((td, tl), lambda n, j: (j, n)),
        out_shape=jax.ShapeDtypeStruct((n_ord * d_model, 2 * seq_len), BF16),
        scratch_shapes=[pltpu.VMEM((fh, tl), BF16), pltpu.VMEM((8, tl), F32)],
        compiler_params=_cparams(("parallel", "arbitrary")),
        name="hyena_taps",
    )(*args, w4t, deltas)


def _spectrum_kernel(x_ref, lf_ref, twre_ref, twim_ref, f1f_ref, re_ref, im_ref, a_scr):
    ct, hr, _ = re_ref.shape
    cre, cim = _dft_fwd(lambda s: x_ref[s], lf_ref[...], twre_ref[...][None], twim_ref[...][None],
                        f1f_ref[...], a_scr)
    re_ref[...] = cre.reshape(ct, hr, DFT_N1)
    im_ref[...] = cim.reshape(ct, hr, DFT_N1)


def _spectrum(taps, *, ct=32):
    c, n = taps.shape
    n2 = n // DFT_N1
    hr = _dft_rows(n2)
    cs = _consts_dev(n2)
    x = taps.reshape(c, n2, DFT_N1)
    blk = pl.BlockSpec((ct, hr, DFT_N1), lambda i: (i, 0, 0))
    full = lambda a: pl.BlockSpec(a.shape, lambda i: (0,) * a.ndim)
    consts = [cs["lf_full"], cs["twre"], cs["twim"], cs["f1_fwd"]]
    return pl.pallas_call(
        _spectrum_kernel,
        grid=(c // ct,),
        in_specs=[pl.BlockSpec((ct, n2, DFT_N1), lambda i: (i, 0, 0))] + [full(a) for a in consts],
        out_specs=[blk, blk],
        out_shape=[jax.ShapeDtypeStruct((c, hr, DFT_N1), F32)] * 2,
        scratch_shapes=[pltpu.VMEM((ct, 2 * hr, DFT_N1), F32)],
        compiler_params=_cparams(("parallel",)),
        name="hyena_spectrum",
    )(x, *consts)


def _fftconv_kernel(x_ref, gate_ref, skip_ref,
                    kpre_ref, kpim_ref, lfp_ref, lip_ref, twpre_ref, twpim_ref,
                    ksre_ref, ksim_ref, lfs_ref, lis_ref, twsre_ref, twsim_ref,
                    f1f_ref, f1i_ref, o_ref, ap_scr, dp_scr, as_scr, ds_scr, *, hp, hs, nb):
    ct = x_ref.shape[0]
    f1f = f1f_ref[...]
    f1i = f1i_ref[...]

    def run(rows, n_sig, kre, kim, lf, li, twre, twim, a_scr, d_scr):
        def sig(s):
            c, r0, h = rows(s)
            return x_ref[c, r0:r0 + h, :]

        cre, cim = _dft_fwd(sig, lf, twre, twim, f1f, a_scr)
        yre = cre * kre - cim * kim
        yim = cre * kim + cim * kre

        def emit(s, y):
            c, r0, h = rows(s)
            xs = x_ref[c, r0:r0 + h, :]
            o_ref[c, r0:r0 + h, :] = gate_ref[c, r0:r0 + h, :] * (y + skip_ref[c] * xs)

        _dft_inv_real(yre, yim, li, twre, twim, f1i, d_scr, emit)

    hrp = kpre_ref.shape[1]
    run(lambda s: (s, 0, hp), ct,
        kpre_ref[...].reshape(ct * hrp, DFT_N1), kpim_ref[...].reshape(ct * hrp, DFT_N1),
        lfp_ref[...], lip_ref[...], twpre_ref[...][None], twpim_ref[...][None], ap_scr, dp_scr)
    hrs = ksre_ref.shape[1]
    rep = lambda k: jnp.broadcast_to(k[:, None], (ct, nb, hrs, DFT_N1)).reshape(ct * nb * hrs, DFT_N1)
    run(lambda s: (s // nb, hp + (s % nb) * hs, hs), ct * nb,
        rep(ksre_ref[...]), rep(ksim_ref[...]),
        lfs_ref[...], lis_ref[...], twsre_ref[...][None], twsim_ref[...][None], as_scr, ds_scr)


def _fftconv(x, x_off, gate, gate_off, skip, kp, kp_off, ks, ks_off, *, d_model, hp, hs, nb, ct=8):
    r = hp + nb * hs
    cp, cs_ = _consts_dev(2 * hp), _consts_dev(2 * hs)
    hrp, hrs = _dft_rows(2 * hp), _dft_rows(2 * hs)
    cb = lambda off: (lambda i: (off // ct + i, 0, 0))
    full = lambda a: pl.BlockSpec(a.shape, lambda i: (0,) * a.ndim)
    kspec = lambda rows, off: pl.BlockSpec((ct, rows, DFT_N1), cb(off))
    skip3 = jnp.broadcast_to(skip.astype(F32).reshape(d_model, 1, 1), (d_model, 1, DFT_N1))
    pc = [cp["lf_half"], cp["li_half"], cp["twre"], cp["twim"]]
    sc = [cs_["lf_half"], cs_["li_half"], cs_["twre"], cs_["twim"]]
    return pl.pallas_call(
        functools.partial(_fftconv_kernel, hp=hp, hs=hs, nb=nb),
        grid=(d_model // ct,),
        in_specs=[pl.BlockSpec((ct, r, DFT_N1), cb(x_off)), pl.BlockSpec((ct, r, DFT_N1), cb(gate_off)),
                  pl.BlockSpec((ct, 1, DFT_N1), cb(0)),
                  kspec(hrp, kp_off), kspec(hrp, kp_off)] + [full(a) for a in pc]
                 + [kspec(hrs, ks_off), kspec(hrs, ks_off)] + [full(a) for a in sc]
                 + [full(cp["f1_fwd"]), full(cp["f1_inv"])],
        out_specs=pl.BlockSpec((ct, r, DFT_N1), cb(0)),
        out_shape=jax.ShapeDtypeStruct((d_model, r, DFT_N1), F32),
        scratch_shapes=[pltpu.VMEM((ct, 2 * hrp, DFT_N1), F32), pltpu.VMEM((ct, 2 * hrp, DFT_N1), F32),
                        pltpu.VMEM((ct * nb, 2 * hrs, DFT_N1), F32), pltpu.VMEM((ct * nb, 2 * hrs, DFT_N1), F32)],
        compiler_params=_cparams(("parallel",)),
        name="hyena_fftconv",
    )(x, gate, skip3, kp[0], kp[1], *pc, ks[0], ks[1], *sc, cp["f1_fwd"], cp["f1_inv"])


def _outproj_t_kernel(zt_ref, w_ref, *refs, part_tiles):
    res_refs, o_ref = refs[:-1], refs[-1]
    i = pl.program_id(0)
    z = zt_ref[...].astype(BF16)
    y = lax.dot_general(z, w_ref[...], (((0,), (0,)), ((), ())), preferred_element_type=F32)
    first = 0
    for res_ref, tiles in zip(res_refs, part_tiles):
        @pl.when((i >= first) & (i < first + tiles))
        def _(res_ref=res_ref):
            o_ref[...] = res_ref[...] + y

        first += tiles


def _outproj_t(zt, w, res, *, tm=1024):
    parts = res if isinstance(res, tuple) else (res,)
    k, m = zt.shape
    n = w.shape[1]
    tm = min(tm, m)
    part_tiles = tuple(p.shape[0] // tm for p in parts)
    assert all(p.shape[0] % tm == 0 for p in parts) and sum(part_tiles) * tm == m
    res_specs, first = [], 0
    for tiles in part_tiles:
        res_specs.append(pl.BlockSpec((tm, n), lambda i, first=first, tiles=tiles: (jnp.clip(i - first, 0, tiles - 1), 0)))
        first += tiles
    return pl.pallas_call(
        functools.partial(_outproj_t_kernel, part_tiles=part_tiles),
        grid=(m // tm,),
        in_specs=[pl.BlockSpec((k, tm), lambda i: (0, i)),
                  pl.BlockSpec((k, n), lambda i: (0, 0))] + res_specs,
        out_specs=pl.BlockSpec((tm, n), lambda i: (i, 0)),
        out_shape=jax.ShapeDtypeStruct((m, n), F32),
        compiler_params=_cparams(("parallel",)),
        name="hyena_outproj",
    )(zt, w.astype(BF16), *parts)


def _softplus(x):
    return jnp.maximum(x, 0.0) + jnp.log1p(jnp.exp(-jnp.abs(x)))


def _gdn_kernel(qf_ref, kf_ref, vf_ref, baf_ref, batf_ref, qb_ref, kb_ref, vb_ref, bab_ref, batb_ref,
                alog_r_ref, dtb_r_ref, alog_c_ref, dtb_c_ref, of_ref, ob_ref, sf_ref, sb_ref, *,
                n_steps, f_resets, b_resets, n_pairs, n_sub):
    hp = pl.program_id(0)
    step = pl.program_id(1)
    t = GDN_TILE
    c = GDN_CHUNK
    nc = t // c
    dh = GDN_DH

    @pl.when(_in_set(step, f_resets))
    def _():
        sf_ref[...] = jnp.zeros_like(sf_ref)

    @pl.when(_in_set(n_steps - 1 - step, b_resets))
    def _():
        sb_ref[...] = jnp.zeros_like(sb_ref)

    ri = lax.broadcasted_iota(jnp.int32, (t, t), 0)
    ci = lax.broadcasted_iota(jnp.int32, (t, t), 1)
    same = (ri // c) == (ci // c)
    rw = lax.broadcasted_iota(jnp.int32, (c, t), 0)
    lw = lax.broadcasted_iota(jnp.int32, (c, t), 1)
    lblk = lw // c
    lsub = lw % c
    rsub = lax.broadcasted_iota(jnp.int32, (t, LANES), 0) % c
    lane = lax.broadcasted_iota(jnp.int32, (t, LANES), 1)
    lrow = lax.broadcasted_iota(jnp.int32, (8, t), 1) % c

    def diag_to_wide(full):
        parts = [jnp.where(lblk == j, full[j * c:(j + 1) * c, :], 0.0) for j in range(nc)]
        return functools.reduce(lambda x, y: x + y, parts)

    def wide_to_diag16(wide):
        return jnp.where(same, jnp.concatenate([wide] * nc, axis=0), 0.0).astype(BF16)

    def l2n(x):
        return x * lax.rsqrt(jnp.sum(x * x, axis=-1, keepdims=True) + RMS_EPS)

    def mm(a, b):
        return jnp.dot(a, b, preferred_element_type=F32)

    dirs = [(0, qf_ref, kf_ref, vf_ref, baf_ref, batf_ref), (1, qb_ref, kb_ref, vb_ref, bab_ref, batb_ref)]

    def stage1_dir(d, r0):
        _, _, _, v_ref, ba_ref, bat_ref = dirs[d]
        back = d == 1
        ba = ba_ref[r0:r0 + t, :]
        g_all = -jnp.exp(alog_r_ref[...]) * _softplus(ba + dtb_r_ref[...])
        sh = 1
        while sh < c:
            if back:
                g_all = g_all + jnp.where(rsub < c - sh, pltpu.roll(g_all, t - sh, axis=0), 0.0)
            else:
                g_all = g_all + jnp.where(rsub >= sh, pltpu.roll(g_all, sh, axis=0), 0.0)
            sh *= 2
        return dict(back=back, incl=(lsub >= rw) if back else (lsub <= rw),
                    strict=(lsub > rw) if back else (lsub < rw),
                    beta_all=jax.nn.sigmoid(ba), g_all=g_all, bat_ref=bat_ref, v_ref=v_ref, r0=r0)

    def stage1_pair(d, pp, r0):
        _, q_ref, k_ref, _, _, _ = dirs[d]
        q = l2n(q_ref[r0:r0 + t, pp * dh:(pp + 1) * dh]) * np.float32(GDN_DH ** -0.5)
        k = l2n(k_ref[r0:r0 + t, pp * dh:(pp + 1) * dh])
        k16 = k.astype(BF16)
        qk_kk = lax.dot_general(jnp.concatenate([q.astype(BF16), k16], axis=0), k16, (((1,), (1,)), ((), ())),
                                preferred_element_type=F32)
        return dict(q=q, k=k, qk=diag_to_wide(qk_kk[0:t]), kk=diag_to_wide(qk_kk[t:2 * t]))

    def stage2(d, pd, pr, ph):
        back = pd["back"]
        head = 2 * n_pairs * hp + ph
        jb = d * GDN_NV + head
        ja = 2 * GDN_NV + d * GDN_NV + head
        beta_c = jnp.sum(jnp.where(lane == jb, pd["beta_all"], 0.0), axis=1, keepdims=True)
        gc_c = jnp.sum(jnp.where(lane == ja, pd["g_all"], 0.0), axis=1, keepdims=True)
        a_row = pd["bat_ref"][pl.ds(ja, 1), pd["r0"]:pd["r0"] + t]
        g_row = -jnp.exp(alog_c_ref[pl.ds(ja, 1), :]) * _softplus(a_row + dtb_c_ref[pl.ds(ja, 1), :])
        g_row = jnp.broadcast_to(g_row, (8, t))
        sh = 1
        while sh < c:
            if back:
                g_row = g_row + jnp.where(lrow < c - sh, pltpu.roll(g_row, t - sh, axis=1), 0.0)
            else:
                g_row = g_row + jnp.where(lrow >= sh, pltpu.roll(g_row, sh, axis=1), 0.0)
            sh *= 2
        gc_r = g_row[0:1, :]
        incl = pd["incl"]
        decay = jnp.where(incl, jnp.exp(jnp.where(incl, diag_to_wide(gc_c) - gc_r, 0.0)), 0.0)
        p = jnp.where(pd["strict"], -(diag_to_wide(beta_c) * pr["kk"] * decay), 0.0)
        return dict(pd=pd, pr=pr, ph=ph, beta_c=beta_c, gc_c=gc_c, decay=decay, p=p)

    def stage3(group):
        for v in group:
            v["nn"] = v["p"]
            v["pm"] = mm(v["p"].astype(BF16), wide_to_diag16(v["p"]))
            yield
        m = 2
        while 2 * m < c:
            for v in group:
                both = mm(jnp.concatenate([v["pm"], v["nn"]], axis=0).astype(BF16), wide_to_diag16(v["pm"]))
                v["nn"] = v["nn"] + v["pm"] + both[c:2 * c]
                v["pm"] = both[0:c]
                yield
            m *= 2
        for v in group:
            v["nn"] = v["nn"] + v["pm"] + mm(v["nn"].astype(BF16), wide_to_diag16(v["pm"]))
            yield

    def stage4(group):
        for v in group:
            pd, pr, ph, beta_c, gc_c = v["pd"], v["pr"], v["ph"], v["beta_c"], v["gc_c"]
            vv = pd["v_ref"][pd["r0"]:pd["r0"] + t, ph * dh:(ph + 1) * dh]
            rhs = jnp.concatenate([vv * beta_c, pr["k"] * (beta_c * jnp.exp(gc_c))], axis=1)
            v["uw16"] = (rhs + mm(wide_to_diag16(v["nn"]), rhs.astype(BF16))).astype(BF16)
        for v in group:
            pd, pr, gc_c = v["pd"], v["pr"], v["gc_c"]
            qq = mm(wide_to_diag16(jnp.where(pd["incl"], pr["qk"] * v["decay"], 0.0)), v["uw16"])
            gc3 = gc_c.reshape(nc, c, 1)
            gl3 = gc3[:, 0:1, :] if pd["back"] else gc3[:, c - 1:c, :]
            kg16 = (pr["k"] * jnp.exp(jnp.broadcast_to(gl3, (nc, c, 1)).reshape(t, 1) - gc_c)).astype(BF16)
            v["qp16"] = (pr["q"] * jnp.exp(gc_c) - qq[:, dh:2 * dh]).astype(BF16)
            v["qq"] = qq
            v["gl3"] = gl3
            v["rp"] = [lax.dot_general(kg16[n * c:(n + 1) * c], v["uw16"][n * c:(n + 1) * c],
                                       (((0,), (0,)), ((), ())), preferred_element_type=F32)
                       for n in range(nc)]

    units = [(sub, d) for sub in range(n_sub) for d in range(2)]
    row0 = lambda sub, d: (n_sub - 1 - sub if d == 1 else sub) * t
    var = {}
    groups = {u: [] for u in units}

    def setup(u):
        sub, d = u
        pd = stage1_dir(d, row0(sub, d))
        yield
        for pp in range(n_pairs):
            pr = stage1_pair(d, pp, row0(sub, d))
            yield
            for ph in (2 * pp, 2 * pp + 1):
                var[(sub, d, ph)] = stage2(d, pd, pr, ph)
                groups[u].append(var[(sub, d, ph)])
                yield

    def run(main, filler=None, every=1):
        for i, _ in enumerate(main):
            if filler is not None and i % every == every - 1:
                next(filler, None)
        if filler is not None:
            for _ in filler:
                pass

    zero = jnp.zeros((dh, dh), F32)
    s_refs = [sf_ref, sb_ref]
    outs = [of_ref, ob_ref]
    state = [[s_refs[d][ph] for ph in range(2 * n_pairs)] for d in range(2)]

    def stage5(sub):
        for i in range(nc):
            for d in range(2):
                n = nc - 1 - i if d == 1 else i
                sl = slice(n * c, (n + 1) * c)
                so = slice(row0(sub, d) + n * c, row0(sub, d) + (n + 1) * c)
                for pp in range(n_pairs):
                    v0, v1 = var[(sub, d, 2 * pp)], var[(sub, d, 2 * pp + 1)]
                    s0, s1 = state[d][2 * pp], state[d][2 * pp + 1]
                    s_d = jnp.concatenate([jnp.concatenate([s0, zero], axis=1),
                                           jnp.concatenate([zero, s1], axis=1)], axis=0).astype(BF16)
                    lhs = jnp.concatenate([
                        jnp.concatenate([v0["rp"][n][:, dh:2 * dh], v1["rp"][n][:, dh:2 * dh]], axis=1).astype(BF16),
                        jnp.concatenate([v0["qp16"][sl], v1["qp16"][sl]], axis=1)], axis=0)
                    z = mm(lhs, s_d)
                    c0 = 2 * pp * dh
                    outs[d][so, c0:c0 + dh] = z[dh:dh + c, 0:dh] + v0["qq"][sl, 0:dh]
                    outs[d][so, c0 + dh:c0 + 2 * dh] = z[dh:dh + c, dh:2 * dh] + v1["qq"][sl, 0:dh]
                    state[d][2 * pp] = jnp.exp(v0["gl3"][n]) * s0 - z[0:dh, 0:dh] + v0["rp"][n][:, 0:dh]
                    state[d][2 * pp + 1] = jnp.exp(v1["gl3"][n]) * s1 - z[0:dh, dh:2 * dh] + v1["rp"][n][:, 0:dh]

    n_setup = 1 + 3 * n_pairs
    n_doubling = 2 * n_pairs * (c.bit_length() - 1)
    run(setup(units[0]))
    for i, u in enumerate(units):
        filler = setup(units[i + 1]) if i + 1 < len(units) else None
        run(stage3(groups[u]), filler, every=max(1, n_doubling // n_setup))
        if u[1] == 1:
            stage4(groups[(u[0], 0)])
            stage4(groups[(u[0], 1)])
            stage5(u[0])
    for d in range(2):
        for ph in range(2 * n_pairs):
            s_refs[d][ph] = state[d][ph]


def _gdn_scan(qkv, ba, a_log, dt_bias, *, starts, ends, n_pairs=4, n_sub=2):
    m = qkv.shape[0]
    t = n_sub * GDN_TILE
    assert all(s % t == 0 for s in starts) and all(e % t == 0 for e in ends)
    n_tiles = m // t
    bat = ba.T
    pad = lambda a: jnp.pad(a.astype(F32).reshape(-1), (2 * GDN_NV, LANES - 4 * GDN_NV))
    alog_r = pad(a_log).reshape(1, LANES)
    dtb_r = pad(dt_bias).reshape(1, LANES)
    alog_c = pad(a_log).reshape(LANES, 1)
    dtb_c = pad(dt_bias).reshape(LANES, 1)
    f_resets = tuple(s // t for s in starts)
    b_resets = tuple(e // t - 1 for e in ends)
    fwd = lambda h, s: s
    bwd = lambda h, s: n_tiles - 1 - s

    kw = n_pairs * GDN_DH
    key_blocks = GDN_NK // n_pairs

    def specs(tile):
        return [pl.BlockSpec((t, kw), lambda h, s: (tile(h, s), h)),
                pl.BlockSpec((t, kw), lambda h, s: (tile(h, s), key_blocks + h)),
                pl.BlockSpec((t, 2 * kw), lambda h, s: (tile(h, s), key_blocks + h)),
                pl.BlockSpec((t, LANES), lambda h, s: (tile(h, s), 0)),
                pl.BlockSpec((LANES, t), lambda h, s: (0, tile(h, s)))]

    small = lambda a: pl.BlockSpec(a.shape, lambda h, s: (0, 0))
    out = jax.ShapeDtypeStruct((m, GDN_NV * GDN_DH), F32)
    return pl.pallas_call(
        functools.partial(_gdn_kernel, n_steps=n_tiles, f_resets=f_resets, b_resets=b_resets, n_pairs=n_pairs,
                          n_sub=n_sub),
        grid=(key_blocks, n_tiles),
        in_specs=specs(fwd) + specs(bwd) + [small(alog_r), small(dtb_r), small(alog_c), small(dtb_c)],
        out_specs=[pl.BlockSpec((t, 2 * kw), lambda h, s: (s, h)),
                   pl.BlockSpec((t, 2 * kw), lambda h, s: (n_tiles - 1 - s, h))],
        out_shape=[out, out],
        scratch_shapes=[pltpu.VMEM((2 * n_pairs, GDN_DH, GDN_DH), F32),
                        pltpu.VMEM((2 * n_pairs, GDN_DH, GDN_DH), F32)],
        compiler_params=_cparams(("parallel", "arbitrary")),
        name="gdn_scan",
    )(qkv, qkv, qkv, ba, bat, qkv, qkv, qkv, ba, bat, alog_r, dtb_r, alog_c, dtb_c)


def _gdn_out_kernel(of_ref, ob_ref, x_ref, g_ref, wz_ref, nw_ref, w_ref, o_ref, y_ref):
    nw = nw_ref[...]
    x = x_ref[...]
    h16 = _norm_rows(x, g_ref[...]).astype(BF16)
    for h in range(GDN_NV):
        sl = slice(h * GDN_DH, (h + 1) * GDN_DH)
        if h % 2 == 0:
            z2 = jnp.dot(h16, wz_ref[:, h * GDN_DH:(h + 2) * GDN_DH], preferred_element_type=F32)
        z = z2[:, (h % 2) * GDN_DH:(h % 2 + 1) * GDN_DH]
        o = of_ref[:, sl] + ob_ref[:, sl]
        o = o * lax.rsqrt(jnp.mean(o * o, axis=-1, keepdims=True) + RMS_EPS) * nw
        y_ref[:, sl] = (o * (z * jax.nn.sigmoid(z))).astype(BF16)
    o_ref[...] = x + jnp.dot(y_ref[...], w_ref[...], preferred_element_type=F32)


def _gdn_out(o_f, o_b, x, gain, w_z, norm_w, w, *, tm=512):
    m, kv = o_f.shape
    d = x.shape[1]
    n = w.shape[1]
    tm = min(tm, m)
    row = lambda width: pl.BlockSpec((tm, width), lambda i: (i, 0))
    whole = lambda a: pl.BlockSpec(a.shape, lambda i: (0, 0))
    args = [gain.astype(F32).reshape(1, d), w_z.astype(BF16), norm_w.astype(F32).reshape(1, GDN_DH), w.astype(BF16)]
    return pl.pallas_call(
        _gdn_out_kernel,
        grid=(m // tm,),
        in_specs=[row(kv), row(kv), row(d)] + [whole(a) for a in args],
        out_specs=row(n),
        out_shape=jax.ShapeDtypeStruct((m, n), F32),
        scratch_shapes=[pltpu.VMEM((tm, kv), BF16)],
        compiler_params=_cparams(("parallel",)),
        name="gdn_out",
    )(o_f, o_b, x, *args)


def _swiglu_tile(h, wg_ref, wu_ref, wd_ref, lead=()):
    tf = wg_ref.shape[-1]
    sub = 2 * LANES if tf % (2 * LANES) == 0 else tf
    y = None
    prev = None

    def finish(a, u, c0):
        act = (a * jax.nn.sigmoid(a) * u).astype(BF16)
        return jnp.dot(act, wd_ref[lead + (slice(c0, c0 + sub), slice(None))], preferred_element_type=F32)

    for c0 in range(0, tf, sub):
        cols = lead + (slice(None), slice(c0, c0 + sub))
        a = jnp.dot(h, wg_ref[cols], preferred_element_type=F32)
        u = jnp.dot(h, wu_ref[cols], preferred_element_type=F32)
        if prev is not None:
            part = finish(*prev)
            y = part if y is None else y + part
        prev = (a, u, c0)
    part = finish(*prev)
    return part if y is None else y + part


def _ffn_kernel(x_ref, g_ref, wg_ref, wu_ref, wd_ref, gf_ref, o_ref, h_ref, acc_ref, *, final_norm):
    f = pl.program_id(1)

    @pl.when(f == 0)
    def _():
        h_ref[...] = _norm_rows(x_ref[...], g_ref[...]).astype(BF16)
        acc_ref[...] = jnp.zeros_like(acc_ref)

    acc_ref[...] += _swiglu_tile(h_ref[...], wg_ref, wu_ref, wd_ref)

    @pl.when(f == pl.num_programs(1) - 1)
    def _():
        out = x_ref[...] + acc_ref[...]
        if final_norm:
            out = _norm_rows(out, gf_ref[...])
        o_ref[...] = out


def _ffn(x, gain, w_gate, w_up, w_down, *, final_gain=None, tm=1024, tf=1792):
    m, d = x.shape
    ff = w_gate.shape[1]
    tm = min(tm, m)
    tf = min(tf, ff)
    final_norm = final_gain is not None
    gf = (final_gain if final_norm else gain).astype(F32).reshape(1, d)
    return pl.pallas_call(
        functools.partial(_ffn_kernel, final_norm=final_norm),
        grid=(m // tm, ff // tf),
        in_specs=[pl.BlockSpec((tm, d), lambda i, f: (i, 0)),
                  pl.BlockSpec((1, d), lambda i, f: (0, 0)),
                  pl.BlockSpec((d, tf), lambda i, f: (0, f)),
                  pl.BlockSpec((d, tf), lambda i, f: (0, f)),
                  pl.BlockSpec((tf, d), lambda i, f: (f, 0)),
                  pl.BlockSpec((1, d), lambda i, f: (0, 0))],
        out_specs=pl.BlockSpec((tm, d), lambda i, f: (i, 0)),
        out_shape=jax.ShapeDtypeStruct((m, d), F32),
        scratch_shapes=[pltpu.VMEM((tm, d), BF16), pltpu.VMEM((tm, d), F32)],
        compiler_params=_cparams(("parallel", "arbitrary")),
        name="ffn_mixer",
    )(x, gain.astype(F32).reshape(1, d), w_gate.astype(BF16), w_up.astype(BF16), w_down.astype(BF16), gf)


def _moe_kernel(x_ref, g_ref, r_ref, rb_ref, wg_ref, wu_ref, wd_ref, gf_ref, o_ref,
                h_ref, acc_ref, gates_ref, posc_ref, posr_ref, hx_ref, y_ref, nblk_ref, *,
                n_exp, final_norm, rb):
    e = pl.program_id(1)
    f = pl.program_id(2)
    tm = x_ref.shape[0]
    lane = lax.broadcasted_iota(jnp.int32, (tm, LANES), 1)

    @pl.when((e == 0) & (f == 0))
    def _():
        h = _norm_rows(x_ref[...], g_ref[...])
        h_ref[...] = h.astype(BF16)
        acc_ref[...] = jnp.zeros_like(acc_ref)
        h_hi = h_ref[...]
        h_lo = (h - h_hi.astype(F32)).astype(BF16)
        r_hi = r_ref[0]
        logits = (jnp.dot(h_hi, r_hi, preferred_element_type=F32)
                  + jnp.dot(h_hi, r_ref[1], preferred_element_type=F32)
                  + jnp.dot(h_lo, r_hi, preferred_element_type=F32)) + rb_ref[...]
        logits = jnp.where(lane < n_exp, logits, -jnp.inf)
        m1 = jnp.max(logits, axis=1, keepdims=True)
        i1 = jnp.min(jnp.where(logits == m1, lane, LANES), axis=1, keepdims=True)
        rest = jnp.where(lane == i1, -jnp.inf, logits)
        m2 = jnp.max(rest, axis=1, keepdims=True)
        i2 = jnp.min(jnp.where(rest == m2, lane, LANES), axis=1, keepdims=True)
        e2 = jnp.exp(m2 - m1)
        w1 = 1.0 / (1.0 + e2)
        gates_ref[...] = jnp.where(lane == i1, w1, 0.0) + jnp.where(lane == i2, e2 * w1, 0.0)
        chosen = jnp.where((lane == i1) | (lane == i2), 1.0, 0.0)
        ri = lax.broadcasted_iota(jnp.int32, (tm, tm), 0)
        ci = lax.broadcasted_iota(jnp.int32, (tm, tm), 1)
        before = jnp.where(ci < ri, 1.0, 0.0).astype(BF16)
        rank = jnp.dot(before, chosen.astype(BF16), preferred_element_type=F32)
        posc = jnp.where(chosen > 0.5, rank, -1.0)
        posc_ref[...] = posc
        posr_ref[...] = posc.T
        for ee in range(n_exp):
            cnt = jnp.sum(chosen[:, ee:ee + 1]).astype(jnp.int32)
            nblk_ref[ee] = (cnt + rb - 1) // rb

    nblk = nblk_ref[e]

    @pl.when(f == 0)
    def _():
        pos_row = posr_ref[pl.ds(e, 1), :]
        h = h_ref[...]

        def gather(b, carry):
            r0 = pl.multiple_of(b * 2 * rb, 2 * rb)
            rows = (lax.broadcasted_iota(jnp.int32, (2 * rb, 1), 0) + r0).astype(F32)
            onehot = jnp.where(pos_row == rows, 1.0, 0.0).astype(BF16)
            hx_ref[pl.ds(r0, 2 * rb), :] = jnp.dot(onehot, h, preferred_element_type=F32).astype(BF16)
            y_ref[pl.ds(r0, 2 * rb), :] = jnp.zeros((2 * rb, y_ref.shape[1]), F32)
            return carry

        lax.fori_loop(0, (nblk + 1) // 2, gather, 0)

    def expert(r0, rows):
        y_ref[pl.ds(r0, rows), :] += _swiglu_tile(hx_ref[pl.ds(r0, rows), :], wg_ref, wu_ref, wd_ref, lead=(0,))

    def expert_pair(b, carry):
        expert(pl.multiple_of(b * 2 * rb, 2 * rb), 2 * rb)
        return carry

    lax.fori_loop(0, nblk // 2, expert_pair, 0)

    @pl.when(nblk % 2 == 1)
    def _():
        expert(pl.multiple_of((nblk - 1) * rb, rb), rb)

    @pl.when(f == pl.num_programs(2) - 1)
    def _():
        pos_col = jnp.sum(jnp.where(lane == e, posc_ref[...], 0.0), axis=1, keepdims=True)
        gate = jnp.sum(jnp.where(lane == e, gates_ref[...], 0.0), axis=1, keepdims=True)

        def scatter(b, carry):
            r0 = pl.multiple_of(b * 2 * rb, 2 * rb)
            cols = (lax.broadcasted_iota(jnp.int32, (1, 2 * rb), 1) + r0).astype(F32)
            onehot = jnp.where(pos_col == cols, 1.0, 0.0).astype(BF16)
            back = jnp.dot(onehot, y_ref[pl.ds(r0, 2 * rb), :].astype(BF16), preferred_element_type=F32)
            acc_ref[...] += gate * back
            return carry

        lax.fori_loop(0, (nblk + 1) // 2, scatter, 0)

    @pl.when((e == n_exp - 1) & (f == pl.num_programs(2) - 1))
    def _():
        out = x_ref[...] + acc_ref[...]
        if final_norm:
            out = _norm_rows(out, gf_ref[...])
        o_ref[...] = out


def _moe(x, gain, router, router_bias, w_gate, w_up, w_down, *, final_gain=None, tm=1024, tf=1792, rb=128):
    m, d = x.shape
    n_exp, _, ff = w_gate.shape
    tm = min(tm, m)
    tf = tf if ff % tf == 0 else min(512, ff)
    assert (tm // rb) % 2 == 0
    r = jnp.pad(router.astype(F32), ((0, 0), (0, LANES - n_exp)))
    r_hi = r.astype(BF16)
    r = jnp.stack([r_hi, (r - r_hi.astype(F32)).astype(BF16)])
    rbias = jnp.pad(router_bias.astype(F32).reshape(1, n_exp), ((0, 0), (0, LANES - n_exp)))
    final_norm = final_gain is not None
    gf = (final_gain if final_norm else gain).astype(F32).reshape(1, d)
    return pl.pallas_call(
        functools.partial(_moe_kernel, n_exp=n_exp, final_norm=final_norm, rb=rb),
        grid=(m // tm, n_exp, ff // tf),
        in_specs=[pl.BlockSpec((tm, d), lambda i, e, f: (i, 0), pipeline_mode=pl.Buffered(1)),
                  pl.BlockSpec((1, d), lambda i, e, f: (0, 0)),
                  pl.BlockSpec((2, d, LANES), lambda i, e, f: (0, 0, 0)),
                  pl.BlockSpec((1, LANES), lambda i, e, f: (0, 0)),
                  pl.BlockSpec((1, d, tf), lambda i, e, f: (e, 0, f)),
                  pl.BlockSpec((1, d, tf), lambda i, e, f: (e, 0, f)),
                  pl.BlockSpec((1, tf, d), lambda i, e, f: (e, f, 0)),
                  pl.BlockSpec((1, d), lambda i, e, f: (0, 0))],
        out_specs=pl.BlockSpec((tm, d), lambda i, e, f: (i, 0)),
        out_shape=jax.ShapeDtypeStruct((m, d), F32),
        scratch_shapes=[pltpu.VMEM((tm, d), BF16), pltpu.VMEM((tm, d), F32), pltpu.VMEM((tm, LANES), F32),
                        pltpu.VMEM((tm, LANES), F32), pltpu.VMEM((LANES, tm), F32),
                        pltpu.VMEM((tm, d), BF16), pltpu.VMEM((tm, d), F32), pltpu.SMEM((n_exp,), jnp.int32)],
        compiler_params=_cparams(("parallel", "arbitrary", "arbitrary")),
        name="moe_mixer",
    )(x, gain.astype(F32).reshape(1, d), r, rbias, w_gate.astype(BF16), w_up.astype(BF16),
      w_down.astype(BF16), gf)


def _hyena_layer(x, gain, lp, ls, nb_p, nb_s, w_in, conv_w, f_w1, f_b1, f_w2, f_b2, f_w3, f_b3, f_w4,
                 f_freq, skip, w_out, *, starts, ends):
    parts = x if isinstance(x, tuple) else (x,)
    m, d = sum(p.shape[0] for p in parts), parts[0].shape[1]
    assert nb_p == 1 and lp % (2 * DFT_N1) == 0 and ls % (2 * DFT_N1) == 0
    hp, hs = lp // DFT_N1, ls // DFT_N1
    ut = _proj(x, gain, w_in, conv_w, transpose_out=True, starts=starts, ends=ends)
    u3 = ut.reshape(3 * d, m // DFT_N1, DFT_N1)
    filt = (f_w1, f_b1, f_w2, f_b2, f_w3, f_b3, f_w4, f_freq)
    kp = _spectrum(_hyena_taps(lp, *filt, d))
    ks = _spectrum(_hyena_taps(ls, *filt, d))
    conv = functools.partial(_fftconv, d_model=d, hp=hp, hs=hs, nb=nb_s)
    z1 = conv(u3, 0, u3, d, skip[0], kp, 0, ks, 0)
    z2 = conv(z1, 0, u3, 2 * d, skip[1], kp, d, ks, d)
    return _outproj_t(z2.reshape(d, m), w_out, x)


def _gdn_layer(x, gain, w_in, conv_w, a_log, dt_bias, norm_w, w_out, *, starts, ends):
    key, val = GDN_NK * GDN_DH, GDN_NV * GDN_DH
    cd = 2 * key + val
    qkv = _proj(x, gain, w_in[:, :cd], conv_w, act="silu", starts=starts, ends=ends)
    w_ba = jnp.pad(w_in[:, cd + val:], ((0, 0), (0, LANES - 4 * GDN_NV)))
    ba = _proj(x, gain, w_ba)
    o_f, o_b = _gdn_scan(qkv, ba, a_log, dt_bias, starts=starts, ends=ends)
    return _gdn_out(o_f, o_b, x, gain, w_in[:, cd:cd + val], norm_w, w_out)


def kernel(x_prompt, x_sample, norm_mix, norm_ffn, norm_final, hy_w_in, hy_conv, hy_f_w1, hy_f_b1, hy_f_w2, hy_f_b2, hy_f_w3, hy_f_b3, hy_f_w4, hy_f_freq, hy_skip, hy_w_out, gdn_w_in, gdn_conv, gdn_a_log, gdn_dt_bias, gdn_norm, gdn_w_out, ffn_w_gate, ffn_w_up, ffn_w_down, moe_router, moe_router_bias, moe_w_gate, moe_w_up, moe_w_down):
    bp, lp, d = x_prompt.shape
    bs, ls, _ = x_sample.shape
    x = (x_prompt.reshape(bp * lp, d), x_sample.reshape(bs * ls, d))
    starts = tuple(b * lp for b in range(bp)) + tuple(bp * lp + b * ls for b in range(bs))
    ends = tuple(s + lp for s in starts[:bp]) + tuple(s + ls for s in starts[bp:])
    depth = norm_mix.shape[0]
    for i in range(depth):
        j = i // 2
        last = i == depth - 1
        if i % 2 == 0:
            x = _hyena_layer(x, norm_mix[i], lp, ls, bp, bs, hy_w_in[j], hy_conv[j], hy_f_w1[j], hy_f_b1[j],
                             hy_f_w2[j], hy_f_b2[j], hy_f_w3[j], hy_f_b3[j], hy_f_w4[j], hy_f_freq[j],
                             hy_skip[j], hy_w_out[j], starts=starts, ends=ends)
            x = _ffn(x, norm_ffn[i], ffn_w_gate[j], ffn_w_up[j], ffn_w_down[j],
                     final_gain=norm_final if last else None)
        else:
            if isinstance(x, tuple):
                x = jnp.concatenate(x, axis=0)
            x = _gdn_layer(x, norm_mix[i], gdn_w_in[j], gdn_conv[j], gdn_a_log[j], gdn_dt_bias[j],
                           gdn_norm[j], gdn_w_out[j], starts=starts, ends=ends)
            x = _moe(x, norm_ffn[i], moe_router[j], moe_router_bias[j], moe_w_gate[j], moe_w_up[j],
                     moe_w_down[j], final_gain=norm_final if last else None)
    y_prompt = x[:bp * lp].reshape(bp, lp, d)
    y_sample = x[bp * lp:].reshape(bs, ls, d)
    return (y_prompt, y_sample)
```
